```python
import math
import jax, jax.numpy as jnp
from jax import lax
import numpy as np

D_MODEL = 1024
BATCH = 2
SEQ = 8192
DEPTH = 2

N_MIXERS = 2
EPS = 1e-6
POOL_WINDOWS = (2, 4, 8, 16)
N_POOL_GROUPS = len(POOL_WINDOWS)
POOL_GROUP_DIM = D_MODEL // N_POOL_GROUPS
HEAD_DIM = 64
N_HEADS = D_MODEL // HEAD_DIM
D_ATTN = N_HEADS * HEAD_DIM
ATTN_PATTERNS = ((128, 1), (512, 4), (2048, 16))
N_ATTN_GROUPS = len(ATTN_PATTERNS)
HEAD_GROUPS = tuple(N_HEADS // N_ATTN_GROUPS + (1 if g < N_HEADS % N_ATTN_GROUPS else 0) for g in range(N_ATTN_GROUPS))
ROPE_THETA = 10000.0
D_FF = -(-8 * D_MODEL // (3 * 256)) * 256
N_POOL_LAYERS = (DEPTH + 1) // 2
N_ATTN_LAYERS = DEPTH // 2
NEG_INF = -1e30

kernel_name = "hybrid_pool_dilated_swa_swiglu"


def rmsnorm(x, g):
    xf = x.astype(jnp.float32)
    y = xf * lax.rsqrt(jnp.mean(xf * xf, axis=-1, keepdims=True) + EPS)
    return (y * g.astype(jnp.float32)).astype(x.dtype)


def rope(t):
    S, hd = t.shape[1], t.shape[-1]
    inv_freq = 1.0 / (ROPE_THETA ** (jnp.arange(0, hd, 2, dtype=jnp.float32) / hd))
    ang = jnp.arange(S, dtype=jnp.float32)[:, None] * inv_freq[None, :]
    ang = jnp.concatenate([ang, ang], axis=-1)[None, :, None, :]
    tf = t.astype(jnp.float32)
    t1, t2 = tf[..., : hd // 2], tf[..., hd // 2 :]
    rot = jnp.concatenate([-t2, t1], axis=-1)
    return (tf * jnp.cos(ang) + rot * jnp.sin(ang)).astype(t.dtype)


def pool_mixer(h, w_in, w_group, scale, w_out):
    B, S, _ = h.shape
    u = jnp.einsum('bsd,de->bse', h, w_in).reshape(B, S, N_POOL_GROUPS, POOL_GROUP_DIM)
    cs = jnp.cumsum(u.astype(jnp.float32), axis=1)
    pos = jnp.arange(S)
    outs = []
    for g, w in enumerate(POOL_WINDOWS):
        c = cs[:, :, g]
        lag = jnp.pad(c, ((0, 0), (w, 0), (0, 0)))[:, :S]
        cnt = jnp.minimum(pos + 1, w).astype(jnp.float32)[None, :, None]
        outs.append((c - lag) / cnt - u[:, :, g].astype(jnp.float32))
    p = jnp.stack(outs, axis=2).astype(h.dtype)
    z = jnp.einsum('bsgc,gce->bsge', p, w_group).reshape(B, S, D_MODEL) * scale
    return jnp.einsum('bsd,de->bse', z, w_out)


def dilated_window_attention(q, k, v, window, dilation):
    B, S, H, hd = q.shape
    w = window // dilation
    L = S // dilation
    nb = -(-L // w)
    Lp = nb * w

    def to_blocks(t):
        t = t.reshape(B, L, dilation, H, hd)
        t = jnp.pad(t, ((0, 0), (0, Lp - L), (0, 0), (0, 0), (0, 0)))
        return t.reshape(B, nb, w, dilation, H, hd)

    qb, kb, vb = to_blocks(q), to_blocks(k), to_blocks(v)

    def with_prev(t):
        prev = jnp.concatenate([jnp.zeros_like(t[:, :1]), t[:, :-1]], axis=1)
        return jnp.concatenate([prev, t], axis=2)

    kc, vc = with_prev(kb), with_prev(vb)
    s = jnp.einsum('bnqrhd,bnkrhd->bnrhqk', qb, kc).astype(jnp.float32)
    qi = jnp.arange(w)[:, None]
    kj = jnp.arange(2 * w)[None, :]
    dist = w + qi - kj
    key_idx = jnp.arange(nb)[:, None, None] * w - w + kj[None]
    mask = (dist >= 0)[None] & (dist <= w)[None] & (key_idx >= 0)
    s = jnp.where(mask[None, :, None, None], s, NEG_INF)
    m = jnp.max(s, axis=-1, keepdims=True)
    e = jnp.exp(s - m)
    den = jnp.sum(e, axis=-1, keepdims=True)
    p = (e / den).astype(v.dtype)
    o = jnp.einsum('bnrhqk,bnkrhd->bnqrhd', p, vc)
    lse = (m + jnp.log(den))[..., 0]
    lse = jnp.transpose(lse, (0, 1, 4, 2, 3))
    o = o.reshape(B, Lp, dilation, H, hd)[:, :L].reshape(B, S, H, hd)
    lse = lse.reshape(B, Lp, dilation, H)[:, :L].reshape(B, S, H)
    return o, lse


def attn_mixer(h, w_qkv, w_out):
    B, S, _ = h.shape
    qkv = jnp.einsum('bsd,de->bse', h, w_qkv).reshape(B, S, 3, N_HEADS, HEAD_DIM)
    q = rope(qkv[:, :, 0]) * jnp.asarray(HEAD_DIM ** -0.5, dtype=h.dtype)
    k = rope(qkv[:, :, 1])
    v = qkv[:, :, 2]
    outs, lses = [], []
    start = 0
    for (window, dilation), n_g in zip(ATTN_PATTERNS, HEAD_GROUPS):
        sl = slice(start, start + n_g)
        o_g, lse_g = dilated_window_attention(q[:, :, sl], k[:, :, sl], v[:, :, sl], window, dilation)
        outs.append(o_g)
        lses.append(jax.nn.logsumexp(lse_g, axis=-1) - math.log(n_g))
        start += n_g
    alpha = jax.nn.softmax(jnp.stack(lses, axis=-1), axis=-1)
    merged = jnp.concatenate(
        [o_g * (N_ATTN_GROUPS * alpha[:, :, g]).astype(o_g.dtype)[:, :, None, None] for g, o_g in enumerate(outs)],
        axis=2,
    ).reshape(B, S, D_ATTN)
    return jnp.einsum('bse,ed->bsd', merged, w_out)


def swiglu(h, w_gate, w_up, w_down):
    g = jnp.einsum('bsd,df->bsf', h, w_gate)
    u = jnp.einsum('bsd,df->bsf', h, w_up)
    return jnp.einsum('bsf,fd->bsd', jax.nn.silu(g) * u, w_down)


def setup_inputs(seed: int = 0) -> dict:
    key = jax.random.key(seed)
    ks = jax.random.split(key, 14)
    f32 = jnp.float32
    nrm = lambda k, shape, fan_in: jax.random.normal(k, shape, f32) * (fan_in ** -0.5)
    return {
        "x": jax.random.normal(ks[0], (BATCH, SEQ, D_MODEL), f32),
        "norm_mix": 1.0 + 0.05 * jax.random.normal(ks[1], (DEPTH, D_MODEL), f32),
        "norm_ffn": 1.0 + 0.05 * jax.random.normal(ks[2], (DEPTH, D_MODEL), f32),
        "norm_final": 1.0 + 0.05 * jax.random.normal(ks[3], (D_MODEL,), f32),
        "pool_w_in": nrm(ks[4], (N_POOL_LAYERS, D_MODEL, D_MODEL), D_MODEL),
        "pool_w_group": nrm(ks[5], (N_POOL_LAYERS, N_POOL_GROUPS, POOL_GROUP_DIM, POOL_GROUP_DIM), POOL_GROUP_DIM),
        "pool_scale": 1.0 + 0.1 * jax.random.normal(ks[6], (N_POOL_LAYERS, D_MODEL), f32),
        "pool_w_out": nrm(ks[7], (N_POOL_LAYERS, D_MODEL, D_MODEL), D_MODEL),
        "attn_w_qkv": nrm(ks[8], (N_ATTN_LAYERS, D_MODEL, 3 * D_ATTN), D_MODEL),
        "attn_w_out": nrm(ks[9], (N_ATTN_LAYERS, D_ATTN, D_MODEL), D_ATTN),
        "ffn_w_gate": nrm(ks[10], (DEPTH, D_MODEL, D_FF), D_MODEL),
        "ffn_w_up": nrm(ks[11], (DEPTH, D_MODEL, D_FF), D_MODEL),
        "ffn_w_down": nrm(ks[12], (DEPTH, D_FF, D_MODEL), D_FF),
    }


def reference(x, norm_mix, norm_ffn, norm_final, pool_w_in, pool_w_group, pool_scale, pool_w_out,
              attn_w_qkv, attn_w_out, ffn_w_gate, ffn_w_up, ffn_w_down):
    for i in range(DEPTH):
        h = rmsnorm(x, norm_mix[i])
        j = i // N_MIXERS
        if i % N_MIXERS == 0:
            y = pool_mixer(h, pool_w_in[j], pool_w_group[j], pool_scale[j], pool_w_out[j])
        else:
            y = attn_mixer(h, attn_w_qkv[j], attn_w_out[j])
        x = x + y
        h = rmsnorm(x, norm_ffn[i])
        x = x + swiglu(h, ffn_w_gate[i], ffn_w_up[i], ffn_w_down[i])
    return rmsnorm(x, norm_final)
```

```python
import functools
import math

import jax
import jax.numpy as jnp
from jax import lax
from jax.experimental import pallas as pl
from jax.experimental.pallas import tpu as pltpu

EPS = 1e-6
POOL_WINDOWS = (2, 4, 8, 16)
HEAD_DIM = 64
ATTN_PATTERNS = ((128, 1), (512, 4), (2048, 16))
ROPE_THETA = 10000.0
NEG_INF = -1e30

LANES = 128
VMEM_LIMIT_BYTES = 56 * 1024 * 1024

F32 = jnp.float32
BF16 = jnp.bfloat16


def _head_groups(n_heads):
    n = len(ATTN_PATTERNS)
    return tuple(n_heads // n + (1 if g < n_heads % n else 0) for g in range(n))


def _rmsnorm(x, g):
    ms = jnp.mean(x * x, axis=-1, keepdims=True)
    return x * lax.rsqrt(ms + EPS) * g


def _params(*sem):
    return pltpu.CompilerParams(dimension_semantics=sem, vmem_limit_bytes=VMEM_LIMIT_BYTES)


def _const_spec(shape):
    zeros = (0,) * len(shape)
    return pl.BlockSpec(shape, lambda *_: zeros)


def _pool_kernel(x_ref, g_ref, win_ref, wgrp_ref, scale_ref, wout_ref, o_ref, ext_ref, *, ts, halo):
    i = pl.program_id(1)
    d_model = x_ref.shape[-1]
    gdim = d_model // len(POOL_WINDOWS)

    @pl.when(i == 0)
    def _():
        ext_ref[0:halo, :] = jnp.zeros((halo, d_model), F32)

    x = x_ref[...]
    h = _rmsnorm(x, g_ref[...]).astype(BF16)
    u = jnp.dot(h, win_ref[...], preferred_element_type=F32)
    ext_ref[halo:halo + ts, :] = u

    pos = i * ts + lax.broadcasted_iota(jnp.int32, (ts, 1), 0)
    zs = []
    for g, w in enumerate(POOL_WINDOWS):
        cols = slice(g * gdim, (g + 1) * gdim)
        s = ext_ref[:, cols]
        k = 1
        while k < w:
            s = s + pltpu.roll(s, k, axis=0)
            k *= 2
        inv_cnt = 1.0 / jnp.minimum(pos + 1, w).astype(F32)
        p = s[halo:, :] * inv_cnt - u[:, cols]
        zs.append(jnp.dot(p.astype(BF16), wgrp_ref[g], preferred_element_type=F32))
    z = jnp.concatenate(zs, axis=1) * scale_ref[...]
    y = jnp.dot(z.astype(BF16), wout_ref[...], preferred_element_type=F32)
    o_ref[...] = x + y
    ext_ref[0:halo, :] = ext_ref[ts:ts + halo, :]


def _pool_layer(x, g, w_in, w_group, scale, w_out, *, ts=512):
    b, s, d = x.shape
    halo = max(POOL_WINDOWS)
    kern = functools.partial(_pool_kernel, ts=ts, halo=halo)
    return pl.pallas_call(
        kern,
        grid=(b, s // ts),
        in_specs=[
            pl.BlockSpec((None, ts, d), lambda bi, i: (bi, i, 0)),
            _const_spec((1, d)),
            _const_spec(w_in.shape),
            _const_spec(w_group.shape),
            _const_spec((1, d)),
            _const_spec(w_out.shape),
        ],
        out_specs=pl.BlockSpec((None, ts, d), lambda bi, i: (bi, i, 0)),
        out_shape=jax.ShapeDtypeStruct(x.shape, F32),
        scratch_shapes=[pltpu.VMEM((ts + halo, d), F32)],
        compiler_params=_params("arbitrary", "arbitrary"),
        name="pool_mixer",
    )(x, g.reshape(1, d), w_in, w_group, scale.reshape(1, d), w_out)


def _ffn_kernel(x_ref, g_ref, wg_ref, wu_ref, wd_ref, gf_ref, o_ref, *, fc, final):
    x = x_ref[...]
    h = _rmsnorm(x, g_ref[...]).astype(BF16)
    acc = x
    for c in range(wg_ref.shape[1] // fc):
        sl = slice(c * fc, (c + 1) * fc)
        gate = jnp.dot(h, wg_ref[:, sl], preferred_element_type=F32)
        up = jnp.dot(h, wu_ref[:, sl], preferred_element_type=F32)
        a = (gate * jax.nn.sigmoid(gate) * up).astype(BF16)
        acc = acc + jnp.dot(a, wd_ref[sl, :], preferred_element_type=F32)
    if final:
        acc = _rmsnorm(acc, gf_ref[...])
    o_ref[...] = acc


def _ffn_layer(x2d, g, w_gate, w_up, w_down, g_final, *, final, tm=512, fc=256):
    n, d = x2d.shape
    kern = functools.partial(_ffn_kernel, fc=fc, final=final)
    return pl.pallas_call(
        kern,
        grid=(n // tm,),
        in_specs=[
            pl.BlockSpec((tm, d), lambda i: (i, 0)),
            _const_spec((1, d)),
            _const_spec(w_gate.shape),
            _const_spec(w_up.shape),
            _const_spec(w_down.shape),
            _const_spec((1, d)),
        ],
        out_specs=pl.BlockSpec((tm, d), lambda i: (i, 0)),
        out_shape=jax.ShapeDtypeStruct(x2d.shape, F32),
        compiler_params=_params("arbitrary"),
        name="ffn_final" if final else "ffn",
    )(x2d, g.reshape(1, d), w_gate, w_up, w_down, g_final.reshape(1, d))


def _deinterleave(ref, ts, d):
    n = ts // d
    return jnp.concatenate([ref[pl.ds(r, n, stride=d), :] for r in range(d)], axis=0)


def _qkv_kernel(x_ref, g_ref, cos_ref, sin_ref, *refs, ts, groups):
    ng = len(groups)
    w_refs, out_refs, xs_ref = refs[:ng], refs[ng:-1], refs[-1]
    lane = lax.broadcasted_iota(jnp.int32, (1, LANES), 1)
    first_half = (lane % HEAD_DIM) < (HEAD_DIM // 2)
    n_chunks = x_ref.shape[-1] // LANES
    for c in range(n_chunks):
        xs_ref[c] = x_ref[:, c * LANES:(c + 1) * LANES]
    for gi, (d, cg) in enumerate(groups):
        n = ts // d
        if d == 1:
            xp, cs, sn = x_ref[...], cos_ref[...], sin_ref[...]
        else:
            xp = jnp.concatenate([_deinterleave(xs_ref.at[c], ts, d) for c in range(n_chunks)], axis=1)
            cs = _deinterleave(cos_ref, ts, d)
            sn = _deinterleave(sin_ref, ts, d)
        h = _rmsnorm(xp, g_ref[...]).astype(BF16)
        qkv = jnp.dot(h, w_refs[gi][...], preferred_element_type=F32)
        chunks = []
        for c in range(2 * cg // LANES):
            t = qkv[:, c * LANES:(c + 1) * LANES]
            swapped = jnp.where(first_half, pltpu.roll(t, LANES - HEAD_DIM // 2, axis=1),
                                pltpu.roll(t, HEAD_DIM // 2, axis=1))
            q_scale = jnp.where(c * LANES + lane < cg, HEAD_DIM ** -0.5, 1.0).astype(F32)
            chunks.append((t * cs + swapped * sn) * q_scale)
        qk = jnp.concatenate(chunks, axis=1)
        parts = (qk[:, :cg], qk[:, cg:2 * cg], qkv[:, 2 * cg:])
        for part, o_ref in zip(parts, out_refs[3 * gi:3 * gi + 3]):
            pb = part.astype(BF16)
            for r in range(d):
                o_ref[r] = pb[r * n:(r + 1) * n, :]


def _qkv_layer(x, g, cos, sin, w_groups, groups, *, ts=512):
    b, s, d_model = x.shape
    kern = functools.partial(_qkv_kernel, ts=ts, groups=groups)
    out_shapes, out_specs = [], []
    for d, cg in groups:
        for _ in range(3):
            out_shapes.append(jax.ShapeDtypeStruct((b, d, s // d, cg), BF16))
            out_specs.append(pl.BlockSpec((None, d, ts // d, cg), lambda bi, i: (bi, 0, i, 0)))
    return pl.pallas_call(
        kern,
        grid=(b, s // ts),
        in_specs=[
            pl.BlockSpec((None, ts, d_model), lambda bi, i: (bi, i, 0)),
            _const_spec((1, d_model)),
            pl.BlockSpec((ts, LANES), lambda bi, i: (i, 0)),
            pl.BlockSpec((ts, LANES), lambda bi, i: (i, 0)),
        ] + [_const_spec(w.shape) for w in w_groups],
        out_specs=out_specs,
        out_shape=out_shapes,
        scratch_shapes=[pltpu.VMEM((d_model // LANES, ts, LANES), F32)],
        compiler_params=_params("arbitrary", "arbitrary"),
        name="qkv_rope",
    )(x, g.reshape(1, d_model), cos, sin, *w_groups)


def _attn_kernel(q_ref, k_ref, v_ref, kp_ref, vp_ref, o_ref, lse_ref, *, tq, w, n_heads):
    i = pl.program_id(1)
    cg = n_heads * HEAD_DIM
    nsub = tq // w
    qi = lax.broadcasted_iota(jnp.int32, (w, w), 0)
    kj = lax.broadcasted_iota(jnp.int32, (w, w), 1)
    mask_cur = kj <= qi
    mask_prev = kj >= qi
    mask_prev0 = jnp.logical_and(mask_prev, i > 0)
    lane = lax.broadcasted_iota(jnp.int32, (1, LANES), 1)
    nt = (((1,), (1,)), ((), ()))

    lses = [[] for _ in range(nsub)]
    for c0 in range(0, cg, LANES):
        width = min(LANES, cg - c0)
        cols = slice(c0, c0 + width)
        heads = width // HEAD_DIM
        if heads == 2:
            sel = [lane < HEAD_DIM, lane >= HEAD_DIM]
        else:
            sel = [None]

        def pick(blk, m):
            return blk if m is None else jnp.where(m, blk, jnp.zeros_like(blk))

        k_prev = [pick(kp_ref[:, cols], m) for m in sel]
        v_prev = [pick(vp_ref[:, cols], m) for m in sel]
        for j in range(nsub):
            rows = slice(j * w, (j + 1) * w)
            qb = q_ref[rows, cols]
            kb, vb = k_ref[rows, cols], v_ref[rows, cols]
            k_cur = [pick(kb, m) for m in sel]
            v_cur = [pick(vb, m) for m in sel]
            mp = mask_prev0 if j == 0 else mask_prev
            o = None
            for hh in range(heads):
                sp = lax.dot_general(qb, k_prev[hh], nt, preferred_element_type=F32)
                sc = lax.dot_general(qb, k_cur[hh], nt, preferred_element_type=F32)
                sp = jnp.where(mp, sp, NEG_INF)
                sc = jnp.where(mask_cur, sc, NEG_INF)
                m = jnp.maximum(jnp.max(sp, axis=-1, keepdims=True), jnp.max(sc, axis=-1, keepdims=True))
                ep = jnp.exp(sp - m)
                ec = jnp.exp(sc - m)
                den = jnp.sum(ep, axis=-1, keepdims=True) + jnp.sum(ec, axis=-1, keepdims=True)
                pv = (jnp.dot(ep.astype(BF16), v_prev[hh], preferred_element_type=F32)
                      + jnp.dot(ec.astype(BF16), v_cur[hh], preferred_element_type=F32))
                pv = pv / den
                o = pv if o is None else o + pv
                lses[j].append(m + jnp.log(den))
            o_ref[rows, cols] = o.astype(BF16)
            k_prev, v_prev = k_cur, v_cur

    for j in range(nsub):
        m = functools.reduce(jnp.maximum, lses[j])
        tot = functools.reduce(lambda a, b2: a + b2, [jnp.exp(l - m) for l in lses[j]])
        grp = m + jnp.log(tot) - math.log(n_heads)
        lse_ref[j * w:(j + 1) * w, :] = jnp.broadcast_to(grp, (w, LANES))


def _attn_group(q, k, v, *, w, n_heads, tq=512):
    nseq, l, cg = q.shape
    tq = min(tq, l)
    ratio = tq // w
    kern = functools.partial(_attn_kernel, tq=tq, w=w, n_heads=n_heads)
    cur = pl.BlockSpec((None, tq, cg), lambda n, i: (n, i, 0))
    prev = pl.BlockSpec((None, w, cg), lambda n, i: (n, jnp.maximum(i * ratio - 1, 0), 0))
    return pl.pallas_call(
        kern,
        grid=(nseq, l // tq),
        in_specs=[cur, cur, cur, prev, prev],
        out_specs=[cur, pl.BlockSpec((None, tq, LANES), lambda n, i: (n, i, 0))],
        out_shape=[jax.ShapeDtypeStruct((nseq, l, cg), BF16),
                   jax.ShapeDtypeStruct((nseq, l, LANES), F32)],
        compiler_params=_params("arbitrary", "arbitrary"),
        name=f"attn_h{n_heads}_l{l}",
    )(q, k, v, k, v)


def _attn_out_kernel(x_ref, *refs, ts, groups):
    ng = len(groups)
    o_refs, l_refs = refs[:ng], refs[ng:2 * ng]
    wout_ref, y_ref = refs[2 * ng], refs[2 * ng + 1]
    scratch = refs[2 * ng + 2:]
    outs, lses = [], []
    si = 0
    for gi, (d, cg) in enumerate(groups):
        if d == 1:
            outs.append(o_refs[gi][0].astype(F32))
            lses.append(l_refs[gi][0])
            continue
        n = ts // d
        o_scr, l_scr = scratch[si], scratch[si + 1]
        si += 2
        n_chunks = o_scr.shape[0]
        for r in range(d):
            blk = o_refs[gi][r].astype(F32)
            for c in range(n_chunks):
                piece = blk[:, c * LANES:min((c + 1) * LANES, cg)]
                if piece.shape[1] < LANES:
                    piece = jnp.concatenate([piece, jnp.zeros((n, LANES - piece.shape[1]), F32)], axis=1)
                o_scr[c, pl.ds(r, n, stride=d), :] = piece
            l_scr[pl.ds(r, n, stride=d), :] = l_refs[gi][r]
        outs.append(jnp.concatenate([o_scr[c] for c in range(n_chunks)], axis=1)[:, :cg])
        lses.append(l_scr[...])
    m = functools.reduce(jnp.maximum, lses)
    es = [jnp.exp(l - m) for l in lses]
    inv = float(ng) / functools.reduce(lambda a, b2: a + b2, es)
    y = x_ref[...]
    row = 0
    for (d, cg), o, e in zip(groups, outs, es):
        merged = (o * (e * inv)[:, :1]).astype(BF16)
        y = y + jnp.dot(merged, wout_ref[row:row + cg, :], preferred_element_type=F32)
        row += cg
    y_ref[...] = y


def _attn_out_layer(x, outs, lses, w_out, groups, *, ts=512):
    b, s, d_model = x.shape
    kern = functools.partial(_attn_out_kernel, ts=ts, groups=groups)
    o_specs = [pl.BlockSpec((None, d, ts // d, cg), lambda bi, i: (bi, 0, i, 0)) for d, cg in groups]
    l_specs = [pl.BlockSpec((None, d, ts // d, LANES), lambda bi, i: (bi, 0, i, 0)) for d, cg in groups]
    scratch = []
    for d, cg in groups:
        if d > 1:
            scratch += [pltpu.VMEM((pl.cdiv(cg, LANES), ts, LANES), F32), pltpu.VMEM((ts, LANES), F32)]
    return pl.pallas_call(
        kern,
        grid=(b, s // ts),
        in_specs=[pl.BlockSpec((None, ts, d_model), lambda bi, i: (bi, i, 0))]
        + o_specs + l_specs + [_const_spec(w_out.shape)],
        out_specs=pl.BlockSpec((None, ts, d_model), lambda bi, i: (bi, i, 0)),
        out_shape=jax.ShapeDtypeStruct(x.shape, F32),
        scratch_shapes=scratch,
        compiler_params=_params("arbitrary", "arbitrary"),
        name="attn_merge_out",
    )(x, *outs, *lses, w_out)


def _rope_tables(s):
    half = HEAD_DIM // 2
    inv_freq = 1.0 / (ROPE_THETA ** (jnp.arange(0, HEAD_DIM, 2, dtype=F32) / HEAD_DIM))
    ang = jnp.arange(s, dtype=F32)[:, None] * inv_freq[None, :]
    cos, sin = jnp.cos(ang), jnp.sin(ang)
    reps = LANES // half
    cos_t = jnp.concatenate([cos] * reps, axis=-1)
    sin_t = jnp.concatenate([-sin, sin] * (reps // 2), axis=-1)
    return cos_t, sin_t


def kernel(x, norm_mix, norm_ffn, norm_final, pool_w_in, pool_w_group, pool_scale, pool_w_out,
           attn_w_qkv, attn_w_out, ffn_w_gate, ffn_w_up, ffn_w_down):
    b, s, d_model = x.shape
    assert norm_mix.shape[0] == 2, "two layers: pooling mixer then dilated attention"
    n_heads = attn_w_out.shape[1] // HEAD_DIM
    d_attn = n_heads * HEAD_DIM
    head_groups = _head_groups(n_heads)
    groups = tuple((dil, nh * HEAD_DIM) for (_, dil), nh in zip(ATTN_PATTERNS, head_groups))
    bf = lambda t: t.astype(BF16)

    x = _pool_layer(x, norm_mix[0], bf(pool_w_in[0]), bf(pool_w_group[0]), pool_scale[0], bf(pool_w_out[0]))
    x = _ffn_layer(x.reshape(b * s, d_model), norm_ffn[0], bf(ffn_w_gate[0]), bf(ffn_w_up[0]),
                   bf(ffn_w_down[0]), norm_final, final=False).reshape(b, s, d_model)

    w_qkv = attn_w_qkv[0]
    w_groups, col = [], 0
    for _, cg in groups:
        w_groups.append(bf(jnp.concatenate(
            [w_qkv[:, p * d_attn + col:p * d_attn + col + cg] for p in range(3)], axis=1)))
        col += cg
    cos_t, sin_t = _rope_tables(s)
    qkv = _qkv_layer(x, norm_mix[1], cos_t, sin_t, w_groups, groups)
    outs, lses = [], []
    for gi, ((window, dil), nh) in enumerate(zip(ATTN_PATTERNS, head_groups)):
        cg = nh * HEAD_DIM
        q, k, v = (t.reshape(b * dil, s // dil, cg) for t in qkv[3 * gi:3 * gi + 3])
        o, lse = _attn_group(q, k, v, w=window // dil, n_heads=nh)
        outs.append(o.reshape(b, dil, s // dil, cg))
        lses.append(lse.reshape(b, dil, s // dil, LANES))
    x = _attn_out_layer(x, outs, lses, bf(attn_w_out[0]), groups)
    x = _ffn_layer(x.reshape(b * s, d_model), norm_ffn[1], bf(ffn_w_gate[1]), bf(ffn_w_up[1]),
                   bf(ffn_w_down[1]), norm_final, final=True).reshape(b, s, d_model)
    return x
```

```python
import functools
import math

import jax
import jax.numpy as jnp
from jax import lax
from jax.experimental import pallas as pl
from jax.experimental.pallas import tpu as pltpu

EPS = 1e-6
POOL_WINDOWS = (2, 4, 8, 16)
HEAD_DIM = 64
ATTN_PATTERNS = ((128, 1), (512, 4), (2048, 16))
ROPE_THETA = 10000.0
NEG_INF = -1e30

LANES = 128
VMEM_LIMIT_BYTES = 56 * 1024 * 1024

F32 = jnp.float32
BF16 = jnp.bfloat16


def _head_groups(n_heads):
    n = len(ATTN_PATTERNS)
    return tuple(n_heads // n + (1 if g < n_heads % n else 0) for g in range(n))


def _rmsnorm(x, g):
    ms = jnp.mean(x * x, axis=-1, keepdims=True)
    return x * lax.rsqrt(ms + EPS) * g


def _params(*sem):
    return pltpu.CompilerParams(dimension_semantics=sem, vmem_limit_bytes=VMEM_LIMIT_BYTES)


def _const_spec(shape):
    zeros = (0,) * len(shape)
    return pl.BlockSpec(shape, lambda *_: zeros)


def _pool_kernel(x_ref, g_ref, win_ref, wgrp_ref, scale_ref, wout_ref, o_ref, ext_ref, *, ts, halo):
    i = pl.program_id(1)
    d_model = x_ref.shape[-1]
    gdim = d_model // len(POOL_WINDOWS)

    @pl.when(i == 0)
    def _():
        ext_ref[0:halo, :] = jnp.zeros((halo, d_model), F32)

    x = x_ref[...]
    h = _rmsnorm(x, g_ref[...]).astype(BF16)
    u = jnp.dot(h, win_ref[...], preferred_element_type=F32)
    ext_ref[halo:halo + ts, :] = u

    pos = i * ts + lax.broadcasted_iota(jnp.int32, (ts, 1), 0)
    zs = []
    for g, w in enumerate(POOL_WINDOWS):
        cols = slice(g * gdim, (g + 1) * gdim)
        s = ext_ref[:, cols]
        k = 1
        while k < w:
            s = s + pltpu.roll(s, k, axis=0)
            k *= 2
        inv_cnt = 1.0 / jnp.minimum(pos + 1, w).astype(F32)
        p = s[halo:, :] * inv_cnt - u[:, cols]
        zs.append(jnp.dot(p.astype(BF16), wgrp_ref[g], preferred_element_type=F32))
    z = jnp.concatenate(zs, axis=1) * scale_ref[...]
    y = jnp.dot(z.astype(BF16), wout_ref[...], preferred_element_type=F32)
    o_ref[...] = x + y
    ext_ref[0:halo, :] = ext_ref[ts:ts + halo, :]


def _pool_layer(x, g, w_in, w_group, scale, w_out, *, ts=512):
    b, s, d = x.shape
    halo = max(POOL_WINDOWS)
    kern = functools.partial(_pool_kernel, ts=ts, halo=halo)
    return pl.pallas_call(
        kern,
        grid=(b, s // ts),
        in_specs=[
            pl.BlockSpec((None, ts, d), lambda bi, i: (bi, i, 0)),
            _const_spec((1, d)),
            _const_spec(w_in.shape),
            _const_spec(w_group.shape),
            _const_spec((1, d)),
            _const_spec(w_out.shape),
        ],
        out_specs=pl.BlockSpec((None, ts, d), lambda bi, i: (bi, i, 0)),
        out_shape=jax.ShapeDtypeStruct(x.shape, F32),
        scratch_shapes=[pltpu.VMEM((ts + halo, d), F32)],
        compiler_params=_params("arbitrary", "arbitrary"),
        name="pool_mixer",
    )(x, g.reshape(1, d), w_in, w_group, scale.reshape(1, d), w_out)


def _ffn_kernel(x_ref, g_ref, wg_ref, wu_ref, wd_ref, gf_ref, o_ref, *, fc, final):
    x = x_ref[...]
    h = _rmsnorm(x, g_ref[...]).astype(BF16)
    acc = x
    for c in range(wg_ref.shape[1] // fc):
        sl = slice(c * fc, (c + 1) * fc)
        gate = jnp.dot(h, wg_ref[:, sl], preferred_element_type=F32)
        up = jnp.dot(h, wu_ref[:, sl], preferred_element_type=F32)
        a = (gate * jax.nn.sigmoid(gate) * up).astype(BF16)
        acc = acc + jnp.dot(a, wd_ref[sl, :], preferred_element_type=F32)
    if final:
        acc = _rmsnorm(acc, gf_ref[...])
    o_ref[...] = acc


def _ffn_layer(x2d, g, w_gate, w_up, w_down, g_final, *, final, tm=512, fc=256):
    n, d = x2d.shape
    kern = functools.partial(_ffn_kernel, fc=fc, final=final)
    return pl.pallas_call(
        kern,
        grid=(n // tm,),
        in_specs=[
            pl.BlockSpec((tm, d), lambda i: (i, 0)),
            _const_spec((1, d)),
            _const_spec(w_gate.shape),
            _const_spec(w_up.shape),
            _const_spec(w_down.shape),
            _const_spec((1, d)),
        ],
        out_specs=pl.BlockSpec((tm, d), lambda i: (i, 0)),
        out_shape=jax.ShapeDtypeStruct(x2d.shape, F32),
        compiler_params=_params("arbitrary"),
        name="ffn_final" if final else "ffn",
    )(x2d, g.reshape(1, d), w_gate, w_up, w_down, g_final.reshape(1, d))


def _deinterleave(ref, ts, d):
    n = ts // d
    return jnp.concatenate([ref[pl.ds(r, n, stride=d), :] for r in range(d)], axis=0)


def _qkv_kernel(x_ref, g_ref, cos_ref, sin_ref, *refs, ts, groups):
    ng = len(groups)
    w_refs, out_refs, xs_ref = refs[:ng], refs[ng:-1], refs[-1]
    lane = lax.broadcasted_iota(jnp.int32, (1, LANES), 1)
    first_half = (lane % HEAD_DIM) < (HEAD_DIM // 2)
    n_chunks = x_ref.shape[-1] // LANES
    for c in range(n_chunks):
        xs_ref[c] = x_ref[:, c * LANES:(c + 1) * LANES]
    for gi, (d, cg) in enumerate(groups):
        n = ts // d
        if d == 1:
            xp, cs, sn = x_ref[...], cos_ref[...], sin_ref[...]
        else:
            xp = jnp.concatenate([_deinterleave(xs_ref.at[c], ts, d) for c in range(n_chunks)], axis=1)
            cs = _deinterleave(cos_ref, ts, d)
            sn = _deinterleave(sin_ref, ts, d)
        h = _rmsnorm(xp, g_ref[...]).astype(BF16)
        qkv = jnp.dot(h, w_refs[gi][...], preferred_element_type=F32)
        chunks = []
        for c in range(2 * cg // LANES):
            t = qkv[:, c * LANES:(c + 1) * LANES]
            swapped = jnp.where(first_half, pltpu.roll(t, LANES - HEAD_DIM // 2, axis=1),
                                pltpu.roll(t, HEAD_DIM // 2, axis=1))
            q_scale = jnp.where(c * LANES + lane < cg, HEAD_DIM ** -0.5, 1.0).astype(F32)
            chunks.append((t * cs + swapped * sn) * q_scale)
        qk = jnp.concatenate(chunks, axis=1)
        parts = (qk[:, :cg], qk[:, cg:2 * cg], qkv[:, 2 * cg:])
        for part, o_ref in zip(parts, out_refs[3 * gi:3 * gi + 3]):
            pb = part.astype(BF16)
            for r in range(d):
                o_ref[r] = pb[r * n:(r + 1) * n, :]


def _qkv_layer(x, g, cos, sin, w_groups, groups, *, ts=512):
    b, s, d_model = x.shape
    kern = functools.partial(_qkv_kernel, ts=ts, groups=groups)
    out_shapes, out_specs = [], []
    for d, cg in groups:
        for _ in range(3):
            out_shapes.append(jax.ShapeDtypeStruct((b, d, s // d, cg), BF16))
            out_specs.append(pl.BlockSpec((None, d, ts // d, cg), lambda bi, i: (bi, 0, i, 0)))
    return pl.pallas_call(
        kern,
        grid=(b, s // ts),
        in_specs=[
            pl.BlockSpec((None, ts, d_model), lambda bi, i: (bi, i, 0)),
            _const_spec((1, d_model)),
            pl.BlockSpec((ts, LANES), lambda bi, i: (i, 0)),
            pl.BlockSpec((ts, LANES), lambda bi, i: (i, 0)),
        ] + [_const_spec(w.shape) for w in w_groups],
        out_specs=out_specs,
        out_shape=out_shapes,
        scratch_shapes=[pltpu.VMEM((d_model // LANES, ts, LANES), F32)],
        compiler_params=_params("arbitrary", "arbitrary"),
        name="qkv_rope",
    )(x, g.reshape(1, d_model), cos, sin, *w_groups)


def _attn_kernel(q_ref, k_ref, v_ref, kp_ref, vp_ref, o_ref, lse_ref, lse_scr, *, tq, w, n_heads):
    i = pl.program_id(1)
    cg = n_heads * HEAD_DIM
    nsub = tq // w
    qi = lax.broadcasted_iota(jnp.int32, (w, 2 * w), 0)
    kj = lax.broadcasted_iota(jnp.int32, (w, 2 * w), 1)
    band = jnp.logical_and(kj >= qi, kj <= qi + w)
    band_first = jnp.logical_and(band, jnp.logical_or(kj >= w, i > 0))
    lane = lax.broadcasted_iota(jnp.int32, (1, LANES), 1)
    nt = (((1,), (1,)), ((), ()))

    units = []
    for c0 in range(0, cg, LANES):
        width = min(LANES, cg - c0)
        for hh in range(width // HEAD_DIM):
            if width == LANES:
                sel = (lane < HEAD_DIM) if hh == 0 else (lane >= HEAD_DIM)
            else:
                sel = None
            units.append((len(units), slice(c0, c0 + width), hh, sel))

    def key_blocks(ref, prev_ref, cols, sel):
        blks = [prev_ref[:, cols]] + [ref[j * w:(j + 1) * w, cols] for j in range(nsub)]
        if sel is not None:
            blks = [jnp.where(sel, b, jnp.zeros_like(b)) for b in blks]
        return [jnp.concatenate([blks[j], blks[j + 1]], axis=0) for j in range(nsub)]

    def scores(unit):
        _, cols, _, sel = unit
        kc = key_blocks(k_ref, kp_ref, cols, sel)
        return [lax.dot_general(q_ref[j * w:(j + 1) * w, cols], kc[j], nt, preferred_element_type=F32)
                for j in range(nsub)]

    def finish(unit, sc):
        h, cols, hh, sel = unit
        vc = key_blocks(v_ref, vp_ref, cols, sel)
        lo = hh * HEAD_DIM
        for j in range(nsub):
            rows = slice(j * w, (j + 1) * w)
            s = jnp.where(band_first if j == 0 else band, sc[j], NEG_INF)
            m = jnp.max(s, axis=-1, keepdims=True)
            e = jnp.exp(s - m)
            den = jnp.sum(e, axis=-1, keepdims=True)
            pv = jnp.dot(e.astype(BF16), vc[j], preferred_element_type=F32) / den
            o_ref[rows, cols.start + lo:cols.start + lo + HEAD_DIM] = pv[:, lo:lo + HEAD_DIM].astype(BF16)
            lse_scr[rows, :] = jnp.where(lane == h, m + jnp.log(den), lse_scr[rows, :])

    lse_scr[...] = jnp.full(lse_scr.shape, NEG_INF, F32)
    pending = scores(units[0])
    for u, unit in enumerate(units):
        ahead = scores(units[u + 1]) if u + 1 < len(units) else None
        finish(unit, pending)
        pending = ahead

    lse = lse_scr[...]
    m = jnp.max(lse, axis=-1, keepdims=True)
    tot = jnp.sum(jnp.exp(lse - m), axis=-1, keepdims=True)
    lse_ref[...] = jnp.broadcast_to(m + jnp.log(tot) - math.log(n_heads), lse_ref.shape)


def _attn_group(q, k, v, *, w, n_heads, tq=512):
    nseq, l, cg = q.shape
    tq = min(tq, l)
    ratio = tq // w
    kern = functools.partial(_attn_kernel, tq=tq, w=w, n_heads=n_heads)
    cur = pl.BlockSpec((None, tq, cg), lambda n, i: (n, i, 0))
    prev = pl.BlockSpec((None, w, cg), lambda n, i: (n, jnp.maximum(i * ratio - 1, 0), 0))
    return pl.pallas_call(
        kern,
        grid=(nseq, l // tq),
        in_specs=[cur, cur, cur, prev, prev],
        out_specs=[cur, pl.BlockSpec((None, tq, LANES), lambda n, i: (n, i, 0))],
        out_shape=[jax.ShapeDtypeStruct((nseq, l, cg), BF16),
                   jax.ShapeDtypeStruct((nseq, l, LANES), F32)],
        scratch_shapes=[pltpu.VMEM((tq, LANES), F32)],
        compiler_params=_params("arbitrary", "arbitrary"),
        name=f"attn_h{n_heads}_l{l}",
    )(q, k, v, k, v)


def _attn_out_kernel(x_ref, *refs, ts, groups):
    ng = len(groups)
    o_refs, l_refs = refs[:ng], refs[ng:2 * ng]
    wout_ref, y_ref = refs[2 * ng], refs[2 * ng + 1]
    scratch = refs[2 * ng + 2:]
    outs, lses = [], []
    si = 0
    for gi, (d, cg) in enumerate(groups):
        if d == 1:
            outs.append(o_refs[gi][0].astype(F32))
            lses.append(l_refs[gi][0])
            continue
        n = ts // d
        o_scr, l_scr = scratch[si], scratch[si + 1]
        si += 2
        n_chunks = o_scr.shape[0]
        for r in range(d):
            blk = o_refs[gi][r].astype(F32)
            for c in range(n_chunks):
                piece = blk[:, c * LANES:min((c + 1) * LANES, cg)]
                if piece.shape[1] < LANES:
                    piece = jnp.concatenate([piece, jnp.zeros((n, LANES - piece.shape[1]), F32)], axis=1)
                o_scr[c, pl.ds(r, n, stride=d), :] = piece
            l_scr[pl.ds(r, n, stride=d), :] = l_refs[gi][r]
        outs.append(jnp.concatenate([o_scr[c] for c in range(n_chunks)], axis=1)[:, :cg])
        lses.append(l_scr[...])
    m = functools.reduce(jnp.maximum, lses)
    es = [jnp.exp(l - m) for l in lses]
    inv = float(ng) / functools.reduce(lambda a, b2: a + b2, es)
    y = x_ref[...]
    row = 0
    for (d, cg), o, e in zip(groups, outs, es):
        merged = (o * (e * inv)[:, :1]).astype(BF16)
        y = y + jnp.dot(merged, wout_ref[row:row + cg, :], preferred_element_type=F32)
        row += cg
    y_ref[...] = y


def _attn_out_layer(x, outs, lses, w_out, groups, *, ts=512):
    b, s, d_model = x.shape
    kern = functools.partial(_attn_out_kernel, ts=ts, groups=groups)
    o_specs = [pl.BlockSpec((None, d, ts // d, cg), lambda bi, i: (bi, 0, i, 0)) for d, cg in groups]
    l_specs = [pl.BlockSpec((None, d, ts // d, LANES), lambda bi, i: (bi, 0, i, 0)) for d, cg in groups]
    scratch = []
    for d, cg in groups:
        if d > 1:
            scratch += [pltpu.VMEM((pl.cdiv(cg, LANES), ts, LANES), F32), pltpu.VMEM((ts, LANES), F32)]
    return pl.pallas_call(
        kern,
        grid=(b, s // ts),
        in_specs=[pl.BlockSpec((None, ts, d_model), lambda bi, i: (bi, i, 0))]
        + o_specs + l_specs + [_const_spec(w_out.shape)],
        out_specs=pl.BlockSpec((None, ts, d_model), lambda bi, i: (bi, i, 0)),
        out_shape=jax.ShapeDtypeStruct(x.shape, F32),
        scratch_shapes=scratch,
        compiler_params=_params("arbitrary", "arbitrary"),
        name="attn_merge_out",
    )(x, *outs, *lses, w_out)


def _rope_tables(s):
    half = HEAD_DIM // 2
    inv_freq = 1.0 / (ROPE_THETA ** (jnp.arange(0, HEAD_DIM, 2, dtype=F32) / HEAD_DIM))
    ang = jnp.arange(s, dtype=F32)[:, None] * inv_freq[None, :]
    cos, sin = jnp.cos(ang), jnp.sin(ang)
    reps = LANES // half
    cos_t = jnp.concatenate([cos] * reps, axis=-1)
    sin_t = jnp.concatenate([-sin, sin] * (reps // 2), axis=-1)
    return cos_t, sin_t


def kernel(x, norm_mix, norm_ffn, norm_final, pool_w_in, pool_w_group, pool_scale, pool_w_out,
           attn_w_qkv, attn_w_out, ffn_w_gate, ffn_w_up, ffn_w_down):
    b, s, d_model = x.shape
    assert norm_mix.shape[0] == 2, "two layers: pooling mixer then dilated attention"
    n_heads = attn_w_out.shape[1] // HEAD_DIM
    d_attn = n_heads * HEAD_DIM
    head_groups = _head_groups(n_heads)
    groups = tuple((dil, nh * HEAD_DIM) for (_, dil), nh in zip(ATTN_PATTERNS, head_groups))
    bf = lambda t: t.astype(BF16)

    x = _pool_layer(x, norm_mix[0], bf(pool_w_in[0]), bf(pool_w_group[0]), pool_scale[0], bf(pool_w_out[0]))
    x = _ffn_layer(x.reshape(b * s, d_model), norm_ffn[0], bf(ffn_w_gate[0]), bf(ffn_w_up[0]),
                   bf(ffn_w_down[0]), norm_final, final=False).reshape(b, s, d_model)

    w_qkv = attn_w_qkv[0]
    w_groups, col = [], 0
    for _, cg in groups:
        w_groups.append(bf(jnp.concatenate(
            [w_qkv[:, p * d_attn + col:p * d_attn + col + cg] for p in range(3)], axis=1)))
        col += cg
    cos_t, sin_t = _rope_tables(s)
    qkv = _qkv_layer(x, norm_mix[1], cos_t, sin_t, w_groups, groups)
    outs, lses = [], []
    for gi, ((window, dil), nh) in enumerate(zip(ATTN_PATTERNS, head_groups)):
        cg = nh * HEAD_DIM
        q, k, v = (t.reshape(b * dil, s // dil, cg) for t in qkv[3 * gi:3 * gi + 3])
        o, lse = _attn_group(q, k, v, w=window // dil, n_heads=nh)
        outs.append(o.reshape(b, dil, s // dil, cg))
        lses.append(lse.reshape(b, dil, s // dil, LANES))
    x = _attn_out_layer(x, outs, lses, bf(attn_w_out[0]), groups)
    x = _ffn_layer(x.reshape(b * s, d_model), norm_ffn[1], bf(ffn_w_gate[1]), bf(ffn_w_up[1]),
                   bf(ffn_w_down[1]), norm_final, final=True).reshape(b, s, d_model)
    return x
```

```python
import functools
import math

import jax
import jax.numpy as jnp
import numpy as np
from jax import lax
from jax.experimental import pallas as pl
from jax.experimental.pallas import tpu as pltpu

EPS = 1e-6
POOL_WINDOWS = (2, 4, 8, 16)
HEAD_DIM = 64
ATTN_PATTERNS = ((128, 1), (512, 4), (2048, 16))
ROPE_THETA = 10000.0
NEG_INF = -1e30

LANES = 128
VMEM_LIMIT_BYTES = 56 * 1024 * 1024
TOKEN_TILE = 512
ATTN_Q_TILE = 512
FFN_CHUNK = 256
DEINT_STRIDE = 4

F32 = jnp.float32
BF16 = jnp.bfloat16


def _head_groups(n_heads):
    n = len(ATTN_PATTERNS)
    return tuple(n_heads // n + (1 if g < n_heads % n else 0) for g in range(n))


def _rmsnorm(x, g):
    ms = jnp.mean(x * x, axis=-1, keepdims=True)
    return x * lax.rsqrt(ms + EPS) * g


def _params(*sem):
    return pltpu.CompilerParams(dimension_semantics=sem, vmem_limit_bytes=VMEM_LIMIT_BYTES)


def _const_spec(shape):
    zeros = (0,) * len(shape)
    return pl.BlockSpec(shape, lambda *_: zeros, pipeline_mode=pl.Buffered(1))


def _layer_spec(stacked, layer):
    idx = (layer,) + (0,) * (stacked.ndim - 1)
    return pl.BlockSpec((None,) + stacked.shape[1:], lambda *_: idx, pipeline_mode=pl.Buffered(1))


def _pool_kernel(x_ref, g_ref, win_ref, wgrp_ref, scale_ref, wout_ref, o_ref, ext_ref, *, ts, halo):
    i = pl.program_id(1)
    d_model = x_ref.shape[-1]
    gdim = d_model // len(POOL_WINDOWS)

    @pl.when(i == 0)
    def _():
        ext_ref[0:halo, :] = jnp.zeros((halo, d_model), F32)

    x = x_ref[...]
    h = _rmsnorm(x, g_ref[...]).astype(BF16)
    u = jnp.dot(h, win_ref[...].astype(BF16), preferred_element_type=F32)
    ext_ref[halo:halo + ts, :] = u

    pos = i * ts + lax.broadcasted_iota(jnp.int32, (ts, 1), 0)
    zs = []
    for g, w in enumerate(POOL_WINDOWS):
        cols = slice(g * gdim, (g + 1) * gdim)
        s = ext_ref[:, cols]
        k = 1
        while k < w:
            s = s + pltpu.roll(s, k, axis=0)
            k *= 2
        inv_cnt = 1.0 / jnp.minimum(pos + 1, w).astype(F32)
        p = s[halo:, :] * inv_cnt - u[:, cols]
        zs.append(jnp.dot(p.astype(BF16), wgrp_ref[g].astype(BF16), preferred_element_type=F32))
    z = jnp.concatenate(zs, axis=1) * scale_ref[...]
    y = jnp.dot(z.astype(BF16), wout_ref[...].astype(BF16), preferred_element_type=F32)
    o_ref[...] = x + y
    ext_ref[0:halo, :] = ext_ref[ts:ts + halo, :]


def _pool_layer(x, g, w_in, w_group, scale, w_out, *, layer, ts=TOKEN_TILE):
    b, s, d = x.shape
    halo = max(POOL_WINDOWS)
    kern = functools.partial(_pool_kernel, ts=ts, halo=halo)
    return pl.pallas_call(
        kern,
        grid=(b, s // ts),
        in_specs=[
            pl.BlockSpec((None, ts, d), lambda bi, i: (bi, i, 0)),
            _const_spec((1, d)),
            _layer_spec(w_in, layer),
            _layer_spec(w_group, layer),
            _const_spec((1, d)),
            _layer_spec(w_out, layer),
        ],
        out_specs=pl.BlockSpec((None, ts, d), lambda bi, i: (bi, i, 0)),
        out_shape=jax.ShapeDtypeStruct(x.shape, F32),
        scratch_shapes=[pltpu.VMEM((ts + halo, d), F32)],
        compiler_params=_params("arbitrary", "arbitrary"),
        name="pool_mixer",
    )(x, g.reshape(1, d), w_in, w_group, scale.reshape(1, d), w_out)


def _ffn_kernel(x_ref, g_ref, wg_ref, wu_ref, wd_ref, gf_ref, o_ref, *, fc, final):
    x = x_ref[...]
    h = _rmsnorm(x, g_ref[...]).astype(BF16)
    acc = x
    for c in range(wg_ref.shape[-1] // fc):
        sl = slice(c * fc, (c + 1) * fc)
        gate = jnp.dot(h, wg_ref[:, sl].astype(BF16), preferred_element_type=F32)
        up = jnp.dot(h, wu_ref[:, sl].astype(BF16), preferred_element_type=F32)
        a = (gate * jax.nn.sigmoid(gate) * up).astype(BF16)
        acc = acc + jnp.dot(a, wd_ref[sl, :].astype(BF16), preferred_element_type=F32)
    if final:
        acc = _rmsnorm(acc, gf_ref[...])
    o_ref[...] = acc


def _ffn_layer(x2d, g, w_gate, w_up, w_down, g_final, *, layer, final, tm=TOKEN_TILE, fc=FFN_CHUNK):
    n, d = x2d.shape
    kern = functools.partial(_ffn_kernel, fc=fc, final=final)
    return pl.pallas_call(
        kern,
        grid=(n // tm,),
        in_specs=[
            pl.BlockSpec((tm, d), lambda i: (i, 0)),
            _const_spec((1, d)),
            _layer_spec(w_gate, layer),
            _layer_spec(w_up, layer),
            _layer_spec(w_down, layer),
            _const_spec((1, d)),
        ],
        out_specs=pl.BlockSpec((tm, d), lambda i: (i, 0)),
        out_shape=jax.ShapeDtypeStruct(x2d.shape, F32),
        compiler_params=_params("arbitrary"),
        name="ffn_final" if final else "ffn",
    )(x2d, g.reshape(1, d), w_gate, w_up, w_down, g_final.reshape(1, d))


def _class_order(d):
    order, cur = (0,), 1
    while cur < d:
        order = tuple(c + cur * r for c in order for r in range(DEINT_STRIDE))
        cur *= DEINT_STRIDE
    assert cur == d, "dilations must be powers of DEINT_STRIDE"
    return order


def _qkv_kernel(x_ref, g_ref, *refs, ts, groups):
    ng = len(groups)
    tab_refs, w_refs = refs[:2 * ng], refs[2 * ng:3 * ng]
    out_refs, hs_refs = refs[3 * ng:-2], refs[-2:]
    lane = lax.broadcasted_iota(jnp.int32, (1, LANES), 1)
    first_half = (lane % HEAD_DIM) < (HEAD_DIM // 2)
    n_chunks = x_ref.shape[-1] // LANES
    h_nat = _rmsnorm(x_ref[...], g_ref[...])
    cur_d = 1
    h = h_nat
    for gi, (d, cg) in enumerate(groups):
        while cur_d < d:
            src = hs_refs[0] if cur_d == 1 else hs_refs[1]
            for c in range(n_chunks):
                src[c] = h[:, c * LANES:(c + 1) * LANES]
            blk = ts // cur_d
            n = blk // DEINT_STRIDE
            h = jnp.concatenate(
                [jnp.concatenate([src[c, pl.ds(b0 * blk + r, n, stride=DEINT_STRIDE), :]
                                  for b0 in range(cur_d) for r in range(DEINT_STRIDE)], axis=0)
                 for c in range(n_chunks)], axis=1)
            cur_d *= DEINT_STRIDE
        assert cur_d == d
        cs, sn = tab_refs[2 * gi][...], tab_refs[2 * gi + 1][...]
        qkv = jnp.dot(h.astype(BF16), w_refs[gi][...], preferred_element_type=F32)
        chunks = []
        for c in range(2 * cg // LANES):
            t = qkv[:, c * LANES:(c + 1) * LANES]
            swapped = jnp.where(first_half, pltpu.roll(t, LANES - HEAD_DIM // 2, axis=1),
                                pltpu.roll(t, HEAD_DIM // 2, axis=1))
            chunks.append(t * cs + swapped * sn)
        qk = jnp.concatenate(chunks, axis=1)
        parts = (qk[:, :cg], qk[:, cg:2 * cg], qkv[:, 2 * cg:])
        n = ts // d
        for part, o_ref in zip(parts, out_refs[3 * gi:3 * gi + 3]):
            pb = part.astype(BF16)
            for slot, r in enumerate(_class_order(d)):
                o_ref[r] = pb[slot * n:(slot + 1) * n, :]


def _qkv_layer(x, g, tables, w_groups, groups, *, ts):
    b, s, d_model = x.shape
    kern = functools.partial(_qkv_kernel, ts=ts, groups=groups)
    out_shapes, out_specs = [], []
    for d, cg in groups:
        for _ in range(3):
            out_shapes.append(jax.ShapeDtypeStruct((b, d, s // d, cg), BF16))
            out_specs.append(pl.BlockSpec((None, d, ts // d, cg), lambda bi, i: (bi, 0, i, 0)))
    lane_chunked = pltpu.VMEM((d_model // LANES, ts, LANES), F32)
    return pl.pallas_call(
        kern,
        grid=(b, s // ts),
        in_specs=[pl.BlockSpec((None, ts, d_model), lambda bi, i: (bi, i, 0)), _const_spec((1, d_model))]
        + [pl.BlockSpec((ts, LANES), lambda bi, i: (i, 0)) for _ in tables]
        + [_const_spec(w.shape) for w in w_groups],
        out_specs=out_specs,
        out_shape=out_shapes,
        scratch_shapes=[lane_chunked, lane_chunked],
        compiler_params=_params("arbitrary", "arbitrary"),
        name="qkv_rope",
    )(x, g.reshape(1, d_model), *tables, *w_groups)


def _attn_kernel(q_ref, k_ref, v_ref, kp_ref, vp_ref, o_ref, lse_ref, lse_scr, *, tq, w, n_heads):
    i = pl.program_id(1)
    cg = n_heads * HEAD_DIM
    nsub = tq // w
    qi = lax.broadcasted_iota(jnp.int32, (w, 2 * w), 0)
    kj = lax.broadcasted_iota(jnp.int32, (w, 2 * w), 1)
    band = jnp.logical_and(kj >= qi, kj <= qi + w)
    band_first = jnp.logical_and(band, jnp.logical_or(kj >= w, i > 0))
    lane = lax.broadcasted_iota(jnp.int32, (1, LANES), 1)
    nt = (((1,), (1,)), ((), ()))

    units = []
    for c0 in range(0, cg, LANES):
        width = min(LANES, cg - c0)
        for hh in range(width // HEAD_DIM):
            if width == LANES:
                sel = (lane < HEAD_DIM) if hh == 0 else (lane >= HEAD_DIM)
            else:
                sel = None
            units.append((len(units), slice(c0, c0 + width), hh, sel))

    def key_rows(ref, prev_ref, cols, j):
        if j == 0:
            return jnp.concatenate([prev_ref[:, cols], ref[0:w, cols]], axis=0)
        return ref[(j - 1) * w:(j + 1) * w, cols]

    def scores(unit):
        _, cols, _, sel = unit
        out = []
        for j in range(nsub):
            qb = q_ref[j * w:(j + 1) * w, cols]
            if sel is not None:
                qb = jnp.where(sel, qb, jnp.zeros_like(qb))
            out.append(lax.dot_general(qb, key_rows(k_ref, kp_ref, cols, j), nt, preferred_element_type=F32))
        return out

    def finish(unit, sc):
        h, cols, hh, _ = unit
        lo = hh * HEAD_DIM
        for j in range(nsub):
            rows = slice(j * w, (j + 1) * w)
            s = jnp.where(band_first if j == 0 else band, sc[j], NEG_INF)
            m = jnp.max(s, axis=-1, keepdims=True)
            e = jnp.exp(s - m)
            den = jnp.sum(e, axis=-1, keepdims=True)
            pv = jnp.dot(e.astype(BF16), key_rows(v_ref, vp_ref, cols, j), preferred_element_type=F32) / den
            o_ref[rows, cols.start + lo:cols.start + lo + HEAD_DIM] = pv[:, lo:lo + HEAD_DIM].astype(BF16)
            lse_scr[rows, :] = jnp.where(lane == h, m + jnp.log(den), lse_scr[rows, :])

    lse_scr[...] = jnp.full(lse_scr.shape, NEG_INF, F32)
    pending = scores(units[0])
    for u, unit in enumerate(units):
        ahead = scores(units[u + 1]) if u + 1 < len(units) else None
        finish(unit, pending)
        pending = ahead

    lse = lse_scr[...]
    m = jnp.max(lse, axis=-1, keepdims=True)
    tot = jnp.sum(jnp.exp(lse - m), axis=-1, keepdims=True)
    lse_ref[...] = jnp.broadcast_to(m + jnp.log(tot) - math.log(n_heads), lse_ref.shape)


def _attn_group(q, k, v, *, w, n_heads, tq=ATTN_Q_TILE):
    nseq, l, cg = q.shape
    tq = min(tq, l)
    ratio = tq // w
    kern = functools.partial(_attn_kernel, tq=tq, w=w, n_heads=n_heads)
    cur = pl.BlockSpec((None, tq, cg), lambda n, i: (n, i, 0))
    prev = pl.BlockSpec((None, w, cg), lambda n, i: (n, jnp.maximum(i * ratio - 1, 0), 0))
    return pl.pallas_call(
        kern,
        grid=(nseq, l // tq),
        in_specs=[cur, cur, cur, prev, prev],
        out_specs=[cur, pl.BlockSpec((None, tq, LANES), lambda n, i: (n, i, 0))],
        out_shape=[jax.ShapeDtypeStruct((nseq, l, cg), BF16),
                   jax.ShapeDtypeStruct((nseq, l, LANES), F32)],
        scratch_shapes=[pltpu.VMEM((tq, LANES), F32)],
        compiler_params=_params("arbitrary", "arbitrary"),
        name=f"attn_h{n_heads}_l{l}",
    )(q, k, v, k, v)


def _attn_out_kernel(x_ref, *refs, ts, groups):
    ng = len(groups)
    o_refs, l_refs = refs[:ng], refs[ng:2 * ng]
    wout_ref, y_ref = refs[2 * ng], refs[2 * ng + 1]
    scratch = refs[2 * ng + 2:]
    outs, lses = [], []
    si = 0
    for gi, (d, cg) in enumerate(groups):
        if d == 1:
            outs.append(o_refs[gi][0].astype(F32))
            lses.append(l_refs[gi][0])
            continue
        n = ts // d
        o_scr, l_scr = scratch[si], scratch[si + 1]
        si += 2
        n_chunks = o_scr.shape[0]
        for r in range(d):
            blk = o_refs[gi][r].astype(F32)
            for c in range(n_chunks):
                piece = blk[:, c * LANES:min((c + 1) * LANES, cg)]
                if piece.shape[1] < LANES:
                    piece = jnp.concatenate([piece, jnp.zeros((n, LANES - piece.shape[1]), F32)], axis=1)
                o_scr[c, pl.ds(r, n, stride=d), :] = piece
            l_scr[pl.ds(r, n, stride=d), :] = l_refs[gi][r]
        outs.append(jnp.concatenate([o_scr[c] for c in range(n_chunks)], axis=1)[:, :cg])
        lses.append(l_scr[...])
    m = functools.reduce(jnp.maximum, lses)
    es = [jnp.exp(l - m) for l in lses]
    inv = float(ng) / functools.reduce(lambda a, b2: a + b2, es)
    y = x_ref[...]
    row = 0
    for (d, cg), o, e in zip(groups, outs, es):
        merged = (o * (e * inv)[:, :1]).astype(BF16)
        y = y + jnp.dot(merged, wout_ref[row:row + cg, :].astype(BF16), preferred_element_type=F32)
        row += cg
    y_ref[...] = y


def _attn_out_layer(x, outs, lses, w_out, groups, *, layer, ts=TOKEN_TILE):
    b, s, d_model = x.shape
    kern = functools.partial(_attn_out_kernel, ts=ts, groups=groups)
    o_specs = [pl.BlockSpec((None, d, ts // d, cg), lambda bi, i: (bi, 0, i, 0)) for d, cg in groups]
    l_specs = [pl.BlockSpec((None, d, ts // d, LANES), lambda bi, i: (bi, 0, i, 0)) for d, cg in groups]
    scratch = []
    for d, cg in groups:
        if d > 1:
            scratch += [pltpu.VMEM((pl.cdiv(cg, LANES), ts, LANES), F32), pltpu.VMEM((ts, LANES), F32)]
    return pl.pallas_call(
        kern,
        grid=(b, s // ts),
        in_specs=[pl.BlockSpec((None, ts, d_model), lambda bi, i: (bi, i, 0))]
        + o_specs + l_specs + [_layer_spec(w_out, layer)],
        out_specs=pl.BlockSpec((None, ts, d_model), lambda bi, i: (bi, i, 0)),
        out_shape=jax.ShapeDtypeStruct(x.shape, F32),
        scratch_shapes=scratch,
        compiler_params=_params("arbitrary", "arbitrary"),
        name="attn_merge_out",
    )(x, *outs, *lses, w_out)


def _rope_tables(s):
    half = HEAD_DIM // 2
    inv_freq = 1.0 / (ROPE_THETA ** (jnp.arange(0, HEAD_DIM, 2, dtype=F32) / HEAD_DIM))
    ang = jnp.arange(s, dtype=F32)[:, None] * inv_freq[None, :]
    cos, sin = jnp.cos(ang), jnp.sin(ang)
    reps = LANES // half
    cos_t = jnp.concatenate([cos] * reps, axis=-1)
    sin_t = jnp.concatenate([-sin, sin] * (reps // 2), axis=-1)
    return cos_t, sin_t


def _tile_class_major(t, ts, d):
    if d == 1:
        return t
    s, c = t.shape
    t = t.reshape(s // ts, ts // d, d, c)[:, :, np.array(_class_order(d)), :]
    return t.transpose(0, 2, 1, 3).reshape(s, c)


def kernel(x, norm_mix, norm_ffn, norm_final, pool_w_in, pool_w_group, pool_scale, pool_w_out,
           attn_w_qkv, attn_w_out, ffn_w_gate, ffn_w_up, ffn_w_down):
    b, s, d_model = x.shape
    assert norm_mix.shape[0] == 2, "two layers: pooling mixer then dilated attention"
    n_heads = attn_w_out.shape[1] // HEAD_DIM
    d_attn = n_heads * HEAD_DIM
    head_groups = _head_groups(n_heads)
    groups = tuple((dil, nh * HEAD_DIM) for (_, dil), nh in zip(ATTN_PATTERNS, head_groups))
    bf = lambda t: t.astype(BF16)

    x = _pool_layer(x, norm_mix[0], pool_w_in, pool_w_group, pool_scale[0], pool_w_out, layer=0)
    x = _ffn_layer(x.reshape(b * s, d_model), norm_ffn[0], ffn_w_gate, ffn_w_up, ffn_w_down, norm_final,
                   layer=0, final=False).reshape(b, s, d_model)

    w_qkv = attn_w_qkv[0]
    part_scale = (HEAD_DIM ** -0.5, 1.0, 1.0)
    w_groups, col = [], 0
    for _, cg in groups:
        w_groups.append(bf(jnp.concatenate(
            [w_qkv[:, p * d_attn + col:p * d_attn + col + cg] * part_scale[p] for p in range(3)], axis=1)))
        col += cg
    cos_t, sin_t = _rope_tables(s)
    tables = [_tile_class_major(t, TOKEN_TILE, dil) for dil, _ in groups for t in (cos_t, sin_t)]
    qkv = _qkv_layer(x, norm_mix[1], tables, w_groups, groups, ts=TOKEN_TILE)
    outs, lses = [], []
    for gi, ((window, dil), nh) in enumerate(zip(ATTN_PATTERNS, head_groups)):
        cg = nh * HEAD_DIM
        q, k, v = (t.reshape(b * dil, s // dil, cg) for t in qkv[3 * gi:3 * gi + 3])
        o, lse = _attn_group(q, k, v, w=window // dil, n_heads=nh)
        outs.append(o.reshape(b, dil, s // dil, cg))
        lses.append(lse.reshape(b, dil, s // dil, LANES))
    x = _attn_out_layer(x, outs, lses, attn_w_out, groups, layer=0)
    x = _ffn_layer(x.reshape(b * s, d_model), norm_ffn[1], ffn_w_gate, ffn_w_up, ffn_w_down, norm_final,
                   layer=1, final=True).reshape(b, s, d_model)
    return x
```

```python
import functools
import math

import jax
import jax.numpy as jnp
import numpy as np
from jax import lax
from jax.experimental import pallas as pl
from jax.experimental.pallas import tpu as pltpu

EPS = 1e-6
POOL_WINDOWS = (2, 4, 8, 16)
HEAD_DIM = 64
ATTN_PATTERNS = ((128, 1), (512, 4), (2048, 16))
ROPE_THETA = 10000.0
NEG_INF = -1e30

LANES = 128
VMEM_LIMIT_BYTES = 56 * 1024 * 1024
TOKEN_TILE = 512
ATTN_Q_TILE = 512
FFN_CHUNK = 256
DEINT_STRIDE = 4

F32 = jnp.float32
BF16 = jnp.bfloat16


def _head_groups(n_heads):
    n = len(ATTN_PATTERNS)
    return tuple(n_heads // n + (1 if g < n_heads % n else 0) for g in range(n))


def _rmsnorm(x, g):
    ms = jnp.mean(x * x, axis=-1, keepdims=True)
    return x * lax.rsqrt(ms + EPS) * g


def _params(*sem):
    return pltpu.CompilerParams(dimension_semantics=sem, vmem_limit_bytes=VMEM_LIMIT_BYTES)


def _const_spec(shape):
    zeros = (0,) * len(shape)
    return pl.BlockSpec(shape, lambda *_: zeros, pipeline_mode=pl.Buffered(1))


def _layer_spec(stacked, layer):
    idx = (layer,) + (0,) * (stacked.ndim - 1)
    return pl.BlockSpec((None,) + stacked.shape[1:], lambda *_: idx, pipeline_mode=pl.Buffered(1))


def _pool_kernel(x_ref, g_ref, win_ref, wgrp_ref, scale_ref, wout_ref, o_ref, ext_ref, *, ts, halo):
    i = pl.program_id(1)
    d_model = x_ref.shape[-1]
    gdim = d_model // len(POOL_WINDOWS)

    @pl.when(i == 0)
    def _():
        ext_ref[0:halo, :] = jnp.zeros((halo, d_model), F32)

    x = x_ref[...]
    h = _rmsnorm(x, g_ref[...]).astype(BF16)
    u = jnp.dot(h, win_ref[...].astype(BF16), preferred_element_type=F32)
    ext_ref[halo:halo + ts, :] = u

    pos = i * ts + lax.broadcasted_iota(jnp.int32, (ts, 1), 0)
    zs = []
    for g, w in enumerate(POOL_WINDOWS):
        cols = slice(g * gdim, (g + 1) * gdim)
        s = ext_ref[:, cols]
        k = 1
        while k < w:
            s = s + pltpu.roll(s, k, axis=0)
            k *= 2
        inv_cnt = 1.0 / jnp.minimum(pos + 1, w).astype(F32)
        p = s[halo:, :] * inv_cnt - u[:, cols]
        zs.append(jnp.dot(p.astype(BF16), wgrp_ref[g].astype(BF16), preferred_element_type=F32))
    z = jnp.concatenate(zs, axis=1) * scale_ref[...]
    y = jnp.dot(z.astype(BF16), wout_ref[...].astype(BF16), preferred_element_type=F32)
    o_ref[...] = x + y
    ext_ref[0:halo, :] = ext_ref[ts:ts + halo, :]


def _pool_layer(x, g, w_in, w_group, scale, w_out, *, layer, ts=TOKEN_TILE):
    b, s, d = x.shape
    halo = max(POOL_WINDOWS)
    kern = functools.partial(_pool_kernel, ts=ts, halo=halo)
    return pl.pallas_call(
        kern,
        grid=(b, s // ts),
        in_specs=[
            pl.BlockSpec((None, ts, d), lambda bi, i: (bi, i, 0)),
            _const_spec((1, d)),
            _layer_spec(w_in, layer),
            _layer_spec(w_group, layer),
            _const_spec((1, d)),
            _layer_spec(w_out, layer),
        ],
        out_specs=pl.BlockSpec((None, ts, d), lambda bi, i: (bi, i, 0)),
        out_shape=jax.ShapeDtypeStruct(x.shape, F32),
        scratch_shapes=[pltpu.VMEM((ts + halo, d), F32)],
        compiler_params=_params("arbitrary", "arbitrary"),
        name="pool_mixer",
    )(x, g.reshape(1, d), w_in, w_group, scale.reshape(1, d), w_out)


def _ffn_kernel(x_ref, g_ref, wg_ref, wu_ref, wd_ref, gf_ref, o_ref, *, fc, final):
    x = x_ref[...]
    h = _rmsnorm(x, g_ref[...]).astype(BF16)
    acc = x
    for c in range(wg_ref.shape[-1] // fc):
        sl = slice(c * fc, (c + 1) * fc)
        gate = jnp.dot(h, wg_ref[:, sl].astype(BF16), preferred_element_type=F32)
        up = jnp.dot(h, wu_ref[:, sl].astype(BF16), preferred_element_type=F32)
        a = (gate * jax.nn.sigmoid(gate) * up).astype(BF16)
        acc = acc + jnp.dot(a, wd_ref[sl, :].astype(BF16), preferred_element_type=F32)
    if final:
        acc = _rmsnorm(acc, gf_ref[...])
    o_ref[...] = acc


def _ffn_layer(x2d, g, w_gate, w_up, w_down, g_final, *, layer, final, tm=TOKEN_TILE, fc=FFN_CHUNK):
    n, d = x2d.shape
    kern = functools.partial(_ffn_kernel, fc=fc, final=final)
    return pl.pallas_call(
        kern,
        grid=(n // tm,),
        in_specs=[
            pl.BlockSpec((tm, d), lambda i: (i, 0)),
            _const_spec((1, d)),
            _layer_spec(w_gate, layer),
            _layer_spec(w_up, layer),
            _layer_spec(w_down, layer),
            _const_spec((1, d)),
        ],
        out_specs=pl.BlockSpec((tm, d), lambda i: (i, 0)),
        out_shape=jax.ShapeDtypeStruct(x2d.shape, F32),
        compiler_params=_params("arbitrary"),
        name="ffn_final" if final else "ffn",
    )(x2d, g.reshape(1, d), w_gate, w_up, w_down, g_final.reshape(1, d))


def _class_order(d):
    order, cur = (0,), 1
    while cur < d:
        order = tuple(c + cur * r for c in order for r in range(DEINT_STRIDE))
        cur *= DEINT_STRIDE
    assert cur == d, "dilations must be powers of DEINT_STRIDE"
    return order


def _split_rows(src, n_blocks, blk):
    n = blk // DEINT_STRIDE
    return jnp.concatenate([src[pl.ds(b0 * blk + r, n, stride=DEINT_STRIDE), :]
                            for b0 in range(n_blocks) for r in range(DEINT_STRIDE)], axis=0)


def _qkv_kernel(x_ref, g_ref, cos_ref, sin_ref, *refs, ts, groups):
    ng = len(groups)
    w_refs, out_refs = refs[:ng], refs[ng:-2]
    hs_ref, tab_ref = refs[-2:]
    lane = lax.broadcasted_iota(jnp.int32, (1, LANES), 1)
    first_half = (lane % HEAD_DIM) < (HEAD_DIM // 2)
    n_chunks = x_ref.shape[-1] // LANES
    h = _rmsnorm(x_ref[...], g_ref[...])
    cs, sn = cos_ref[...], sin_ref[...]
    cur_d = 1
    for gi, (d, cg) in enumerate(groups):
        while cur_d < d:
            for c in range(n_chunks):
                hs_ref[c] = h[:, c * LANES:(c + 1) * LANES]
            tab_ref[0], tab_ref[1] = cs, sn
            blk = ts // cur_d
            h = jnp.concatenate([_split_rows(hs_ref.at[c], cur_d, blk) for c in range(n_chunks)], axis=1)
            cs, sn = _split_rows(tab_ref.at[0], cur_d, blk), _split_rows(tab_ref.at[1], cur_d, blk)
            cur_d *= DEINT_STRIDE
        assert cur_d == d, "head groups must come in increasing powers of DEINT_STRIDE"
        qkv = jnp.dot(h.astype(BF16), w_refs[gi][...], preferred_element_type=F32)
        chunks = []
        for c in range(2 * cg // LANES):
            t = qkv[:, c * LANES:(c + 1) * LANES]
            swapped = jnp.where(first_half, pltpu.roll(t, LANES - HEAD_DIM // 2, axis=1),
                                pltpu.roll(t, HEAD_DIM // 2, axis=1))
            chunks.append(t * cs + swapped * sn)
        qk = jnp.concatenate(chunks, axis=1)
        parts = (qk[:, :cg], qk[:, cg:2 * cg], qkv[:, 2 * cg:])
        n = ts // d
        for part, o_ref in zip(parts, out_refs[3 * gi:3 * gi + 3]):
            pb = part.astype(BF16)
            for slot, r in enumerate(_class_order(d)):
                o_ref[r] = pb[slot * n:(slot + 1) * n, :]


def _qkv_layer(x, g, cos, sin, w_groups, groups, *, ts=TOKEN_TILE):
    b, s, d_model = x.shape
    kern = functools.partial(_qkv_kernel, ts=ts, groups=groups)
    out_shapes, out_specs = [], []
    for d, cg in groups:
        for _ in range(3):
            out_shapes.append(jax.ShapeDtypeStruct((b, d, s // d, cg), BF16))
            out_specs.append(pl.BlockSpec((None, d, ts // d, cg), lambda bi, i: (bi, 0, i, 0)))
    table_spec = pl.BlockSpec((ts, LANES), lambda bi, i: (i, 0))
    return pl.pallas_call(
        kern,
        grid=(b, s // ts),
        in_specs=[pl.BlockSpec((None, ts, d_model), lambda bi, i: (bi, i, 0)), _const_spec((1, d_model)),
                  table_spec, table_spec] + [_const_spec(w.shape) for w in w_groups],
        out_specs=out_specs,
        out_shape=out_shapes,
        scratch_shapes=[pltpu.VMEM((d_model // LANES, ts, LANES), F32), pltpu.VMEM((2, ts, LANES), F32)],
        compiler_params=_params("arbitrary", "arbitrary"),
        name="qkv_rope",
    )(x, g.reshape(1, d_model), cos, sin, *w_groups)


def _attn_kernel(q_ref, k_ref, v_ref, kp_ref, vp_ref, o_ref, lse_ref, lse_scr, *, tq, w, n_heads):
    i = pl.program_id(1)
    cg = n_heads * HEAD_DIM
    nsub = tq // w
    qi = lax.broadcasted_iota(jnp.int32, (w, 2 * w), 0)
    kj = lax.broadcasted_iota(jnp.int32, (w, 2 * w), 1)
    band = jnp.logical_and(kj >= qi, kj <= qi + w)
    band_first = jnp.logical_and(band, jnp.logical_or(kj >= w, i > 0))
    lane = lax.broadcasted_iota(jnp.int32, (1, LANES), 1)
    nt = (((1,), (1,)), ((), ()))

    units = []
    for c0 in range(0, cg, LANES):
        width = min(LANES, cg - c0)
        for hh in range(width // HEAD_DIM):
            if width == LANES:
                sel = (lane < HEAD_DIM) if hh == 0 else (lane >= HEAD_DIM)
            else:
                sel = None
            units.append((len(units), slice(c0, c0 + width), hh, sel))

    def key_rows(ref, prev_ref, cols, j):
        if j == 0:
            return jnp.concatenate([prev_ref[:, cols], ref[0:w, cols]], axis=0)
        return ref[(j - 1) * w:(j + 1) * w, cols]

    def scores(unit):
        _, cols, _, sel = unit
        out = []
        for j in range(nsub):
            qb = q_ref[j * w:(j + 1) * w, cols]
            if sel is not None:
                qb = jnp.where(sel, qb, jnp.zeros_like(qb))
            out.append(lax.dot_general(qb, key_rows(k_ref, kp_ref, cols, j), nt, preferred_element_type=F32))
        return out

    def finish(unit, sc):
        h, cols, hh, _ = unit
        lo = hh * HEAD_DIM
        for j in range(nsub):
            rows = slice(j * w, (j + 1) * w)
            s = jnp.where(band_first if j == 0 else band, sc[j], NEG_INF)
            m = jnp.max(s, axis=-1, keepdims=True)
            e = jnp.exp(s - m)
            den = jnp.sum(e, axis=-1, keepdims=True)
            pv = jnp.dot(e.astype(BF16), key_rows(v_ref, vp_ref, cols, j), preferred_element_type=F32) / den
            o_ref[rows, cols.start + lo:cols.start + lo + HEAD_DIM] = pv[:, lo:lo + HEAD_DIM].astype(BF16)
            lse_scr[rows, :] = jnp.where(lane == h, m + jnp.log(den), lse_scr[rows, :])

    lse_scr[...] = jnp.full(lse_scr.shape, NEG_INF, F32)
    pending = scores(units[0])
    for u, unit in enumerate(units):
        ahead = scores(units[u + 1]) if u + 1 < len(units) else None
        finish(unit, pending)
        pending = ahead

    lse = lse_scr[...]
    m = jnp.max(lse, axis=-1, keepdims=True)
    tot = jnp.sum(jnp.exp(lse - m), axis=-1, keepdims=True)
    lse_ref[...] = jnp.broadcast_to(m + jnp.log(tot) - math.log(n_heads), lse_ref.shape)


def _attn_group(q, k, v, *, w, n_heads, tq=ATTN_Q_TILE):
    nseq, l, cg = q.shape
    tq = min(tq, l)
    ratio = tq // w
    kern = functools.partial(_attn_kernel, tq=tq, w=w, n_heads=n_heads)
    cur = pl.BlockSpec((None, tq, cg), lambda n, i: (n, i, 0))
    prev = pl.BlockSpec((None, w, cg), lambda n, i: (n, jnp.maximum(i * ratio - 1, 0), 0))
    return pl.pallas_call(
        kern,
        grid=(nseq, l // tq),
        in_specs=[cur, cur, cur, prev, prev],
        out_specs=[cur, pl.BlockSpec((None, tq, LANES), lambda n, i: (n, i, 0))],
        out_shape=[jax.ShapeDtypeStruct((nseq, l, cg), BF16),
                   jax.ShapeDtypeStruct((nseq, l, LANES), F32)],
        scratch_shapes=[pltpu.VMEM((tq, LANES), F32)],
        compiler_params=_params("arbitrary", "arbitrary"),
        name=f"attn_h{n_heads}_l{l}",
    )(q, k, v, k, v)


def _attn_out_kernel(x_ref, *refs, ts, groups):
    ng = len(groups)
    o_refs, l_refs = refs[:ng], refs[ng:2 * ng]
    wout_ref, y_ref = refs[2 * ng], refs[2 * ng + 1]
    scratch = refs[2 * ng + 2:]
    outs, lses = [], []
    si = 0
    for gi, (d, cg) in enumerate(groups):
        if d == 1:
            outs.append(o_refs[gi][0].astype(F32))
            lses.append(l_refs[gi][0])
            continue
        n = ts // d
        o_scr, l_scr = scratch[si], scratch[si + 1]
        si += 2
        n_chunks = o_scr.shape[0]
        for r in range(d):
            blk = o_refs[gi][r].astype(F32)
            for c in range(n_chunks):
                piece = blk[:, c * LANES:min((c + 1) * LANES, cg)]
                if piece.shape[1] < LANES:
                    piece = jnp.concatenate([piece, jnp.zeros((n, LANES - piece.shape[1]), F32)], axis=1)
                o_scr[c, pl.ds(r, n, stride=d), :] = piece
            l_scr[pl.ds(r, n, stride=d), :] = l_refs[gi][r]
        outs.append(jnp.concatenate([o_scr[c] for c in range(n_chunks)], axis=1)[:, :cg])
        lses.append(l_scr[...])
    m = functools.reduce(jnp.maximum, lses)
    es = [jnp.exp(l - m) for l in lses]
    inv = float(ng) / functools.reduce(lambda a, b2: a + b2, es)
    y = x_ref[...]
    row = 0
    for (d, cg), o, e in zip(groups, outs, es):
        merged = (o * (e * inv)[:, :1]).astype(BF16)
        y = y + jnp.dot(merged, wout_ref[row:row + cg, :].astype(BF16), preferred_element_type=F32)
        row += cg
    y_ref[...] = y


def _attn_out_layer(x, outs, lses, w_out, groups, *, layer, ts=TOKEN_TILE):
    b, s, d_model = x.shape
    kern = functools.partial(_attn_out_kernel, ts=ts, groups=groups)
    o_specs = [pl.BlockSpec((None, d, ts // d, cg), lambda bi, i: (bi, 0, i, 0)) for d, cg in groups]
    l_specs = [pl.BlockSpec((None, d, ts // d, LANES), lambda bi, i: (bi, 0, i, 0)) for d, cg in groups]
    scratch = []
    for d, cg in groups:
        if d > 1:
            scratch += [pltpu.VMEM((pl.cdiv(cg, LANES), ts, LANES), F32), pltpu.VMEM((ts, LANES), F32)]
    return pl.pallas_call(
        kern,
        grid=(b, s // ts),
        in_specs=[pl.BlockSpec((None, ts, d_model), lambda bi, i: (bi, i, 0))]
        + o_specs + l_specs + [_layer_spec(w_out, layer)],
        out_specs=pl.BlockSpec((None, ts, d_model), lambda bi, i: (bi, i, 0)),
        out_shape=jax.ShapeDtypeStruct(x.shape, F32),
        scratch_shapes=scratch,
        compiler_params=_params("arbitrary", "arbitrary"),
        name="attn_merge_out",
    )(x, *outs, *lses, w_out)


def _rope_tables(s):
    half = HEAD_DIM // 2
    inv_freq = 1.0 / (ROPE_THETA ** (np.arange(0, HEAD_DIM, 2, dtype=np.float64) / HEAD_DIM))
    ang = np.arange(s, dtype=np.float64)[:, None] * inv_freq[None, :]
    cos, sin = np.cos(ang), np.sin(ang)
    reps = LANES // half
    cos_t = np.concatenate([cos] * reps, axis=-1)
    sin_t = np.concatenate([-sin, sin] * (reps // 2), axis=-1)
    return jnp.asarray(cos_t, F32), jnp.asarray(sin_t, F32)


def kernel(x, norm_mix, norm_ffn, norm_final, pool_w_in, pool_w_group, pool_scale, pool_w_out,
           attn_w_qkv, attn_w_out, ffn_w_gate, ffn_w_up, ffn_w_down):
    b, s, d_model = x.shape
    assert norm_mix.shape[0] == 2, "two layers: pooling mixer then dilated attention"
    n_heads = attn_w_out.shape[1] // HEAD_DIM
    d_attn = n_heads * HEAD_DIM
    head_groups = _head_groups(n_heads)
    groups = tuple((dil, nh * HEAD_DIM) for (_, dil), nh in zip(ATTN_PATTERNS, head_groups))
    bf = lambda t: t.astype(BF16)

    x = _pool_layer(x, norm_mix[0], pool_w_in, pool_w_group, pool_scale[0], pool_w_out, layer=0)
    x = _ffn_layer(x.reshape(b * s, d_model), norm_ffn[0], ffn_w_gate, ffn_w_up, ffn_w_down, norm_final,
                   layer=0, final=False).reshape(b, s, d_model)

    w_qkv = attn_w_qkv[0]
    part_scale = (HEAD_DIM ** -0.5, 1.0, 1.0)
    w_groups, col = [], 0
    for _, cg in groups:
        w_groups.append(bf(jnp.concatenate(
            [w_qkv[:, p * d_attn + col:p * d_attn + col + cg] * part_scale[p] for p in range(3)], axis=1)))
        col += cg
    cos_t, sin_t = _rope_tables(s)
    qkv = _qkv_layer(x, norm_mix[1], cos_t, sin_t, w_groups, groups)
    outs, lses = [], []
    for gi, ((window, dil), nh) in enumerate(zip(ATTN_PATTERNS, head_groups)):
        cg = nh * HEAD_DIM
        q, k, v = (t.reshape(b * dil, s // dil, cg) for t in qkv[3 * gi:3 * gi + 3])
        o, lse = _attn_group(q, k, v, w=window // dil, n_heads=nh)
        outs.append(o.reshape(b, dil, s // dil, cg))
        lses.append(lse.reshape(b, dil, s // dil, LANES))
    x = _attn_out_layer(x, outs, lses, attn_w_out, groups, layer=0)
    x = _ffn_layer(x.reshape(b * s, d_model), norm_ffn[1], ffn_w_gate, ffn_w_up, ffn_w_down, norm_final,
                   layer=1, final=True).reshape(b, s, d_model)
    return x
```

```python
import functools
import math

import jax
import jax.numpy as jnp
import numpy as np
from jax import lax
from jax.experimental import pallas as pl
from jax.experimental.pallas import tpu as pltpu

EPS = 1e-6
POOL_WINDOWS = (2, 4, 8, 16)
HEAD_DIM = 64
ATTN_PATTERNS = ((128, 1), (512, 4), (2048, 16))
ROPE_THETA = 10000.0
NEG_INF = -1e30

LANES = 128
VMEM_LIMIT_BYTES = 56 * 1024 * 1024
TOKEN_TILE = 512
ATTN_Q_TILE = 512
FFN_CHUNK = 256
DEINT_STRIDE = 4

F32 = jnp.float32
BF16 = jnp.bfloat16


def _head_groups(n_heads):
    n = len(ATTN_PATTERNS)
    return tuple(n_heads // n + (1 if g < n_heads % n else 0) for g in range(n))


def _rmsnorm(x, g):
    ms = jnp.mean(x * x, axis=-1, keepdims=True)
    return x * lax.rsqrt(ms + EPS) * g


def _params(*sem):
    return pltpu.CompilerParams(dimension_semantics=sem, vmem_limit_bytes=VMEM_LIMIT_BYTES)


def _const_spec(shape):
    zeros = (0,) * len(shape)
    return pl.BlockSpec(shape, lambda *_: zeros, pipeline_mode=pl.Buffered(1))


def _layer_spec(stacked, layer):
    idx = (layer,) + (0,) * (stacked.ndim - 1)
    return pl.BlockSpec((None,) + stacked.shape[1:], lambda *_: idx, pipeline_mode=pl.Buffered(1))


def _swiglu_residual(x, g_ref, wg_ref, wu_ref, wd_ref):
    h = _rmsnorm(x, g_ref[...]).astype(BF16)
    acc = x
    for c in range(wg_ref.shape[-1] // FFN_CHUNK):
        sl = slice(c * FFN_CHUNK, (c + 1) * FFN_CHUNK)
        gate = jnp.dot(h, wg_ref[:, sl].astype(BF16), preferred_element_type=F32)
        up = jnp.dot(h, wu_ref[:, sl].astype(BF16), preferred_element_type=F32)
        a = (gate * jax.nn.sigmoid(gate) * up).astype(BF16)
        acc = acc + jnp.dot(a, wd_ref[sl, :].astype(BF16), preferred_element_type=F32)
    return acc


def _ffn_specs(d, w_gate, w_up, w_down, layer):
    return [_const_spec((1, d)), _layer_spec(w_gate, layer), _layer_spec(w_up, layer), _layer_spec(w_down, layer)]


def _pool_ffn_kernel(x_ref, g_ref, win_ref, wgrp_ref, scale_ref, wout_ref,
                     gf_ref, wg_ref, wu_ref, wd_ref, o_ref, ext_ref, *, ts, halo):
    i = pl.program_id(1)
    d_model = x_ref.shape[-1]
    gdim = d_model // len(POOL_WINDOWS)

    @pl.when(i == 0)
    def _():
        ext_ref[0:halo, :] = jnp.zeros((halo, d_model), F32)

    x = x_ref[...]
    h = _rmsnorm(x, g_ref[...]).astype(BF16)
    u = jnp.dot(h, win_ref[...], preferred_element_type=F32)
    ext_ref[halo:halo + ts, :] = u

    pos = i * ts + lax.broadcasted_iota(jnp.int32, (ts, 1), 0)
    zs = []
    for g, w in enumerate(POOL_WINDOWS):
        cols = slice(g * gdim, (g + 1) * gdim)
        s = ext_ref[:, cols]
        k = 1
        while k < w:
            s = s + pltpu.roll(s, k, axis=0)
            k *= 2
        inv_cnt = 1.0 / jnp.minimum(pos + 1, w).astype(F32)
        p = s[halo:, :] * inv_cnt - u[:, cols]
        zs.append(jnp.dot(p.astype(BF16), wgrp_ref[g], preferred_element_type=F32))
    z = jnp.concatenate(zs, axis=1) * scale_ref[...]
    y = jnp.dot(z.astype(BF16), wout_ref[...], preferred_element_type=F32)
    ext_ref[0:halo, :] = ext_ref[ts:ts + halo, :]
    o_ref[...] = _swiglu_residual(x + y, gf_ref, wg_ref, wu_ref, wd_ref)


def _pool_ffn_layer(x, g_mix, w_in, w_group, scale, w_out, g_ffn, w_gate, w_up, w_down, *, layer, ts=TOKEN_TILE):
    b, s, d = x.shape
    halo = max(POOL_WINDOWS)
    kern = functools.partial(_pool_ffn_kernel, ts=ts, halo=halo)
    tile = pl.BlockSpec((None, ts, d), lambda bi, i: (bi, i, 0))
    return pl.pallas_call(
        kern,
        grid=(b, s // ts),
        in_specs=[tile, _const_spec((1, d)), _layer_spec(w_in, layer), _layer_spec(w_group, layer),
                  _const_spec((1, d)), _layer_spec(w_out, layer)] + _ffn_specs(d, w_gate, w_up, w_down, layer),
        out_specs=tile,
        out_shape=jax.ShapeDtypeStruct(x.shape, F32),
        scratch_shapes=[pltpu.VMEM((ts + halo, d), F32)],
        compiler_params=_params("arbitrary", "arbitrary"),
        name="pool_ffn",
    )(x, g_mix.reshape(1, d), w_in, w_group, scale.reshape(1, d), w_out, g_ffn.reshape(1, d), w_gate, w_up, w_down)


def _class_order(d):
    order, cur = (0,), 1
    while cur < d:
        order = tuple(c + cur * r for c in order for r in range(DEINT_STRIDE))
        cur *= DEINT_STRIDE
    assert cur == d, "dilations must be powers of DEINT_STRIDE"
    return order


def _split_rows(src, n_blocks, blk):
    n = blk // DEINT_STRIDE
    return jnp.concatenate([src[pl.ds(b0 * blk + r, n, stride=DEINT_STRIDE), :]
                            for b0 in range(n_blocks) for r in range(DEINT_STRIDE)], axis=0)


def _qkv_kernel(x_ref, g_ref, cos_ref, sin_ref, *refs, ts, groups):
    ng = len(groups)
    w_refs, out_refs = refs[:ng], refs[ng:-2]
    hs_ref, tab_ref = refs[-2:]
    lane = lax.broadcasted_iota(jnp.int32, (1, LANES), 1)
    first_half = (lane % HEAD_DIM) < (HEAD_DIM // 2)
    n_chunks = x_ref.shape[-1] // LANES
    h = _rmsnorm(x_ref[...], g_ref[...])
    cs, sn = cos_ref[...], sin_ref[...]
    cur_d = 1
    for gi, (d, cg) in enumerate(groups):
        while cur_d < d:
            for c in range(n_chunks):
                hs_ref[c] = h[:, c * LANES:(c + 1) * LANES]
            tab_ref[0], tab_ref[1] = cs, sn
            blk = ts // cur_d
            h = jnp.concatenate([_split_rows(hs_ref.at[c], cur_d, blk) for c in range(n_chunks)], axis=1)
            cs, sn = _split_rows(tab_ref.at[0], cur_d, blk), _split_rows(tab_ref.at[1], cur_d, blk)
            cur_d *= DEINT_STRIDE
        assert cur_d == d, "head groups must come in increasing powers of DEINT_STRIDE"
        qkv = jnp.dot(h.astype(BF16), w_refs[gi][...], preferred_element_type=F32)
        chunks = []
        for c in range(2 * cg // LANES):
            t = qkv[:, c * LANES:(c + 1) * LANES]
            swapped = jnp.where(first_half, pltpu.roll(t, LANES - HEAD_DIM // 2, axis=1),
                                pltpu.roll(t, HEAD_DIM // 2, axis=1))
            chunks.append(t * cs + swapped * sn)
        qk = jnp.concatenate(chunks, axis=1)
        parts = (qk[:, :cg], qk[:, cg:2 * cg], qkv[:, 2 * cg:])
        n = ts // d
        for part, o_ref in zip(parts, out_refs[3 * gi:3 * gi + 3]):
            pb = part.astype(BF16)
            for slot, r in enumerate(_class_order(d)):
                o_ref[r] = pb[slot * n:(slot + 1) * n, :]


def _qkv_layer(x, g, cos, sin, w_groups, groups, *, ts=TOKEN_TILE):
    b, s, d_model = x.shape
    kern = functools.partial(_qkv_kernel, ts=ts, groups=groups)
    out_shapes, out_specs = [], []
    for d, cg in groups:
        for _ in range(3):
            out_shapes.append(jax.ShapeDtypeStruct((b, d, s // d, cg), BF16))
            out_specs.append(pl.BlockSpec((None, d, ts // d, cg), lambda bi, i: (bi, 0, i, 0)))
    table_spec = pl.BlockSpec((ts, LANES), lambda bi, i: (i, 0))
    return pl.pallas_call(
        kern,
        grid=(b, s // ts),
        in_specs=[pl.BlockSpec((None, ts, d_model), lambda bi, i: (bi, i, 0)), _const_spec((1, d_model)),
                  table_spec, table_spec] + [_const_spec(w.shape) for w in w_groups],
        out_specs=out_specs,
        out_shape=out_shapes,
        scratch_shapes=[pltpu.VMEM((d_model // LANES, ts, LANES), F32), pltpu.VMEM((2, ts, LANES), F32)],
        compiler_params=_params("arbitrary", "arbitrary"),
        name="qkv_rope",
    )(x, g.reshape(1, d_model), cos, sin, *w_groups)


def _attn_kernel(q_ref, k_ref, v_ref, kp_ref, vp_ref, o_ref, lse_ref, lse_scr, *, tq, w, n_heads):
    i = pl.program_id(1)
    cg = n_heads * HEAD_DIM
    nsub = tq // w
    qi = lax.broadcasted_iota(jnp.int32, (w, 2 * w), 0)
    kj = lax.broadcasted_iota(jnp.int32, (w, 2 * w), 1)
    band = jnp.logical_and(kj >= qi, kj <= qi + w)
    band_first = jnp.logical_and(band, jnp.logical_or(kj >= w, i > 0))
    lane = lax.broadcasted_iota(jnp.int32, (1, LANES), 1)
    nt = (((1,), (1,)), ((), ()))

    units = []
    for c0 in range(0, cg, LANES):
        width = min(LANES, cg - c0)
        for hh in range(width // HEAD_DIM):
            if width == LANES:
                sel = (lane < HEAD_DIM) if hh == 0 else (lane >= HEAD_DIM)
            else:
                sel = None
            units.append((len(units), slice(c0, c0 + width), hh, sel))

    def key_rows(ref, prev_ref, cols, j):
        if j == 0:
            return jnp.concatenate([prev_ref[:, cols], ref[0:w, cols]], axis=0)
        return ref[(j - 1) * w:(j + 1) * w, cols]

    def scores(unit):
        _, cols, _, sel = unit
        out = []
        for j in range(nsub):
            qb = q_ref[j * w:(j + 1) * w, cols]
            if sel is not None:
                qb = jnp.where(sel, qb, jnp.zeros_like(qb))
            out.append(lax.dot_general(qb, key_rows(k_ref, kp_ref, cols, j), nt, preferred_element_type=F32))
        return out

    def finish(unit, sc):
        h, cols, hh, _ = unit
        lo = hh * HEAD_DIM
        for j in range(nsub):
            rows = slice(j * w, (j + 1) * w)
            s = jnp.where(band_first if j == 0 else band, sc[j], NEG_INF)
            m = jnp.max(s, axis=-1, keepdims=True)
            e = jnp.exp(s - m)
            den = jnp.sum(e, axis=-1, keepdims=True)
            pv = jnp.dot(e.astype(BF16), key_rows(v_ref, vp_ref, cols, j), preferred_element_type=F32) / den
            o_ref[rows, cols.start + lo:cols.start + lo + HEAD_DIM] = pv[:, lo:lo + HEAD_DIM].astype(BF16)
            lse_scr[rows, :] = jnp.where(lane == h, m + jnp.log(den), lse_scr[rows, :])

    lse_scr[...] = jnp.full(lse_scr.shape, NEG_INF, F32)
    pending = scores(units[0])
    for u, unit in enumerate(units):
        ahead = scores(units[u + 1]) if u + 1 < len(units) else None
        finish(unit, pending)
        pending = ahead

    lse = lse_scr[...]
    m = jnp.max(lse, axis=-1, keepdims=True)
    tot = jnp.sum(jnp.exp(lse - m), axis=-1, keepdims=True)
    lse_ref[...] = jnp.broadcast_to(m + jnp.log(tot) - math.log(n_heads), lse_ref.shape)


def _attn_group(q, k, v, *, w, n_heads, tq=ATTN_Q_TILE):
    nseq, l, cg = q.shape
    tq = min(tq, l)
    ratio = tq // w
    kern = functools.partial(_attn_kernel, tq=tq, w=w, n_heads=n_heads)
    cur = pl.BlockSpec((None, tq, cg), lambda n, i: (n, i, 0))
    prev = pl.BlockSpec((None, w, cg), lambda n, i: (n, jnp.maximum(i * ratio - 1, 0), 0))
    return pl.pallas_call(
        kern,
        grid=(nseq, l // tq),
        in_specs=[cur, cur, cur, prev, prev],
        out_specs=[cur, pl.BlockSpec((None, tq, LANES), lambda n, i: (n, i, 0))],
        out_shape=[jax.ShapeDtypeStruct((nseq, l, cg), BF16),
                   jax.ShapeDtypeStruct((nseq, l, LANES), F32)],
        scratch_shapes=[pltpu.VMEM((tq, LANES), F32)],
        compiler_params=_params("arbitrary", "arbitrary"),
        name=f"attn_h{n_heads}_l{l}",
    )(q, k, v, k, v)


def _merge_ffn_kernel(x_ref, *refs, ts, groups):
    ng = len(groups)
    o_refs, l_refs = refs[:ng], refs[ng:2 * ng]
    wout_ref, gf_ref, wg_ref, wu_ref, wd_ref, gfin_ref, y_ref = refs[2 * ng:2 * ng + 7]
    scratch = refs[2 * ng + 7:]
    outs, lses = [], []
    si = 0
    for gi, (d, cg) in enumerate(groups):
        if d == 1:
            outs.append(o_refs[gi][0].astype(F32))
            lses.append(l_refs[gi][0])
            continue
        n = ts // d
        o_scr, l_scr = scratch[si], scratch[si + 1]
        si += 2
        n_chunks = o_scr.shape[0]
        for r in range(d):
            blk = o_refs[gi][r].astype(F32)
            for c in range(n_chunks):
                piece = blk[:, c * LANES:min((c + 1) * LANES, cg)]
                if piece.shape[1] < LANES:
                    piece = jnp.concatenate([piece, jnp.zeros((n, LANES - piece.shape[1]), F32)], axis=1)
                o_scr[c, pl.ds(r, n, stride=d), :] = piece
            l_scr[pl.ds(r, n, stride=d), :] = l_refs[gi][r]
        outs.append(jnp.concatenate([o_scr[c] for c in range(n_chunks)], axis=1)[:, :cg])
        lses.append(l_scr[...])
    m = functools.reduce(jnp.maximum, lses)
    es = [jnp.exp(l - m) for l in lses]
    inv = float(ng) / functools.reduce(lambda a, b2: a + b2, es)
    y = x_ref[...]
    row = 0
    for (d, cg), o, e in zip(groups, outs, es):
        merged = (o * (e * inv)[:, :1]).astype(BF16)
        y = y + jnp.dot(merged, wout_ref[row:row + cg, :], preferred_element_type=F32)
        row += cg
    y_ref[...] = y
    y = _swiglu_residual(y_ref[...], gf_ref, wg_ref, wu_ref, wd_ref)
    y_ref[...] = _rmsnorm(y, gfin_ref[...])


def _merge_ffn_layer(x, outs, lses, w_out, g_ffn, w_gate, w_up, w_down, g_final, groups, *,
                     attn_layer, layer, ts=TOKEN_TILE):
    b, s, d_model = x.shape
    kern = functools.partial(_merge_ffn_kernel, ts=ts, groups=groups)
    tile = pl.BlockSpec((None, ts, d_model), lambda bi, i: (bi, i, 0))
    o_specs = [pl.BlockSpec((None, d, ts // d, cg), lambda bi, i: (bi, 0, i, 0)) for d, cg in groups]
    l_specs = [pl.BlockSpec((None, d, ts // d, LANES), lambda bi, i: (bi, 0, i, 0)) for d, cg in groups]
    scratch = []
    for d, cg in groups:
        if d > 1:
            scratch += [pltpu.VMEM((pl.cdiv(cg, LANES), ts, LANES), F32), pltpu.VMEM((ts, LANES), F32)]
    return pl.pallas_call(
        kern,
        grid=(b, s // ts),
        in_specs=[tile] + o_specs + l_specs + [_layer_spec(w_out, attn_layer)]
        + _ffn_specs(d_model, w_gate, w_up, w_down, layer) + [_const_spec((1, d_model))],
        out_specs=tile,
        out_shape=jax.ShapeDtypeStruct(x.shape, F32),
        scratch_shapes=scratch,
        compiler_params=_params("arbitrary", "arbitrary"),
        name="attn_merge_ffn_final",
    )(x, *outs, *lses, w_out, g_ffn.reshape(1, d_model), w_gate, w_up, w_down, g_final.reshape(1, d_model))


def _rope_tables(s):
    half = HEAD_DIM // 2
    inv_freq = 1.0 / (ROPE_THETA ** (np.arange(0, HEAD_DIM, 2, dtype=np.float64) / HEAD_DIM))
    ang = np.arange(s, dtype=np.float64)[:, None] * inv_freq[None, :]
    cos, sin = np.cos(ang), np.sin(ang)
    reps = LANES // half
    cos_t = np.concatenate([cos] * reps, axis=-1)
    sin_t = np.concatenate([-sin, sin] * (reps // 2), axis=-1)
    return jnp.asarray(cos_t, F32), jnp.asarray(sin_t, F32)


def kernel(x, norm_mix, norm_ffn, norm_final, pool_w_in, pool_w_group, pool_scale, pool_w_out,
           attn_w_qkv, attn_w_out, ffn_w_gate, ffn_w_up, ffn_w_down):
    b, s, d_model = x.shape
    assert norm_mix.shape[0] == 2, "two layers: pooling mixer then dilated attention"
    n_heads = attn_w_out.shape[1] // HEAD_DIM
    d_attn = n_heads * HEAD_DIM
    head_groups = _head_groups(n_heads)
    groups = tuple((dil, nh * HEAD_DIM) for (_, dil), nh in zip(ATTN_PATTERNS, head_groups))
    bf = lambda t: t.astype(BF16)

    x = _pool_ffn_layer(x, norm_mix[0], bf(pool_w_in), bf(pool_w_group), pool_scale[0], bf(pool_w_out),
                        norm_ffn[0], ffn_w_gate, ffn_w_up, ffn_w_down, layer=0)

    w_qkv = attn_w_qkv[0]
    part_scale = (HEAD_DIM ** -0.5, 1.0, 1.0)
    w_groups, col = [], 0
    for _, cg in groups:
        w_groups.append(bf(jnp.concatenate(
            [w_qkv[:, p * d_attn + col:p * d_attn + col + cg] * part_scale[p] for p in range(3)], axis=1)))
        col += cg
    cos_t, sin_t = _rope_tables(s)
    qkv = _qkv_layer(x, norm_mix[1], cos_t, sin_t, w_groups, groups)
    outs, lses = [], []
    for gi, ((window, dil), nh) in enumerate(zip(ATTN_PATTERNS, head_groups)):
        cg = nh * HEAD_DIM
        q, k, v = (t.reshape(b * dil, s // dil, cg) for t in qkv[3 * gi:3 * gi + 3])
        o, lse = _attn_group(q, k, v, w=window // dil, n_heads=nh)
        outs.append(o.reshape(b, dil, s // dil, cg))
        lses.append(lse.reshape(b, dil, s // dil, LANES))
    return _merge_ffn_layer(x, outs, lses, bf(attn_w_out), norm_ffn[1], ffn_w_gate, ffn_w_up, ffn_w_down,
                            norm_final, groups, attn_layer=0, layer=1)
```

```python
import functools
import math

import jax
import jax.numpy as jnp
import numpy as np
from jax import lax
from jax.experimental import pallas as pl
from jax.experimental.pallas import tpu as pltpu

EPS = 1e-6
POOL_WINDOWS = (2, 4, 8, 16)
HEAD_DIM = 64
ATTN_PATTERNS = ((128, 1), (512, 4), (2048, 16))
ROPE_THETA = 10000.0
NEG_INF = -1e30

LANES = 128
VMEM_LIMIT_BYTES = 58 * 1024 * 1024
TOKEN_TILE = 512
ATTN_Q_TILE = 512
ATTN_LOOKAHEAD = 2
FFN_CHUNK = 256
MIXER_GAP = 3
DEINT_STRIDE = 4

F32 = jnp.float32
BF16 = jnp.bfloat16


def _head_groups(n_heads):
    n = len(ATTN_PATTERNS)
    return tuple(n_heads // n + (1 if g < n_heads % n else 0) for g in range(n))


def _rmsnorm(x, g):
    ms = jnp.mean(x * x, axis=-1, keepdims=True)
    return x * lax.rsqrt(ms + EPS) * g


def _params(*sem):
    return pltpu.CompilerParams(dimension_semantics=sem, vmem_limit_bytes=VMEM_LIMIT_BYTES)


def _const_spec(shape):
    zeros = (0,) * len(shape)
    return pl.BlockSpec(shape, lambda *_: zeros, pipeline_mode=pl.Buffered(1))


def _layer_spec(stacked, layer):
    idx = (layer,) + (0,) * (stacked.ndim - 1)
    return pl.BlockSpec((None,) + stacked.shape[1:], lambda *_: idx, pipeline_mode=pl.Buffered(1))


def _swiglu_steps(x1_ref, o_ref, g_ref, wg_ref, wu_ref, wd_ref, finish):
    x1 = x1_ref[...]
    h = _rmsnorm(x1, g_ref[...]).astype(BF16)
    acc = x1
    n_chunks = wg_ref.shape[-1] // FFN_CHUNK
    for c in range(n_chunks):
        sl = slice(c * FFN_CHUNK, (c + 1) * FFN_CHUNK)
        gate = jnp.dot(h, wg_ref[:, sl].astype(BF16), preferred_element_type=F32)
        up = jnp.dot(h, wu_ref[:, sl].astype(BF16), preferred_element_type=F32)
        a = (gate * jax.nn.sigmoid(gate) * up).astype(BF16)
        acc = acc + jnp.dot(a, wd_ref[sl, :].astype(BF16), preferred_element_type=F32)
        if c == n_chunks - 1:
            o_ref[...] = finish(acc)
        yield


def _skewed_step(t, n_tiles, make_mixer, make_ffn):
    @pl.when(t == 0)
    def _():
        for _ in make_mixer():
            pass

    @pl.when(t == n_tiles)
    def _():
        for _ in make_ffn():
            pass

    @pl.when(jnp.logical_and(t > 0, t < n_tiles))
    def _():
        mixer = make_mixer()
        for c, _ in enumerate(make_ffn()):
            if c % MIXER_GAP == MIXER_GAP - 1:
                next(mixer, None)
        for _ in mixer:
            pass


def _ffn_specs(d, w_gate, w_up, w_down, layer):
    return [_const_spec((1, d)), _layer_spec(w_gate, layer), _layer_spec(w_up, layer), _layer_spec(w_down, layer)]


def _tile_index(t, n_tiles, per_batch):
    tt = jnp.minimum(t, n_tiles - 1)
    return tt // per_batch, tt % per_batch


def _pool_steps(i, x_ref, g_ref, win_ref, wgrp_ref, scale_ref, wout_ref, ext_ref, x1_ref, *, ts, halo):
    d_model = x_ref.shape[-1]
    gdim = d_model // len(POOL_WINDOWS)
    h = _rmsnorm(x_ref[...], g_ref[...]).astype(BF16)
    ext_ref[halo:halo + ts, :] = jnp.dot(h, win_ref[...], preferred_element_type=F32)
    yield
    pos = i * ts + lax.broadcasted_iota(jnp.int32, (ts, 1), 0)
    zs = []
    for g, w in enumerate(POOL_WINDOWS):
        cols = slice(g * gdim, (g + 1) * gdim)
        s = ext_ref[:, cols]
        k = 1
        while k < w:
            s = s + pltpu.roll(s, k, axis=0)
            k *= 2
        inv_cnt = 1.0 / jnp.minimum(pos + 1, w).astype(F32)
        p = s[halo:, :] * inv_cnt - ext_ref[halo:halo + ts, cols]
        zs.append(jnp.dot(p.astype(BF16), wgrp_ref[g], preferred_element_type=F32))
    yield
    z = jnp.concatenate(zs, axis=1) * scale_ref[...]
    y = jnp.dot(z.astype(BF16), wout_ref[...], preferred_element_type=F32)
    ext_ref[0:halo, :] = ext_ref[ts:ts + halo, :]
    x1_ref[...] = x_ref[...] + y
    yield


def _pool_ffn_kernel(x_ref, g_ref, win_ref, wgrp_ref, scale_ref, wout_ref,
                     gf_ref, wg_ref, wu_ref, wd_ref, o_ref, ext_ref, x1_ref, *, ts, halo, n_tiles, per_batch):
    t = pl.program_id(0)
    i = jnp.minimum(t, n_tiles - 1) % per_batch

    @pl.when(i == 0)
    def _():
        ext_ref[0:halo, :] = jnp.zeros((halo, x_ref.shape[-1]), F32)

    make_mixer = functools.partial(_pool_steps, i, x_ref, g_ref, win_ref, wgrp_ref, scale_ref, wout_ref,
                                   ext_ref, x1_ref, ts=ts, halo=halo)
    make_ffn = functools.partial(_swiglu_steps, x1_ref, o_ref, gf_ref, wg_ref, wu_ref, wd_ref, lambda acc: acc)
    _skewed_step(t, n_tiles, make_mixer, make_ffn)


def _pool_ffn_layer(x, g_mix, w_in, w_group, scale, w_out, g_ffn, w_gate, w_up, w_down, *, layer, ts=TOKEN_TILE):
    b, s, d = x.shape
    halo = max(POOL_WINDOWS)
    per_batch = s // ts
    n_tiles = b * per_batch
    kern = functools.partial(_pool_ffn_kernel, ts=ts, halo=halo, n_tiles=n_tiles, per_batch=per_batch)
    mixer_tile = pl.BlockSpec((None, ts, d), lambda t: (*_tile_index(t, n_tiles, per_batch), 0))
    ffn_tile = pl.BlockSpec((None, ts, d), lambda t: (*_tile_index(jnp.maximum(t - 1, 0), n_tiles, per_batch), 0))
    return pl.pallas_call(
        kern,
        grid=(n_tiles + 1,),
        in_specs=[mixer_tile, _const_spec((1, d)), _layer_spec(w_in, layer), _layer_spec(w_group, layer),
                  _const_spec((1, d)), _layer_spec(w_out, layer)] + _ffn_specs(d, w_gate, w_up, w_down, layer),
        out_specs=ffn_tile,
        out_shape=jax.ShapeDtypeStruct(x.shape, F32),
        scratch_shapes=[pltpu.VMEM((ts + halo, d), F32), pltpu.VMEM((ts, d), F32)],
        compiler_params=_params("arbitrary"),
        name="pool_ffn",
    )(x, g_mix.reshape(1, d), w_in, w_group, scale.reshape(1, d), w_out, g_ffn.reshape(1, d), w_gate, w_up, w_down)


def _class_order(d):
    order, cur = (0,), 1
    while cur < d:
        order = tuple(c + cur * r for c in order for r in range(DEINT_STRIDE))
        cur *= DEINT_STRIDE
    assert cur == d, "dilations must be powers of DEINT_STRIDE"
    return order


def _split_rows(src, n_blocks, blk):
    n = blk // DEINT_STRIDE
    return jnp.concatenate([src[pl.ds(b0 * blk + r, n, stride=DEINT_STRIDE), :]
                            for b0 in range(n_blocks) for r in range(DEINT_STRIDE)], axis=0)


def _qkv_kernel(x_ref, g_ref, cos_ref, sin_ref, *refs, ts, groups):
    ng = len(groups)
    w_refs, out_refs = refs[:ng], refs[ng:-2]
    hs_ref, tab_ref = refs[-2:]
    lane = lax.broadcasted_iota(jnp.int32, (1, LANES), 1)
    first_half = (lane % HEAD_DIM) < (HEAD_DIM // 2)
    n_chunks = x_ref.shape[-1] // LANES
    h = _rmsnorm(x_ref[...], g_ref[...])
    cs, sn = cos_ref[...], sin_ref[...]
    cur_d = 1
    for gi, (d, cg) in enumerate(groups):
        while cur_d < d:
            for c in range(n_chunks):
                hs_ref[c] = h[:, c * LANES:(c + 1) * LANES]
            tab_ref[0], tab_ref[1] = cs, sn
            blk = ts // cur_d
            h = jnp.concatenate([_split_rows(hs_ref.at[c], cur_d, blk) for c in range(n_chunks)], axis=1)
            cs, sn = _split_rows(tab_ref.at[0], cur_d, blk), _split_rows(tab_ref.at[1], cur_d, blk)
            cur_d *= DEINT_STRIDE
        assert cur_d == d, "head groups must come in increasing powers of DEINT_STRIDE"
        qkv = jnp.dot(h.astype(BF16), w_refs[gi][...], preferred_element_type=F32)
        chunks = []
        for c in range(2 * cg // LANES):
            t = qkv[:, c * LANES:(c + 1) * LANES]
            swapped = jnp.where(first_half, pltpu.roll(t, LANES - HEAD_DIM // 2, axis=1),
                                pltpu.roll(t, HEAD_DIM // 2, axis=1))
            chunks.append(t * cs + swapped * sn)
        qk = jnp.concatenate(chunks, axis=1)
        parts = (qk[:, :cg], qk[:, cg:2 * cg], qkv[:, 2 * cg:])
        n = ts // d
        for part, o_ref in zip(parts, out_refs[3 * gi:3 * gi + 3]):
            pb = part.astype(BF16)
            for slot, r in enumerate(_class_order(d)):
                o_ref[r] = pb[slot * n:(slot + 1) * n, :]


def _qkv_layer(x, g, cos, sin, w_groups, groups, *, ts=TOKEN_TILE):
    b, s, d_model = x.shape
    kern = functools.partial(_qkv_kernel, ts=ts, groups=groups)
    out_shapes, out_specs = [], []
    for d, cg in groups:
        for _ in range(3):
            out_shapes.append(jax.ShapeDtypeStruct((b, d, s // d, cg), BF16))
            out_specs.append(pl.BlockSpec((None, d, ts // d, cg), lambda bi, i: (bi, 0, i, 0)))
    table_spec = pl.BlockSpec((ts, LANES), lambda bi, i: (i, 0))
    return pl.pallas_call(
        kern,
        grid=(b, s // ts),
        in_specs=[pl.BlockSpec((None, ts, d_model), lambda bi, i: (bi, i, 0)), _const_spec((1, d_model)),
                  table_spec, table_spec] + [_const_spec(w.shape) for w in w_groups],
        out_specs=out_specs,
        out_shape=out_shapes,
        scratch_shapes=[pltpu.VMEM((d_model // LANES, ts, LANES), F32), pltpu.VMEM((2, ts, LANES), F32)],
        compiler_params=_params("arbitrary", "arbitrary"),
        name="qkv_rope",
    )(x, g.reshape(1, d_model), cos, sin, *w_groups)


def _attn_kernel(q_ref, k_ref, v_ref, kp_ref, vp_ref, o_ref, lse_ref, lse_scr, *, tq, w, n_heads):
    i = pl.program_id(1)
    cg = n_heads * HEAD_DIM
    nsub = tq // w
    qi = lax.broadcasted_iota(jnp.int32, (w, 2 * w), 0)
    kj = lax.broadcasted_iota(jnp.int32, (w, 2 * w), 1)
    band = jnp.logical_and(kj >= qi, kj <= qi + w)
    band_first = jnp.logical_and(band, jnp.logical_or(kj >= w, i > 0))
    lane = lax.broadcasted_iota(jnp.int32, (1, LANES), 1)
    nt = (((1,), (1,)), ((), ()))

    units = []
    for c0 in range(0, cg, LANES):
        width = min(LANES, cg - c0)
        for hh in range(width // HEAD_DIM):
            if width == LANES:
                sel = (lane < HEAD_DIM) if hh == 0 else (lane >= HEAD_DIM)
            else:
                sel = None
            units.append((len(units), slice(c0, c0 + width), hh, sel))

    def key_rows(ref, prev_ref, cols, j):
        if j == 0:
            return jnp.concatenate([prev_ref[:, cols], ref[0:w, cols]], axis=0)
        return ref[(j - 1) * w:(j + 1) * w, cols]

    def scores(item):
        (_, cols, _, sel), j = item
        qb = q_ref[j * w:(j + 1) * w, cols]
        if sel is not None:
            qb = jnp.where(sel, qb, jnp.zeros_like(qb))
        return lax.dot_general(qb, key_rows(k_ref, kp_ref, cols, j), nt, preferred_element_type=F32)

    def finish(item, sc):
        (h, cols, hh, _), j = item
        lo = hh * HEAD_DIM
        rows = slice(j * w, (j + 1) * w)
        s = jnp.where(band_first if j == 0 else band, sc, NEG_INF)
        m = jnp.max(s, axis=-1, keepdims=True)
        e = jnp.exp(s - m)
        den = jnp.sum(e, axis=-1, keepdims=True)
        pv = jnp.dot(e.astype(BF16), key_rows(v_ref, vp_ref, cols, j), preferred_element_type=F32) / den
        o_ref[rows, cols.start + lo:cols.start + lo + HEAD_DIM] = pv[:, lo:lo + HEAD_DIM].astype(BF16)
        lse_scr[rows, :] = jnp.where(lane == h, m + jnp.log(den), lse_scr[rows, :])

    lse_scr[...] = jnp.full(lse_scr.shape, NEG_INF, F32)
    items = [(unit, j) for unit in units for j in range(nsub)]
    pending = [scores(item) for item in items[:ATTN_LOOKAHEAD]]
    for n, item in enumerate(items):
        if n + ATTN_LOOKAHEAD < len(items):
            pending.append(scores(items[n + ATTN_LOOKAHEAD]))
        finish(item, pending.pop(0))

    lse = lse_scr[...]
    m = jnp.max(lse, axis=-1, keepdims=True)
    tot = jnp.sum(jnp.exp(lse - m), axis=-1, keepdims=True)
    lse_ref[...] = jnp.broadcast_to(m + jnp.log(tot) - math.log(n_heads), lse_ref.shape)


def _attn_group(q, k, v, *, w, n_heads, tq=ATTN_Q_TILE):
    nseq, l, cg = q.shape
    tq = min(tq, l)
    ratio = tq // w
    kern = functools.partial(_attn_kernel, tq=tq, w=w, n_heads=n_heads)
    cur = pl.BlockSpec((None, tq, cg), lambda n, i: (n, i, 0))
    prev = pl.BlockSpec((None, w, cg), lambda n, i: (n, jnp.maximum(i * ratio - 1, 0), 0))
    return pl.pallas_call(
        kern,
        grid=(nseq, l // tq),
        in_specs=[cur, cur, cur, prev, prev],
        out_specs=[cur, pl.BlockSpec((None, tq, LANES), lambda n, i: (n, i, 0))],
        out_shape=[jax.ShapeDtypeStruct((nseq, l, cg), BF16),
                   jax.ShapeDtypeStruct((nseq, l, LANES), F32)],
        scratch_shapes=[pltpu.VMEM((tq, LANES), F32)],
        compiler_params=_params("arbitrary", "arbitrary"),
        name=f"attn_h{n_heads}_l{l}",
    )(q, k, v, k, v)


def _merge_steps(x_ref, o_refs, l_refs, wout_ref, scratch, x1_ref, *, ts, groups):
    ng = len(groups)
    outs, lses = [], []
    si = 0
    for gi, (d, cg) in enumerate(groups):
        if d == 1:
            outs.append(o_refs[gi][0].astype(F32))
            lses.append(l_refs[gi][0])
            continue
        n = ts // d
        o_scr, l_scr = scratch[si], scratch[si + 1]
        si += 2
        n_chunks = o_scr.shape[0]
        for r in range(d):
            blk = o_refs[gi][r].astype(F32)
            for c in range(n_chunks):
                piece = blk[:, c * LANES:min((c + 1) * LANES, cg)]
                if piece.shape[1] < LANES:
                    piece = jnp.concatenate([piece, jnp.zeros((n, LANES - piece.shape[1]), F32)], axis=1)
                o_scr[c, pl.ds(r, n, stride=d), :] = piece
            l_scr[pl.ds(r, n, stride=d), :] = l_refs[gi][r]
        outs.append(jnp.concatenate([o_scr[c] for c in range(n_chunks)], axis=1)[:, :cg])
        lses.append(l_scr[...])
    m = functools.reduce(jnp.maximum, lses)
    es = [jnp.exp(l - m) for l in lses]
    inv = float(ng) / functools.reduce(lambda a, b2: a + b2, es)
    merged = [(o * (e * inv)[:, :1]).astype(BF16) for o, e in zip(outs, es)]
    yield
    y = x_ref[...]
    row = 0
    for (d, cg), mg in zip(groups, merged):
        y = y + jnp.dot(mg, wout_ref[row:row + cg, :], preferred_element_type=F32)
        row += cg
    x1_ref[...] = y
    yield


def _merge_ffn_kernel(x_ref, *refs, ts, groups, n_tiles):
    ng = len(groups)
    o_refs, l_refs = refs[:ng], refs[ng:2 * ng]
    wout_ref, gf_ref, wg_ref, wu_ref, wd_ref, gfin_ref, y_ref, x1_ref = refs[2 * ng:2 * ng + 8]
    scratch = refs[2 * ng + 8:]
    t = pl.program_id(0)
    make_mixer = functools.partial(_merge_steps, x_ref, o_refs, l_refs, wout_ref, scratch, x1_ref,
                                   ts=ts, groups=groups)
    make_ffn = functools.partial(_swiglu_steps, x1_ref, y_ref, gf_ref, wg_ref, wu_ref, wd_ref,
                                 lambda acc: _rmsnorm(acc, gfin_ref[...]))
    _skewed_step(t, n_tiles, make_mixer, make_ffn)


def _merge_ffn_layer(x, outs, lses, w_out, g_ffn, w_gate, w_up, w_down, g_final, groups, *,
                     attn_layer, layer, ts=TOKEN_TILE):
    b, s, d_model = x.shape
    per_batch = s // ts
    n_tiles = b * per_batch
    kern = functools.partial(_merge_ffn_kernel, ts=ts, groups=groups, n_tiles=n_tiles)

    def mixer_spec(block):
        lead = len(block) - 2
        return pl.BlockSpec(block, lambda t: (_tile_index(t, n_tiles, per_batch)[0],) + (0,) * (lead - 1)
                            + (_tile_index(t, n_tiles, per_batch)[1], 0))

    ffn_tile = pl.BlockSpec((None, ts, d_model),
                            lambda t: (*_tile_index(jnp.maximum(t - 1, 0), n_tiles, per_batch), 0))
    o_specs = [mixer_spec((None, d, ts // d, cg)) for d, cg in groups]
    l_specs = [mixer_spec((None, d, ts // d, LANES)) for d, cg in groups]
    scratch = [pltpu.VMEM((ts, d_model), F32)]
    for d, cg in groups:
        if d > 1:
            scratch += [pltpu.VMEM((pl.cdiv(cg, LANES), ts, LANES), F32), pltpu.VMEM((ts, LANES), F32)]
    return pl.pallas_call(
        kern,
        grid=(n_tiles + 1,),
        in_specs=[mixer_spec((None, ts, d_model))] + o_specs + l_specs + [_layer_spec(w_out, attn_layer)]
        + _ffn_specs(d_model, w_gate, w_up, w_down, layer) + [_const_spec((1, d_model))],
        out_specs=ffn_tile,
        out_shape=jax.ShapeDtypeStruct(x.shape, F32),
        scratch_shapes=scratch,
        compiler_params=_params("arbitrary"),
        name="attn_merge_ffn_final",
    )(x, *outs, *lses, w_out, g_ffn.reshape(1, d_model), w_gate, w_up, w_down, g_final.reshape(1, d_model))


def _rope_tables(s):
    half = HEAD_DIM // 2
    inv_freq = 1.0 / (ROPE_THETA ** (np.arange(0, HEAD_DIM, 2, dtype=np.float64) / HEAD_DIM))
    ang = np.arange(s, dtype=np.float64)[:, None] * inv_freq[None, :]
    cos, sin = np.cos(ang), np.sin(ang)
    reps = LANES // half
    cos_t = np.concatenate([cos] * reps, axis=-1)
    sin_t = np.concatenate([-sin, sin] * (reps // 2), axis=-1)
    return jnp.asarray(cos_t, F32), jnp.asarray(sin_t, F32)


def kernel(x, norm_mix, norm_ffn, norm_final, pool_w_in, pool_w_group, pool_scale, pool_w_out,
           attn_w_qkv, attn_w_out, ffn_w_gate, ffn_w_up, ffn_w_down):
    b, s, d_model = x.shape
    assert norm_mix.shape[0] == 2, "two layers: pooling mixer then dilated attention"
    n_heads = attn_w_out.shape[1] // HEAD_DIM
    d_attn = n_heads * HEAD_DIM
    head_groups = _head_groups(n_heads)
    groups = tuple((dil, nh * HEAD_DIM) for (_, dil), nh in zip(ATTN_PATTERNS, head_groups))
    bf = lambda t: t.astype(BF16)

    x = _pool_ffn_layer(x, norm_mix[0], bf(pool_w_in), bf(pool_w_group), pool_scale[0], bf(pool_w_out),
                        norm_ffn[0], ffn_w_gate, ffn_w_up, ffn_w_down, layer=0)

    w_qkv = attn_w_qkv[0]
    part_scale = (HEAD_DIM ** -0.5, 1.0, 1.0)
    w_groups, col = [], 0
    for _, cg in groups:
        w_groups.append(bf(jnp.concatenate(
            [w_qkv[:, p * d_attn + col:p * d_attn + col + cg] * part_scale[p] for p in range(3)], axis=1)))
        col += cg
    cos_t, sin_t = _rope_tables(s)
    qkv = _qkv_layer(x, norm_mix[1], cos_t, sin_t, w_groups, groups)
    outs, lses = [], []
    for gi, ((window, dil), nh) in enumerate(zip(ATTN_PATTERNS, head_groups)):
        cg = nh * HEAD_DIM
        q, k, v = (t.reshape(b * dil, s // dil, cg) for t in qkv[3 * gi:3 * gi + 3])
        o, lse = _attn_group(q, k, v, w=window // dil, n_heads=nh)
        outs.append(o.reshape(b, dil, s // dil, cg))
        lses.append(lse.reshape(b, dil, s // dil, LANES))
    return _merge_ffn_layer(x, outs, lses, bf(attn_w_out), norm_ffn[1], ffn_w_gate, ffn_w_up, ffn_w_down,
                            norm_final, groups, attn_layer=0, layer=1)
```

```python
import functools
import math

import jax
import jax.numpy as jnp
import numpy as np
from jax import lax
from jax.experimental import pallas as pl
from jax.experimental.pallas import tpu as pltpu

EPS = 1e-6
POOL_WINDOWS = (2, 4, 8, 16)
HEAD_DIM = 64
ATTN_PATTERNS = ((128, 1), (512, 4), (2048, 16))
ROPE_THETA = 10000.0
NEG_INF = -1e30

LANES = 128
VMEM_LIMIT_BYTES = 58 * 1024 * 1024
TOKEN_TILE = 512
QKV_TILE = 1024
ATTN_Q_TILE = 1024
ATTN_LOOKAHEAD = 2
FFN_CHUNK = 256
MIXER_GAP = 3
DEINT_STRIDE = 4

F32 = jnp.float32
BF16 = jnp.bfloat16


def _head_groups(n_heads):
    n = len(ATTN_PATTERNS)
    return tuple(n_heads // n + (1 if g < n_heads % n else 0) for g in range(n))


def _rmsnorm(x, g):
    ms = jnp.mean(x * x, axis=-1, keepdims=True)
    return x * lax.rsqrt(ms + EPS) * g


def _params(*sem):
    return pltpu.CompilerParams(dimension_semantics=sem, vmem_limit_bytes=VMEM_LIMIT_BYTES)


def _const_spec(shape):
    zeros = (0,) * len(shape)
    return pl.BlockSpec(shape, lambda *_: zeros, pipeline_mode=pl.Buffered(1))


def _layer_spec(stacked, layer):
    idx = (layer,) + (0,) * (stacked.ndim - 1)
    return pl.BlockSpec((None,) + stacked.shape[1:], lambda *_: idx, pipeline_mode=pl.Buffered(1))


def _swiglu_steps(x1_ref, o_ref, g_ref, wg_ref, wu_ref, wd_ref, finish):
    x1 = x1_ref[...]
    h = _rmsnorm(x1, g_ref[...]).astype(BF16)
    acc = x1
    n_chunks = wg_ref.shape[-1] // FFN_CHUNK
    for c in range(n_chunks):
        sl = slice(c * FFN_CHUNK, (c + 1) * FFN_CHUNK)
        gate = jnp.dot(h, wg_ref[:, sl].astype(BF16), preferred_element_type=F32)
        up = jnp.dot(h, wu_ref[:, sl].astype(BF16), preferred_element_type=F32)
        a = (gate * jax.nn.sigmoid(gate) * up).astype(BF16)
        acc = acc + jnp.dot(a, wd_ref[sl, :].astype(BF16), preferred_element_type=F32)
        if c == n_chunks - 1:
            o_ref[...] = finish(acc)
        yield


def _skewed_step(t, n_tiles, make_mixer, make_ffn):
    @pl.when(t == 0)
    def _():
        for _ in make_mixer():
            pass

    @pl.when(t == n_tiles)
    def _():
        for _ in make_ffn():
            pass

    @pl.when(jnp.logical_and(t > 0, t < n_tiles))
    def _():
        mixer = make_mixer()
        for c, _ in enumerate(make_ffn()):
            if c % MIXER_GAP == MIXER_GAP - 1:
                next(mixer, None)
        for _ in mixer:
            pass


def _ffn_specs(d, w_gate, w_up, w_down, layer):
    return [_const_spec((1, d)), _layer_spec(w_gate, layer), _layer_spec(w_up, layer), _layer_spec(w_down, layer)]


def _tile_index(t, n_tiles, per_batch):
    tt = jnp.minimum(t, n_tiles - 1)
    return tt // per_batch, tt % per_batch


def _pool_steps(i, x_ref, g_ref, win_ref, wgrp_ref, scale_ref, wout_ref, ext_ref, x1_ref, *, ts, halo):
    d_model = x_ref.shape[-1]
    gdim = d_model // len(POOL_WINDOWS)
    h = _rmsnorm(x_ref[...], g_ref[...]).astype(BF16)
    ext_ref[halo:halo + ts, :] = jnp.dot(h, win_ref[...], preferred_element_type=F32)
    yield
    pos = i * ts + lax.broadcasted_iota(jnp.int32, (ts, 1), 0)
    zs = []
    for g, w in enumerate(POOL_WINDOWS):
        cols = slice(g * gdim, (g + 1) * gdim)
        s = ext_ref[:, cols]
        k = 1
        while k < w:
            s = s + pltpu.roll(s, k, axis=0)
            k *= 2
        inv_cnt = 1.0 / jnp.minimum(pos + 1, w).astype(F32)
        p = s[halo:, :] * inv_cnt - ext_ref[halo:halo + ts, cols]
        zs.append(jnp.dot(p.astype(BF16), wgrp_ref[g], preferred_element_type=F32))
    yield
    z = jnp.concatenate(zs, axis=1) * scale_ref[...]
    y = jnp.dot(z.astype(BF16), wout_ref[...], preferred_element_type=F32)
    ext_ref[0:halo, :] = ext_ref[ts:ts + halo, :]
    x1_ref[...] = x_ref[...] + y
    yield


def _pool_ffn_kernel(x_ref, g_ref, win_ref, wgrp_ref, scale_ref, wout_ref,
                     gf_ref, wg_ref, wu_ref, wd_ref, o_ref, ext_ref, x1_ref, *, ts, halo, n_tiles, per_batch):
    t = pl.program_id(0)
    i = jnp.minimum(t, n_tiles - 1) % per_batch

    @pl.when(i == 0)
    def _():
        ext_ref[0:halo, :] = jnp.zeros((halo, x_ref.shape[-1]), F32)

    make_mixer = functools.partial(_pool_steps, i, x_ref, g_ref, win_ref, wgrp_ref, scale_ref, wout_ref,
                                   ext_ref, x1_ref, ts=ts, halo=halo)
    make_ffn = functools.partial(_swiglu_steps, x1_ref, o_ref, gf_ref, wg_ref, wu_ref, wd_ref, lambda acc: acc)
    _skewed_step(t, n_tiles, make_mixer, make_ffn)


def _pool_ffn_layer(x, g_mix, w_in, w_group, scale, w_out, g_ffn, w_gate, w_up, w_down, *, layer, ts=TOKEN_TILE):
    b, s, d = x.shape
    halo = max(POOL_WINDOWS)
    per_batch = s // ts
    n_tiles = b * per_batch
    kern = functools.partial(_pool_ffn_kernel, ts=ts, halo=halo, n_tiles=n_tiles, per_batch=per_batch)
    mixer_tile = pl.BlockSpec((None, ts, d), lambda t: (*_tile_index(t, n_tiles, per_batch), 0))
    ffn_tile = pl.BlockSpec((None, ts, d), lambda t: (*_tile_index(jnp.maximum(t - 1, 0), n_tiles, per_batch), 0))
    return pl.pallas_call(
        kern,
        grid=(n_tiles + 1,),
        in_specs=[mixer_tile, _const_spec((1, d)), _layer_spec(w_in, layer), _layer_spec(w_group, layer),
                  _const_spec((1, d)), _layer_spec(w_out, layer)] + _ffn_specs(d, w_gate, w_up, w_down, layer),
        out_specs=ffn_tile,
        out_shape=jax.ShapeDtypeStruct(x.shape, F32),
        scratch_shapes=[pltpu.VMEM((ts + halo, d), F32), pltpu.VMEM((ts, d), F32)],
        compiler_params=_params("arbitrary"),
        name="pool_ffn",
    )(x, g_mix.reshape(1, d), w_in, w_group, scale.reshape(1, d), w_out, g_ffn.reshape(1, d), w_gate, w_up, w_down)


def _class_order(d):
    order, cur = (0,), 1
    while cur < d:
        order = tuple(c + cur * r for c in order for r in range(DEINT_STRIDE))
        cur *= DEINT_STRIDE
    assert cur == d, "dilations must be powers of DEINT_STRIDE"
    return order


def _split_rows(src, n_blocks, blk):
    n = blk // DEINT_STRIDE
    return jnp.concatenate([src[pl.ds(b0 * blk + r, n, stride=DEINT_STRIDE), :]
                            for b0 in range(n_blocks) for r in range(DEINT_STRIDE)], axis=0)


def _qkv_kernel(x_ref, g_ref, cos_ref, sin_ref, *refs, ts, groups):
    ng = len(groups)
    w_refs, out_refs = refs[:ng], refs[ng:-2]
    hs_ref, tab_ref = refs[-2:]
    lane = lax.broadcasted_iota(jnp.int32, (1, LANES), 1)
    first_half = (lane % HEAD_DIM) < (HEAD_DIM // 2)
    n_chunks = x_ref.shape[-1] // LANES
    h = _rmsnorm(x_ref[...], g_ref[...])
    cs, sn = cos_ref[...], sin_ref[...]
    cur_d = 1
    for gi, (d, cg) in enumerate(groups):
        while cur_d < d:
            for c in range(n_chunks):
                hs_ref[c] = h[:, c * LANES:(c + 1) * LANES]
            tab_ref[0], tab_ref[1] = cs, sn
            blk = ts // cur_d
            h = jnp.concatenate([_split_rows(hs_ref.at[c], cur_d, blk) for c in range(n_chunks)], axis=1)
            cs, sn = _split_rows(tab_ref.at[0], cur_d, blk), _split_rows(tab_ref.at[1], cur_d, blk)
            cur_d *= DEINT_STRIDE
        assert cur_d == d, "head groups must come in increasing powers of DEINT_STRIDE"
        qkv = jnp.dot(h.astype(BF16), w_refs[gi][...], preferred_element_type=F32)
        chunks = []
        for c in range(2 * cg // LANES):
            t = qkv[:, c * LANES:(c + 1) * LANES]
            swapped = jnp.where(first_half, pltpu.roll(t, LANES - HEAD_DIM // 2, axis=1),
                                pltpu.roll(t, HEAD_DIM // 2, axis=1))
            chunks.append(t * cs + swapped * sn)
        qk = jnp.concatenate(chunks, axis=1)
        parts = (qk[:, :cg], qk[:, cg:2 * cg], qkv[:, 2 * cg:])
        n = ts // d
        for part, o_ref in zip(parts, out_refs[3 * gi:3 * gi + 3]):
            pb = part.astype(BF16)
            for slot, r in enumerate(_class_order(d)):
                o_ref[r] = pb[slot * n:(slot + 1) * n, :]


def _qkv_layer(x, g, cos, sin, w_groups, groups, *, ts=QKV_TILE):
    b, s, d_model = x.shape
    kern = functools.partial(_qkv_kernel, ts=ts, groups=groups)
    out_shapes, out_specs = [], []
    for d, cg in groups:
        for _ in range(3):
            out_shapes.append(jax.ShapeDtypeStruct((b, d, s // d, cg), BF16))
            out_specs.append(pl.BlockSpec((None, d, ts // d, cg), lambda bi, i: (bi, 0, i, 0)))
    table_spec = pl.BlockSpec((ts, LANES), lambda bi, i: (i, 0))
    return pl.pallas_call(
        kern,
        grid=(b, s // ts),
        in_specs=[pl.BlockSpec((None, ts, d_model), lambda bi, i: (bi, i, 0)), _const_spec((1, d_model)),
                  table_spec, table_spec] + [_const_spec(w.shape) for w in w_groups],
        out_specs=out_specs,
        out_shape=out_shapes,
        scratch_shapes=[pltpu.VMEM((d_model // LANES, ts, LANES), F32), pltpu.VMEM((2, ts, LANES), F32)],
        compiler_params=_params("arbitrary", "arbitrary"),
        name="qkv_rope",
    )(x, g.reshape(1, d_model), cos, sin, *w_groups)


def _attn_kernel(q_ref, k_ref, v_ref, kp_ref, vp_ref, o_ref, lse_ref, lse_scr, *, tq, w, n_heads):
    i = pl.program_id(1)
    cg = n_heads * HEAD_DIM
    nsub = tq // w
    qi = lax.broadcasted_iota(jnp.int32, (w, 2 * w), 0)
    kj = lax.broadcasted_iota(jnp.int32, (w, 2 * w), 1)
    band = jnp.logical_and(kj >= qi, kj <= qi + w)
    band_first = jnp.logical_and(band, jnp.logical_or(kj >= w, i > 0))
    lane = lax.broadcasted_iota(jnp.int32, (1, LANES), 1)
    nt = (((1,), (1,)), ((), ()))

    units = []
    for c0 in range(0, cg, LANES):
        width = min(LANES, cg - c0)
        for hh in range(width // HEAD_DIM):
            if width == LANES:
                sel = (lane < HEAD_DIM) if hh == 0 else (lane >= HEAD_DIM)
            else:
                sel = None
            units.append((len(units), slice(c0, c0 + width), hh, sel))

    def key_rows(ref, prev_ref, cols, j):
        if j == 0:
            return jnp.concatenate([prev_ref[:, cols], ref[0:w, cols]], axis=0)
        return ref[(j - 1) * w:(j + 1) * w, cols]

    def scores(item):
        (_, cols, _, sel), j = item
        qb = q_ref[j * w:(j + 1) * w, cols]
        if sel is not None:
            qb = jnp.where(sel, qb, jnp.zeros_like(qb))
        return lax.dot_general(qb, key_rows(k_ref, kp_ref, cols, j), nt, preferred_element_type=F32)

    def finish(item, sc):
        (h, cols, hh, _), j = item
        lo = hh * HEAD_DIM
        rows = slice(j * w, (j + 1) * w)
        s = jnp.where(band_first if j == 0 else band, sc, NEG_INF)
        m = jnp.max(s, axis=-1, keepdims=True)
        e = jnp.exp(s - m)
        den = jnp.sum(e, axis=-1, keepdims=True)
        pv = jnp.dot(e.astype(BF16), key_rows(v_ref, vp_ref, cols, j), preferred_element_type=F32) / den
        o_ref[rows, cols.start + lo:cols.start + lo + HEAD_DIM] = pv[:, lo:lo + HEAD_DIM].astype(BF16)
        lse_scr[rows, :] = jnp.where(lane == h, m + jnp.log(den), lse_scr[rows, :])

    lse_scr[...] = jnp.full(lse_scr.shape, NEG_INF, F32)
    items = [(unit, j) for unit in units for j in range(nsub)]
    pending = [scores(item) for item in items[:ATTN_LOOKAHEAD]]
    for n, item in enumerate(items):
        if n + ATTN_LOOKAHEAD < len(items):
            pending.append(scores(items[n + ATTN_LOOKAHEAD]))
        finish(item, pending.pop(0))

    lse = lse_scr[...]
    m = jnp.max(lse, axis=-1, keepdims=True)
    tot = jnp.sum(jnp.exp(lse - m), axis=-1, keepdims=True)
    lse_ref[...] = jnp.broadcast_to(m + jnp.log(tot) - math.log(n_heads), lse_ref.shape)


def _attn_group(q, k, v, *, w, n_heads, tq=ATTN_Q_TILE):
    nseq, l, cg = q.shape
    tq = min(tq, l)
    ratio = tq // w
    kern = functools.partial(_attn_kernel, tq=tq, w=w, n_heads=n_heads)
    cur = pl.BlockSpec((None, tq, cg), lambda n, i: (n, i, 0))
    prev = pl.BlockSpec((None, w, cg), lambda n, i: (n, jnp.maximum(i * ratio - 1, 0), 0))
    return pl.pallas_call(
        kern,
        grid=(nseq, l // tq),
        in_specs=[cur, cur, cur, prev, prev],
        out_specs=[cur, pl.BlockSpec((None, tq, LANES), lambda n, i: (n, i, 0))],
        out_shape=[jax.ShapeDtypeStruct((nseq, l, cg), BF16),
                   jax.ShapeDtypeStruct((nseq, l, LANES), F32)],
        scratch_shapes=[pltpu.VMEM((tq, LANES), F32)],
        compiler_params=_params("arbitrary", "arbitrary"),
        name=f"attn_h{n_heads}_l{l}",
    )(q, k, v, k, v)


def _merge_steps(x_ref, o_refs, l_refs, wout_ref, scratch, x1_ref, *, ts, groups):
    ng = len(groups)
    outs, lses = [], []
    si = 0
    for gi, (d, cg) in enumerate(groups):
        if d == 1:
            outs.append(o_refs[gi][0].astype(F32))
            lses.append(l_refs[gi][0])
            continue
        n = ts // d
        o_scr, l_scr = scratch[si], scratch[si + 1]
        si += 2
        n_chunks = o_scr.shape[0]
        for r in range(d):
            blk = o_refs[gi][r].astype(F32)
            for c in range(n_chunks):
                piece = blk[:, c * LANES:min((c + 1) * LANES, cg)]
                if piece.shape[1] < LANES:
                    piece = jnp.concatenate([piece, jnp.zeros((n, LANES - piece.shape[1]), F32)], axis=1)
                o_scr[c, pl.ds(r, n, stride=d), :] = piece
            l_scr[pl.ds(r, n, stride=d), :] = l_refs[gi][r]
        outs.append(jnp.concatenate([o_scr[c] for c in range(n_chunks)], axis=1)[:, :cg])
        lses.append(l_scr[...])
    m = functools.reduce(jnp.maximum, lses)
    es = [jnp.exp(l - m) for l in lses]
    inv = float(ng) / functools.reduce(lambda a, b2: a + b2, es)
    merged = [(o * (e * inv)[:, :1]).astype(BF16) for o, e in zip(outs, es)]
    yield
    y = x_ref[...]
    row = 0
    for (d, cg), mg in zip(groups, merged):
        y = y + jnp.dot(mg, wout_ref[row:row + cg, :], preferred_element_type=F32)
        row += cg
    x1_ref[...] = y
    yield


def _merge_ffn_kernel(x_ref, *refs, ts, groups, n_tiles):
    ng = len(groups)
    o_refs, l_refs = refs[:ng], refs[ng:2 * ng]
    wout_ref, gf_ref, wg_ref, wu_ref, wd_ref, gfin_ref, y_ref, x1_ref = refs[2 * ng:2 * ng + 8]
    scratch = refs[2 * ng + 8:]
    t = pl.program_id(0)
    make_mixer = functools.partial(_merge_steps, x_ref, o_refs, l_refs, wout_ref, scratch, x1_ref,
                                   ts=ts, groups=groups)
    make_ffn = functools.partial(_swiglu_steps, x1_ref, y_ref, gf_ref, wg_ref, wu_ref, wd_ref,
                                 lambda acc: _rmsnorm(acc, gfin_ref[...]))
    _skewed_step(t, n_tiles, make_mixer, make_ffn)


def _merge_ffn_layer(x, outs, lses, w_out, g_ffn, w_gate, w_up, w_down, g_final, groups, *,
                     attn_layer, layer, ts=TOKEN_TILE):
    b, s, d_model = x.shape
    per_batch = s // ts
    n_tiles = b * per_batch
    kern = functools.partial(_merge_ffn_kernel, ts=ts, groups=groups, n_tiles=n_tiles)

    def mixer_spec(block):
        lead = len(block) - 2
        return pl.BlockSpec(block, lambda t: (_tile_index(t, n_tiles, per_batch)[0],) + (0,) * (lead - 1)
                            + (_tile_index(t, n_tiles, per_batch)[1], 0))

    ffn_tile = pl.BlockSpec((None, ts, d_model),
                            lambda t: (*_tile_index(jnp.maximum(t - 1, 0), n_tiles, per_batch), 0))
    o_specs = [mixer_spec((None, d, ts // d, cg)) for d, cg in groups]
    l_specs = [mixer_spec((None, d, ts // d, LANES)) for d, cg in groups]
    scratch = [pltpu.VMEM((ts, d_model), F32)]
    for d, cg in groups:
        if d > 1:
            scratch += [pltpu.VMEM((pl.cdiv(cg, LANES), ts, LANES), F32), pltpu.VMEM((ts, LANES), F32)]
    return pl.pallas_call(
        kern,
        grid=(n_tiles + 1,),
        in_specs=[mixer_spec((None, ts, d_model))] + o_specs + l_specs + [_layer_spec(w_out, attn_layer)]
        + _ffn_specs(d_model, w_gate, w_up, w_down, layer) + [_const_spec((1, d_model))],
        out_specs=ffn_tile,
        out_shape=jax.ShapeDtypeStruct(x.shape, F32),
        scratch_shapes=scratch,
        compiler_params=_params("arbitrary"),
        name="attn_merge_ffn_final",
    )(x, *outs, *lses, w_out, g_ffn.reshape(1, d_model), w_gate, w_up, w_down, g_final.reshape(1, d_model))


def _rope_tables(s):
    half = HEAD_DIM // 2
    inv_freq = 1.0 / (ROPE_THETA ** (np.arange(0, HEAD_DIM, 2, dtype=np.float64) / HEAD_DIM))
    ang = np.arange(s, dtype=np.float64)[:, None] * inv_freq[None, :]
    cos, sin = np.cos(ang), np.sin(ang)
    reps = LANES // half
    cos_t = np.concatenate([cos] * reps, axis=-1)
    sin_t = np.concatenate([-sin, sin] * (reps // 2), axis=-1)
    return jnp.asarray(cos_t, F32), jnp.asarray(sin_t, F32)


def kernel(x, norm_mix, norm_ffn, norm_final, pool_w_in, pool_w_group, pool_scale, pool_w_out,
           attn_w_qkv, attn_w_out, ffn_w_gate, ffn_w_up, ffn_w_down):
    b, s, d_model = x.shape
    assert norm_mix.shape[0] == 2, "two layers: pooling mixer then dilated attention"
    n_heads = attn_w_out.shape[1] // HEAD_DIM
    d_attn = n_heads * HEAD_DIM
    head_groups = _head_groups(n_heads)
    groups = tuple((dil, nh * HEAD_DIM) for (_, dil), nh in zip(ATTN_PATTERNS, head_groups))
    bf = lambda t: t.astype(BF16)

    x = _pool_ffn_layer(x, norm_mix[0], bf(pool_w_in), bf(pool_w_group), pool_scale[0], bf(pool_w_out),
                        norm_ffn[0], ffn_w_gate, ffn_w_up, ffn_w_down, layer=0)

    w_qkv = attn_w_qkv[0]
    part_scale = (HEAD_DIM ** -0.5, 1.0, 1.0)
    w_groups, col = [], 0
    for _, cg in groups:
        w_groups.append(bf(jnp.concatenate(
            [w_qkv[:, p * d_attn + col:p * d_attn + col + cg] * part_scale[p] for p in range(3)], axis=1)))
        col += cg
    cos_t, sin_t = _rope_tables(s)
    qkv = _qkv_layer(x, norm_mix[1], cos_t, sin_t, w_groups, groups)
    outs, lses = [], []
    for gi, ((window, dil), nh) in enumerate(zip(ATTN_PATTERNS, head_groups)):
        cg = nh * HEAD_DIM
        q, k, v = (t.reshape(b * dil, s // dil, cg) for t in qkv[3 * gi:3 * gi + 3])
        o, lse = _attn_group(q, k, v, w=window // dil, n_heads=nh)
        outs.append(o.reshape(b, dil, s // dil, cg))
        lses.append(lse.reshape(b, dil, s // dil, LANES))
    return _merge_ffn_layer(x, outs, lses, bf(attn_w_out), norm_ffn[1], ffn_w_gate, ffn_w_up, ffn_w_down,
                            norm_final, groups, attn_layer=0, layer=1)
```

```python
import functools
import math

import jax
import jax.numpy as jnp
import numpy as np
from jax import lax
from jax.experimental import pallas as pl
from jax.experimental.pallas import tpu as pltpu

EPS = 1e-6
POOL_WINDOWS = (2, 4, 8, 16)
HEAD_DIM = 64
ATTN_PATTERNS = ((128, 1), (512, 4), (2048, 16))
ROPE_THETA = 10000.0
NEG_INF = -1e30

LANES = 128
VMEM_LIMIT_BYTES = 58 * 1024 * 1024
TOKEN_TILE = 512
QKV_TILE = 1024
ATTN_Q_TILE = 1024
ATTN_LOOKAHEAD = 2
ATTN_PER_DOT = 7
N_FUSED_ATTN = 2
MXU_COLS = 256
FFN_CHUNK = 256
MIXER_GAP = 3
DEINT_STRIDE = 4

F32 = jnp.float32
BF16 = jnp.bfloat16


def _head_groups(n_heads):
    n = len(ATTN_PATTERNS)
    return tuple(n_heads // n + (1 if g < n_heads % n else 0) for g in range(n))


def _rmsnorm(x, g):
    ms = jnp.mean(x * x, axis=-1, keepdims=True)
    return x * lax.rsqrt(ms + EPS) * g


def _params(*sem):
    return pltpu.CompilerParams(dimension_semantics=sem, vmem_limit_bytes=VMEM_LIMIT_BYTES)


def _const_spec(shape):
    zeros = (0,) * len(shape)
    return pl.BlockSpec(shape, lambda *_: zeros, pipeline_mode=pl.Buffered(1))


def _layer_spec(stacked, layer):
    idx = (layer,) + (0,) * (stacked.ndim - 1)
    return pl.BlockSpec((None,) + stacked.shape[1:], lambda *_: idx, pipeline_mode=pl.Buffered(1))


def _swiglu_steps(x1_ref, o_ref, g_ref, wg_ref, wu_ref, wd_ref, finish):
    x1 = x1_ref[...]
    h = _rmsnorm(x1, g_ref[...]).astype(BF16)
    acc = x1
    n_chunks = wg_ref.shape[-1] // FFN_CHUNK
    for c in range(n_chunks):
        sl = slice(c * FFN_CHUNK, (c + 1) * FFN_CHUNK)
        gate = jnp.dot(h, wg_ref[:, sl].astype(BF16), preferred_element_type=F32)
        up = jnp.dot(h, wu_ref[:, sl].astype(BF16), preferred_element_type=F32)
        a = (gate * jax.nn.sigmoid(gate) * up).astype(BF16)
        acc = acc + jnp.dot(a, wd_ref[sl, :].astype(BF16), preferred_element_type=F32)
        if c == n_chunks - 1:
            o_ref[...] = finish(acc)
        yield


def _skewed_step(t, n_tiles, make_mixer, make_ffn):
    @pl.when(t == 0)
    def _():
        for _ in make_mixer():
            pass

    @pl.when(t == n_tiles)
    def _():
        for _ in make_ffn():
            pass

    @pl.when(jnp.logical_and(t > 0, t < n_tiles))
    def _():
        mixer = make_mixer()
        for c, _ in enumerate(make_ffn()):
            if c % MIXER_GAP == MIXER_GAP - 1:
                next(mixer, None)
        for _ in mixer:
            pass


def _ffn_specs(d, w_gate, w_up, w_down, layer):
    return [_const_spec((1, d)), _layer_spec(w_gate, layer), _layer_spec(w_up, layer), _layer_spec(w_down, layer)]


def _tile_index(t, n_tiles, per_batch):
    tt = jnp.minimum(t, n_tiles - 1)
    return tt // per_batch, tt % per_batch


def _pool_steps(i, x_ref, g_ref, win_ref, wgrp_ref, scale_ref, wout_ref, ext_ref, x1_ref, *, ts, halo):
    d_model = x_ref.shape[-1]
    gdim = d_model // len(POOL_WINDOWS)
    h = _rmsnorm(x_ref[...], g_ref[...]).astype(BF16)
    ext_ref[halo:halo + ts, :] = jnp.dot(h, win_ref[...], preferred_element_type=F32)
    yield
    pos = i * ts + lax.broadcasted_iota(jnp.int32, (ts, 1), 0)
    zs = []
    for g, w in enumerate(POOL_WINDOWS):
        cols = slice(g * gdim, (g + 1) * gdim)
        s = ext_ref[:, cols]
        k = 1
        while k < w:
            s = s + pltpu.roll(s, k, axis=0)
            k *= 2
        inv_cnt = 1.0 / jnp.minimum(pos + 1, w).astype(F32)
        p = s[halo:, :] * inv_cnt - ext_ref[halo:halo + ts, cols]
        zs.append(jnp.dot(p.astype(BF16), wgrp_ref[g], preferred_element_type=F32))
    yield
    z = jnp.concatenate(zs, axis=1) * scale_ref[...]
    y = jnp.dot(z.astype(BF16), wout_ref[...], preferred_element_type=F32)
    ext_ref[0:halo, :] = ext_ref[ts:ts + halo, :]
    x1_ref[...] = x_ref[...] + y
    yield


def _pool_ffn_kernel(x_ref, g_ref, win_ref, wgrp_ref, scale_ref, wout_ref,
                     gf_ref, wg_ref, wu_ref, wd_ref, o_ref, ext_ref, x1_ref, *, ts, halo, n_tiles, per_batch):
    t = pl.program_id(0)
    i = jnp.minimum(t, n_tiles - 1) % per_batch

    @pl.when(i == 0)
    def _():
        ext_ref[0:halo, :] = jnp.zeros((halo, x_ref.shape[-1]), F32)

    make_mixer = functools.partial(_pool_steps, i, x_ref, g_ref, win_ref, wgrp_ref, scale_ref, wout_ref,
                                   ext_ref, x1_ref, ts=ts, halo=halo)
    make_ffn = functools.partial(_swiglu_steps, x1_ref, o_ref, gf_ref, wg_ref, wu_ref, wd_ref, lambda acc: acc)
    _skewed_step(t, n_tiles, make_mixer, make_ffn)


def _pool_ffn_layer(x, g_mix, w_in, w_group, scale, w_out, g_ffn, w_gate, w_up, w_down, *, layer, ts=TOKEN_TILE):
    b, s, d = x.shape
    halo = max(POOL_WINDOWS)
    per_batch = s // ts
    n_tiles = b * per_batch
    kern = functools.partial(_pool_ffn_kernel, ts=ts, halo=halo, n_tiles=n_tiles, per_batch=per_batch)
    mixer_tile = pl.BlockSpec((None, ts, d), lambda t: (*_tile_index(t, n_tiles, per_batch), 0))
    ffn_tile = pl.BlockSpec((None, ts, d), lambda t: (*_tile_index(jnp.maximum(t - 1, 0), n_tiles, per_batch), 0))
    return pl.pallas_call(
        kern,
        grid=(n_tiles + 1,),
        in_specs=[mixer_tile, _const_spec((1, d)), _layer_spec(w_in, layer), _layer_spec(w_group, layer),
                  _const_spec((1, d)), _layer_spec(w_out, layer)] + _ffn_specs(d, w_gate, w_up, w_down, layer),
        out_specs=ffn_tile,
        out_shape=jax.ShapeDtypeStruct(x.shape, F32),
        scratch_shapes=[pltpu.VMEM((ts + halo, d), F32), pltpu.VMEM((ts, d), F32)],
        compiler_params=_params("arbitrary"),
        name="pool_ffn",
    )(x, g_mix.reshape(1, d), w_in, w_group, scale.reshape(1, d), w_out, g_ffn.reshape(1, d), w_gate, w_up, w_down)


def _class_order(d):
    order, cur = (0,), 1
    while cur < d:
        order = tuple(c + cur * r for c in order for r in range(DEINT_STRIDE))
        cur *= DEINT_STRIDE
    assert cur == d, "dilations must be powers of DEINT_STRIDE"
    return order


def _split_rows(src, n_blocks, blk):
    n = blk // DEINT_STRIDE
    return jnp.concatenate([src[pl.ds(b0 * blk + r, n, stride=DEINT_STRIDE), :]
                            for b0 in range(n_blocks) for r in range(DEINT_STRIDE)], axis=0)


def _qkv_steps(x_ref, g_ref, cos_ref, sin_ref, w_refs, store, hs_ref, tab_ref, *, ts, groups):
    lane = lax.broadcasted_iota(jnp.int32, (1, LANES), 1)
    first_half = (lane % HEAD_DIM) < (HEAD_DIM // 2)
    n_chunks = x_ref.shape[-1] // LANES
    yield "vpu"
    h = _rmsnorm(x_ref[...], g_ref[...])
    cs, sn = cos_ref[...], sin_ref[...]
    cur_d = 1
    for gi, (d, cg) in enumerate(groups):
        while cur_d < d:
            yield "vpu"
            for c in range(n_chunks):
                hs_ref[c] = h[:, c * LANES:(c + 1) * LANES]
            tab_ref[0], tab_ref[1] = cs, sn
            blk = ts // cur_d
            h = jnp.concatenate([_split_rows(hs_ref.at[c], cur_d, blk) for c in range(n_chunks)], axis=1)
            cs, sn = _split_rows(tab_ref.at[0], cur_d, blk), _split_rows(tab_ref.at[1], cur_d, blk)
            cur_d *= DEINT_STRIDE
        assert cur_d == d, "head groups must come in increasing powers of DEINT_STRIDE"
        hb = h.astype(BF16)
        pieces = []
        for c0 in range(0, 3 * cg, MXU_COLS):
            yield "dot"
            pieces.append(jnp.dot(hb, w_refs[gi][:, c0:min(c0 + MXU_COLS, 3 * cg)], preferred_element_type=F32))
        yield "vpu"
        qkv = jnp.concatenate(pieces, axis=1)
        chunks = []
        for c in range(2 * cg // LANES):
            t = qkv[:, c * LANES:(c + 1) * LANES]
            swapped = jnp.where(first_half, pltpu.roll(t, LANES - HEAD_DIM // 2, axis=1),
                                pltpu.roll(t, HEAD_DIM // 2, axis=1))
            chunks.append(t * cs + swapped * sn)
        qk = jnp.concatenate(chunks, axis=1)
        n = ts // d
        for part, rows in enumerate((qk[:, :cg], qk[:, cg:2 * cg], qkv[:, 2 * cg:])):
            pb = rows.astype(BF16)
            for slot, r in enumerate(_class_order(d)):
                store(gi, part, r, pb[slot * n:(slot + 1) * n, :])


def _band_masks(w, has_prev):
    qi = lax.broadcasted_iota(jnp.int32, (w, 2 * w), 0)
    kj = lax.broadcasted_iota(jnp.int32, (w, 2 * w), 1)
    band = jnp.logical_and(kj >= qi, kj <= qi + w)
    return band, jnp.logical_and(band, jnp.logical_or(kj >= w, has_prev))


def _attn_segment(masks, q_ref, k_ref, v_ref, kp_ref, vp_ref, o_ref, lse_ref, lse_scr, *, w, n_heads):
    band, band_first = masks
    tq = q_ref.shape[0]
    cg = n_heads * HEAD_DIM
    nsub = tq // w
    lane = lax.broadcasted_iota(jnp.int32, (1, LANES), 1)
    nt = (((1,), (1,)), ((), ()))

    units = []
    for c0 in range(0, cg, LANES):
        width = min(LANES, cg - c0)
        for hh in range(width // HEAD_DIM):
            if width == LANES:
                sel = (lane < HEAD_DIM) if hh == 0 else (lane >= HEAD_DIM)
            else:
                sel = None
            units.append((len(units), slice(c0, c0 + width), hh, sel))

    def key_rows(ref, prev_ref, cols, j):
        if j == 0:
            return jnp.concatenate([prev_ref[:, cols], ref[0:w, cols]], axis=0)
        return ref[(j - 1) * w:(j + 1) * w, cols]

    def scores(item):
        (_, cols, _, sel), j = item
        qb = q_ref[j * w:(j + 1) * w, cols]
        if sel is not None:
            qb = jnp.where(sel, qb, jnp.zeros_like(qb))
        return lax.dot_general(qb, key_rows(k_ref, kp_ref, cols, j), nt, preferred_element_type=F32)

    def finish(item, first, last, sc):
        (h, cols, hh, _), j = item
        if first:
            lse_scr[0:tq, :] = jnp.full((tq, LANES), NEG_INF, F32)
        lo = hh * HEAD_DIM
        rows = slice(j * w, (j + 1) * w)
        s = jnp.where(band_first if j == 0 else band, sc, NEG_INF)
        m = jnp.max(s, axis=-1, keepdims=True)
        e = jnp.exp(s - m)
        den = jnp.sum(e, axis=-1, keepdims=True)
        pv = jnp.dot(e.astype(BF16), key_rows(v_ref, vp_ref, cols, j), preferred_element_type=F32) / den
        o_ref[rows, cols.start + lo:cols.start + lo + HEAD_DIM] = pv[:, lo:lo + HEAD_DIM].astype(BF16)
        lse_scr[rows, :] = jnp.where(lane == h, m + jnp.log(den), lse_scr[rows, :])
        if last:
            lse = lse_scr[0:tq, :]
            mx = jnp.max(lse, axis=-1, keepdims=True)
            tot = jnp.sum(jnp.exp(lse - mx), axis=-1, keepdims=True)
            lse_ref[...] = jnp.broadcast_to(mx + jnp.log(tot) - math.log(n_heads), (tq, LANES))

    items = [(unit, j) for unit in units for j in range(nsub)]
    return [(functools.partial(scores, item), functools.partial(finish, item, n == 0, n == len(items) - 1))
            for n, item in enumerate(items)]


def _run_lookahead(work):
    pending = [scores() for scores, _ in work[:ATTN_LOOKAHEAD]]
    for n, (_, finish) in enumerate(work):
        if n + ATTN_LOOKAHEAD < len(work):
            pending.append(work[n + ATTN_LOOKAHEAD][0]())
        finish(pending.pop(0))


def _attn_kernel(q_ref, k_ref, v_ref, kp_ref, vp_ref, o_ref, lse_ref, lse_scr, *, w, n_heads):
    masks = _band_masks(w, pl.program_id(1) > 0)
    _run_lookahead(_attn_segment(masks, q_ref, k_ref, v_ref, kp_ref, vp_ref, o_ref, lse_ref, lse_scr,
                                 w=w, n_heads=n_heads))


def _attn_group(q, k, v, *, w, n_heads, tq=ATTN_Q_TILE):
    nseq, l, cg = q.shape
    tq = min(tq, l)
    ratio = tq // w
    kern = functools.partial(_attn_kernel, w=w, n_heads=n_heads)
    cur = pl.BlockSpec((None, tq, cg), lambda n, i: (n, i, 0))
    prev = pl.BlockSpec((None, w, cg), lambda n, i: (n, jnp.maximum(i * ratio - 1, 0), 0))
    return pl.pallas_call(
        kern,
        grid=(nseq, l // tq),
        in_specs=[cur, cur, cur, prev, prev],
        out_specs=[cur, pl.BlockSpec((None, tq, LANES), lambda n, i: (n, i, 0))],
        out_shape=[jax.ShapeDtypeStruct((nseq, l, cg), BF16),
                   jax.ShapeDtypeStruct((nseq, l, LANES), F32)],
        scratch_shapes=[pltpu.VMEM((tq, LANES), F32)],
        compiler_params=_params("arbitrary", "arbitrary"),
        name=f"attn_h{n_heads}_l{l}",
    )(q, k, v, k, v)


def _qkv_attn_kernel(x_ref, g_ref, cos_ref, sin_ref, *refs, ts, w, groups, n_fused, n_tiles, per_batch):
    ng = len(groups)
    n_sep = ng - n_fused
    w_refs = refs[:ng]
    sep_out = refs[ng:ng + 3 * n_sep]
    fused_out = refs[ng + 3 * n_sep:ng + 3 * n_sep + 2 * n_fused]
    hs_ref, tab_ref, lse_scr = refs[ng + 3 * n_sep + 2 * n_fused:][:3]
    qkv_scrs = refs[ng + 3 * n_sep + 2 * n_fused + 3:][:n_fused]
    prev_scrs = refs[ng + 3 * n_sep + 2 * n_fused + 3 + n_fused:]
    t = pl.program_id(0)
    cur, ready = t % 2, (t + 1) % 2

    @pl.when(t == 0)
    def _():
        for prev in prev_scrs:
            prev[...] = jnp.zeros(prev.shape, BF16)

    def store(gi, part, r, rows):
        if gi < n_fused:
            qkv_scrs[gi][cur, part, r] = rows
        else:
            sep_out[3 * (gi - n_fused) + part][r] = rows

    make_proj = functools.partial(_qkv_steps, x_ref, g_ref, cos_ref, sin_ref, w_refs, store, hs_ref, tab_ref,
                                  ts=ts, groups=groups)

    def attn_work():
        masks = _band_masks(w, (jnp.maximum(t - 1, 0) % per_batch) > 0)
        work = []
        for gi in range(n_fused):
            d, cg = groups[gi]
            scr, prev = qkv_scrs[gi], prev_scrs[gi]
            for r in range(d):
                work += _attn_segment(masks, scr.at[ready, 0, r], scr.at[ready, 1, r], scr.at[ready, 2, r],
                                      prev.at[0, r], prev.at[1, r], fused_out[2 * gi].at[r],
                                      fused_out[2 * gi + 1].at[r], lse_scr, w=w, n_heads=cg // HEAD_DIM)
        return work

    def keep_tails():
        for gi in range(n_fused):
            d, _ = groups[gi]
            rows = ts // d
            for r in range(d):
                for part in (1, 2):
                    prev_scrs[gi][part - 1, r] = qkv_scrs[gi][ready, part, r, rows - w:rows, :]

    @pl.when(t == 0)
    def _():
        for _ in make_proj():
            pass

    @pl.when(t == n_tiles)
    def _():
        _run_lookahead(attn_work())

    @pl.when(jnp.logical_and(t > 0, t < n_tiles))
    def _():
        work = attn_work()
        issued = []
        for kind in make_proj():
            for finish, sc in issued:
                finish(sc)
            issued = []
            if kind == "dot":
                issued = [(finish, scores()) for scores, finish in work[:ATTN_PER_DOT]]
                work = work[ATTN_PER_DOT:]
        for finish, sc in issued:
            finish(sc)
        _run_lookahead(work)
        keep_tails()


def _qkv_attn_layer(x, g, cos, sin, w_groups, groups, *, n_fused, w, ts=QKV_TILE):
    b, s, d_model = x.shape
    per_batch = s // ts
    n_tiles = b * per_batch
    kern = functools.partial(_qkv_attn_kernel, ts=ts, w=w, groups=groups, n_fused=n_fused,
                             n_tiles=n_tiles, per_batch=per_batch)

    def tile_spec(block, lag):
        lead = len(block) - 2

        def index(t):
            bi, i = _tile_index(jnp.maximum(t - lag, 0), n_tiles, per_batch)
            return (bi,) + (0,) * (lead - 1) + (i, 0)
        return pl.BlockSpec(block, index)

    table_spec = pl.BlockSpec((ts, LANES), lambda t: (_tile_index(t, n_tiles, per_batch)[1], 0))
    out_shapes, out_specs, scratch_qkv, scratch_prev = [], [], [], []
    for d, cg in groups[n_fused:]:
        for _ in range(3):
            out_shapes.append(jax.ShapeDtypeStruct((b, d, s // d, cg), BF16))
            out_specs.append(tile_spec((None, d, ts // d, cg), 0))
    for d, cg in groups[:n_fused]:
        out_shapes += [jax.ShapeDtypeStruct((b, d, s // d, cg), BF16),
                       jax.ShapeDtypeStruct((b, d, s // d, LANES), F32)]
        out_specs += [tile_spec((None, d, ts // d, cg), 1), tile_spec((None, d, ts // d, LANES), 1)]
        scratch_qkv.append(pltpu.VMEM((2, 3, d, ts // d, cg), BF16))
        scratch_prev.append(pltpu.VMEM((2, d, w, cg), BF16))
    return pl.pallas_call(
        kern,
        grid=(n_tiles + 1,),
        in_specs=[tile_spec((None, ts, d_model), 0), _const_spec((1, d_model)), table_spec, table_spec]
        + [_const_spec(wg.shape) for wg in w_groups],
        out_specs=out_specs,
        out_shape=out_shapes,
        scratch_shapes=[pltpu.VMEM((d_model // LANES, ts, LANES), F32), pltpu.VMEM((2, ts, LANES), F32),
                        pltpu.VMEM((ts, LANES), F32)] + scratch_qkv + scratch_prev,
        compiler_params=_params("arbitrary"),
        name="qkv_rope_attn",
    )(x, g.reshape(1, d_model), cos, sin, *w_groups)


def _merge_steps(x_ref, o_refs, l_refs, wout_ref, scratch, x1_ref, *, ts, groups):
    ng = len(groups)
    outs, lses = [], []
    si = 0
    for gi, (d, cg) in enumerate(groups):
        if d == 1:
            outs.append(o_refs[gi][0].astype(F32))
            lses.append(l_refs[gi][0])
            continue
        n = ts // d
        o_scr, l_scr = scratch[si], scratch[si + 1]
        si += 2
        n_chunks = o_scr.shape[0]
        for r in range(d):
            blk = o_refs[gi][r].astype(F32)
            for c in range(n_chunks):
                piece = blk[:, c * LANES:min((c + 1) * LANES, cg)]
                if piece.shape[1] < LANES:
                    piece = jnp.concatenate([piece, jnp.zeros((n, LANES - piece.shape[1]), F32)], axis=1)
                o_scr[c, pl.ds(r, n, stride=d), :] = piece
            l_scr[pl.ds(r, n, stride=d), :] = l_refs[gi][r]
        outs.append(jnp.concatenate([o_scr[c] for c in range(n_chunks)], axis=1)[:, :cg])
        lses.append(l_scr[...])
    m = functools.reduce(jnp.maximum, lses)
    es = [jnp.exp(l - m) for l in lses]
    inv = float(ng) / functools.reduce(lambda a, b2: a + b2, es)
    merged = [(o * (e * inv)[:, :1]).astype(BF16) for o, e in zip(outs, es)]
    yield
    y = x_ref[...]
    row = 0
    for (d, cg), mg in zip(groups, merged):
        y = y + jnp.dot(mg, wout_ref[row:row + cg, :], preferred_element_type=F32)
        row += cg
    x1_ref[...] = y
    yield


def _merge_ffn_kernel(x_ref, *refs, ts, groups, n_tiles):
    ng = len(groups)
    o_refs, l_refs = refs[:ng], refs[ng:2 * ng]
    wout_ref, gf_ref, wg_ref, wu_ref, wd_ref, gfin_ref, y_ref, x1_ref = refs[2 * ng:2 * ng + 8]
    scratch = refs[2 * ng + 8:]
    t = pl.program_id(0)
    make_mixer = functools.partial(_merge_steps, x_ref, o_refs, l_refs, wout_ref, scratch, x1_ref,
                                   ts=ts, groups=groups)
    make_ffn = functools.partial(_swiglu_steps, x1_ref, y_ref, gf_ref, wg_ref, wu_ref, wd_ref,
                                 lambda acc: _rmsnorm(acc, gfin_ref[...]))
    _skewed_step(t, n_tiles, make_mixer, make_ffn)


def _merge_ffn_layer(x, outs, lses, w_out, g_ffn, w_gate, w_up, w_down, g_final, groups, *,
                     attn_layer, layer, ts=TOKEN_TILE):
    b, s, d_model = x.shape
    per_batch = s // ts
    n_tiles = b * per_batch
    kern = functools.partial(_merge_ffn_kernel, ts=ts, groups=groups, n_tiles=n_tiles)

    def mixer_spec(block):
        lead = len(block) - 2
        return pl.BlockSpec(block, lambda t: (_tile_index(t, n_tiles, per_batch)[0],) + (0,) * (lead - 1)
                            + (_tile_index(t, n_tiles, per_batch)[1], 0))

    ffn_tile = pl.BlockSpec((None, ts, d_model),
                            lambda t: (*_tile_index(jnp.maximum(t - 1, 0), n_tiles, per_batch), 0))
    o_specs = [mixer_spec((None, d, ts // d, cg)) for d, cg in groups]
    l_specs = [mixer_spec((None, d, ts // d, LANES)) for d, cg in groups]
    scratch = [pltpu.VMEM((ts, d_model), F32)]
    for d, cg in groups:
        if d > 1:
            scratch += [pltpu.VMEM((pl.cdiv(cg, LANES), ts, LANES), F32), pltpu.VMEM((ts, LANES), F32)]
    return pl.pallas_call(
        kern,
        grid=(n_tiles + 1,),
        in_specs=[mixer_spec((None, ts, d_model))] + o_specs + l_specs + [_layer_spec(w_out, attn_layer)]
        + _ffn_specs(d_model, w_gate, w_up, w_down, layer) + [_const_spec((1, d_model))],
        out_specs=ffn_tile,
        out_shape=jax.ShapeDtypeStruct(x.shape, F32),
        scratch_shapes=scratch,
        compiler_params=_params("arbitrary"),
        name="attn_merge_ffn_final",
    )(x, *outs, *lses, w_out, g_ffn.reshape(1, d_model), w_gate, w_up, w_down, g_final.reshape(1, d_model))


def _rope_tables(s):
    half = HEAD_DIM // 2
    inv_freq = 1.0 / (ROPE_THETA ** (np.arange(0, HEAD_DIM, 2, dtype=np.float64) / HEAD_DIM))
    ang = np.arange(s, dtype=np.float64)[:, None] * inv_freq[None, :]
    cos, sin = np.cos(ang), np.sin(ang)
    reps = LANES // half
    cos_t = np.concatenate([cos] * reps, axis=-1)
    sin_t = np.concatenate([-sin, sin] * (reps // 2), axis=-1)
    return jnp.asarray(cos_t, F32), jnp.asarray(sin_t, F32)


def kernel(x, norm_mix, norm_ffn, norm_final, pool_w_in, pool_w_group, pool_scale, pool_w_out,
           attn_w_qkv, attn_w_out, ffn_w_gate, ffn_w_up, ffn_w_down):
    b, s, d_model = x.shape
    assert norm_mix.shape[0] == 2, "two layers: pooling mixer then dilated attention"
    n_heads = attn_w_out.shape[1] // HEAD_DIM
    d_attn = n_heads * HEAD_DIM
    head_groups = _head_groups(n_heads)
    groups = tuple((dil, nh * HEAD_DIM) for (_, dil), nh in zip(ATTN_PATTERNS, head_groups))
    bf = lambda t: t.astype(BF16)

    x = _pool_ffn_layer(x, norm_mix[0], bf(pool_w_in), bf(pool_w_group), pool_scale[0], bf(pool_w_out),
                        norm_ffn[0], ffn_w_gate, ffn_w_up, ffn_w_down, layer=0)

    w_qkv = attn_w_qkv[0]
    part_scale = (HEAD_DIM ** -0.5, 1.0, 1.0)
    w_groups, col = [], 0
    for _, cg in groups:
        w_groups.append(bf(jnp.concatenate(
            [w_qkv[:, p * d_attn + col:p * d_attn + col + cg] * part_scale[p] for p in range(3)], axis=1)))
        col += cg
    cos_t, sin_t = _rope_tables(s)
    window = ATTN_PATTERNS[0][0] // ATTN_PATTERNS[0][1]
    assert all(win // dil == window for win, dil in ATTN_PATTERNS), "one window length in class rows"
    res = _qkv_attn_layer(x, norm_mix[1], cos_t, sin_t, w_groups, groups, n_fused=N_FUSED_ATTN, w=window)
    n_sep = len(groups) - N_FUSED_ATTN
    outs = [res[3 * n_sep + 2 * gi] for gi in range(N_FUSED_ATTN)]
    lses = [res[3 * n_sep + 2 * gi + 1] for gi in range(N_FUSED_ATTN)]
    for gi in range(n_sep):
        dil, cg = groups[N_FUSED_ATTN + gi]
        q, k, v = (t.reshape(b * dil, s // dil, cg) for t in res[3 * gi:3 * gi + 3])
        o, lse = _attn_group(q, k, v, w=window, n_heads=cg // HEAD_DIM)
        outs.append(o.reshape(b, dil, s // dil, cg))
        lses.append(lse.reshape(b, dil, s // dil, LANES))
    return _merge_ffn_layer(x, outs, lses, bf(attn_w_out), norm_ffn[1], ffn_w_gate, ffn_w_up, ffn_w_down,
                            norm_final, groups, attn_layer=0, layer=1)
```

```python
import functools
import math

import jax
import jax.numpy as jnp
import numpy as np
from jax import lax
from jax.experimental import pallas as pl
from jax.experimental.pallas import tpu as pltpu

EPS = 1e-6
POOL_WINDOWS = (2, 4, 8, 16)
HEAD_DIM = 64
ATTN_PATTERNS = ((128, 1), (512, 4), (2048, 16))
ROPE_THETA = 10000.0
NEG_INF = -1e30

LANES = 128
VMEM_LIMIT_BYTES = 58 * 1024 * 1024
TOKEN_TILE = 512
QKV_TILE = 1024
ATTN_Q_TILE = 1024
ATTN_LOOKAHEAD = 2
ATTN_PER_DOT = 7
N_FUSED_ATTN = 2
MXU_COLS = 256
FFN_CHUNK = 256
MIXER_GAP = 3
DEINT_STRIDE = 4

F32 = jnp.float32
BF16 = jnp.bfloat16


def _head_groups(n_heads):
    n = len(ATTN_PATTERNS)
    return tuple(n_heads // n + (1 if g < n_heads % n else 0) for g in range(n))


def _rmsnorm(x, g):
    ms = jnp.mean(x * x, axis=-1, keepdims=True)
    return x * lax.rsqrt(ms + EPS) * g


def _params(*sem):
    return pltpu.CompilerParams(dimension_semantics=sem, vmem_limit_bytes=VMEM_LIMIT_BYTES)


def _const_spec(shape):
    zeros = (0,) * len(shape)
    return pl.BlockSpec(shape, lambda *_: zeros, pipeline_mode=pl.Buffered(1))


def _layer_spec(stacked, layer):
    idx = (layer,) + (0,) * (stacked.ndim - 1)
    return pl.BlockSpec((None,) + stacked.shape[1:], lambda *_: idx, pipeline_mode=pl.Buffered(1))


def _swiglu_steps(x1_ref, o_ref, g_ref, wg_ref, wu_ref, wd_ref, finish):
    x1 = x1_ref[...]
    h = _rmsnorm(x1, g_ref[...]).astype(BF16)
    acc = x1
    n_chunks = wg_ref.shape[-1] // FFN_CHUNK
    for c in range(n_chunks):
        sl = slice(c * FFN_CHUNK, (c + 1) * FFN_CHUNK)
        gate = jnp.dot(h, wg_ref[:, sl].astype(BF16), preferred_element_type=F32)
        up = jnp.dot(h, wu_ref[:, sl].astype(BF16), preferred_element_type=F32)
        a = (gate * jax.nn.sigmoid(gate) * up).astype(BF16)
        acc = acc + jnp.dot(a, wd_ref[sl, :].astype(BF16), preferred_element_type=F32)
        if c == n_chunks - 1:
            o_ref[...] = finish(acc)
        yield


def _skewed_step(t, n_tiles, make_mixer, make_ffn):
    @pl.when(t == 0)
    def _():
        for _ in make_mixer():
            pass

    @pl.when(t == n_tiles)
    def _():
        for _ in make_ffn():
            pass

    @pl.when(jnp.logical_and(t > 0, t < n_tiles))
    def _():
        mixer = make_mixer()
        for c, _ in enumerate(make_ffn()):
            if c % MIXER_GAP == MIXER_GAP - 1:
                next(mixer, None)
        for _ in mixer:
            pass


def _ffn_specs(d, w_gate, w_up, w_down, layer):
    return [_const_spec((1, d)), _layer_spec(w_gate, layer), _layer_spec(w_up, layer), _layer_spec(w_down, layer)]


def _tile_index(t, n_tiles, per_batch):
    tt = jnp.minimum(t, n_tiles - 1)
    return tt // per_batch, tt % per_batch


def _pool_steps(i, x_ref, g_ref, win_ref, wgrp_ref, scale_ref, wout_ref, ext_ref, x1_ref, *, ts, halo):
    d_model = x_ref.shape[-1]
    gdim = d_model // len(POOL_WINDOWS)
    h = _rmsnorm(x_ref[...], g_ref[...]).astype(BF16)
    ext_ref[halo:halo + ts, :] = jnp.dot(h, win_ref[...], preferred_element_type=F32)
    yield
    pos = i * ts + lax.broadcasted_iota(jnp.int32, (ts, 1), 0)
    zs = []
    for g, w in enumerate(POOL_WINDOWS):
        cols = slice(g * gdim, (g + 1) * gdim)
        s = ext_ref[:, cols]
        k = 1
        while k < w:
            s = s + pltpu.roll(s, k, axis=0)
            k *= 2
        inv_cnt = 1.0 / jnp.minimum(pos + 1, w).astype(F32)
        p = s[halo:, :] * inv_cnt - ext_ref[halo:halo + ts, cols]
        zs.append(jnp.dot(p.astype(BF16), wgrp_ref[g], preferred_element_type=F32))
    yield
    z = jnp.concatenate(zs, axis=1) * scale_ref[...]
    y = jnp.dot(z.astype(BF16), wout_ref[...], preferred_element_type=F32)
    ext_ref[0:halo, :] = ext_ref[ts:ts + halo, :]
    x1_ref[...] = x_ref[...] + y
    yield


def _pool_ffn_kernel(x_ref, g_ref, win_ref, wgrp_ref, scale_ref, wout_ref,
                     gf_ref, wg_ref, wu_ref, wd_ref, o_ref, ext_ref, x1_ref, *, ts, halo, n_tiles, per_batch):
    t = pl.program_id(0)
    i = jnp.minimum(t, n_tiles - 1) % per_batch

    @pl.when(i == 0)
    def _():
        ext_ref[0:halo, :] = jnp.zeros((halo, x_ref.shape[-1]), F32)

    make_mixer = functools.partial(_pool_steps, i, x_ref, g_ref, win_ref, wgrp_ref, scale_ref, wout_ref,
                                   ext_ref, x1_ref, ts=ts, halo=halo)
    make_ffn = functools.partial(_swiglu_steps, x1_ref, o_ref, gf_ref, wg_ref, wu_ref, wd_ref, lambda acc: acc)
    _skewed_step(t, n_tiles, make_mixer, make_ffn)


def _pool_ffn_layer(x, g_mix, w_in, w_group, scale, w_out, g_ffn, w_gate, w_up, w_down, *, layer, ts=TOKEN_TILE):
    b, s, d = x.shape
    halo = max(POOL_WINDOWS)
    per_batch = s // ts
    n_tiles = b * per_batch
    kern = functools.partial(_pool_ffn_kernel, ts=ts, halo=halo, n_tiles=n_tiles, per_batch=per_batch)
    mixer_tile = pl.BlockSpec((None, ts, d), lambda t: (*_tile_index(t, n_tiles, per_batch), 0))
    ffn_tile = pl.BlockSpec((None, ts, d), lambda t: (*_tile_index(jnp.maximum(t - 1, 0), n_tiles, per_batch), 0))
    return pl.pallas_call(
        kern,
        grid=(n_tiles + 1,),
        in_specs=[mixer_tile, _const_spec((1, d)), _layer_spec(w_in, layer), _layer_spec(w_group, layer),
                  _const_spec((1, d)), _layer_spec(w_out, layer)] + _ffn_specs(d, w_gate, w_up, w_down, layer),
        out_specs=ffn_tile,
        out_shape=jax.ShapeDtypeStruct(x.shape, F32),
        scratch_shapes=[pltpu.VMEM((ts + halo, d), F32), pltpu.VMEM((ts, d), F32)],
        compiler_params=_params("arbitrary"),
        name="pool_ffn",
    )(x, g_mix.reshape(1, d), w_in, w_group, scale.reshape(1, d), w_out, g_ffn.reshape(1, d), w_gate, w_up, w_down)


def _class_order(d):
    order, cur = (0,), 1
    while cur < d:
        order = tuple(c + cur * r for c in order for r in range(DEINT_STRIDE))
        cur *= DEINT_STRIDE
    assert cur == d, "dilations must be powers of DEINT_STRIDE"
    return order


def _split_rows(src, n_blocks, blk):
    n = blk // DEINT_STRIDE
    return jnp.concatenate([src[pl.ds(b0 * blk + r, n, stride=DEINT_STRIDE), :]
                            for b0 in range(n_blocks) for r in range(DEINT_STRIDE)], axis=0)


def _qkv_steps(x_ref, g_ref, cos_ref, sin_ref, w_refs, store, hs_ref, tab_ref, *, ts, groups):
    lane = lax.broadcasted_iota(jnp.int32, (1, LANES), 1)
    first_half = (lane % HEAD_DIM) < (HEAD_DIM // 2)
    n_chunks = x_ref.shape[-1] // LANES
    yield "vpu"
    h = _rmsnorm(x_ref[...], g_ref[...])
    cs, sn = cos_ref[...], sin_ref[...]
    cur_d = 1
    for gi, (d, cg) in enumerate(groups):
        while cur_d < d:
            yield "vpu"
            for c in range(n_chunks):
                hs_ref[c] = h[:, c * LANES:(c + 1) * LANES]
            tab_ref[0], tab_ref[1] = cs, sn
            blk = ts // cur_d
            h = jnp.concatenate([_split_rows(hs_ref.at[c], cur_d, blk) for c in range(n_chunks)], axis=1)
            cs, sn = _split_rows(tab_ref.at[0], cur_d, blk), _split_rows(tab_ref.at[1], cur_d, blk)
            cur_d *= DEINT_STRIDE
        assert cur_d == d, "head groups must come in increasing powers of DEINT_STRIDE"
        hb = h.astype(BF16)
        pieces = []
        for c0 in range(0, 3 * cg, MXU_COLS):
            yield "dot"
            pieces.append(jnp.dot(hb, w_refs[gi][:, c0:min(c0 + MXU_COLS, 3 * cg)], preferred_element_type=F32))
        yield "vpu"
        qkv = jnp.concatenate(pieces, axis=1)
        chunks = []
        for c in range(2 * cg // LANES):
            t = qkv[:, c * LANES:(c + 1) * LANES]
            swapped = jnp.where(first_half, pltpu.roll(t, LANES - HEAD_DIM // 2, axis=1),
                                pltpu.roll(t, HEAD_DIM // 2, axis=1))
            chunks.append(t * cs + swapped * sn)
        qk = jnp.concatenate(chunks, axis=1)
        n = ts // d
        for part, rows in enumerate((qk[:, :cg], qk[:, cg:2 * cg], qkv[:, 2 * cg:])):
            pb = rows.astype(BF16)
            for slot, r in enumerate(_class_order(d)):
                store(gi, part, r, pb[slot * n:(slot + 1) * n, :])


def _band_masks(w, has_prev):
    qi = lax.broadcasted_iota(jnp.int32, (w, 2 * w), 0)
    kj = lax.broadcasted_iota(jnp.int32, (w, 2 * w), 1)
    band = jnp.logical_and(kj >= qi, kj <= qi + w)
    return band, jnp.logical_and(band, jnp.logical_or(kj >= w, has_prev))


def _attn_segment(masks, q_ref, k_ref, v_ref, kp_ref, vp_ref, o_ref, lse_ref, lse_scr, *, w, n_heads):
    band, band_first = masks
    tq = q_ref.shape[0]
    cg = n_heads * HEAD_DIM
    nsub = tq // w
    lane = lax.broadcasted_iota(jnp.int32, (1, LANES), 1)
    nt = (((1,), (1,)), ((), ()))

    units = []
    for c0 in range(0, cg, LANES):
        width = min(LANES, cg - c0)
        for hh in range(width // HEAD_DIM):
            if width == LANES:
                sel = (lane < HEAD_DIM) if hh == 0 else (lane >= HEAD_DIM)
            else:
                sel = None
            units.append((len(units), slice(c0, c0 + width), hh, sel))

    def key_rows(ref, prev_ref, cols, j):
        if j == 0:
            return jnp.concatenate([prev_ref[:, cols], ref[0:w, cols]], axis=0)
        return ref[(j - 1) * w:(j + 1) * w, cols]

    def scores(item):
        (_, cols, _, sel), j = item
        qb = q_ref[j * w:(j + 1) * w, cols]
        if sel is not None:
            qb = jnp.where(sel, qb, jnp.zeros_like(qb))
        return lax.dot_general(qb, key_rows(k_ref, kp_ref, cols, j), nt, preferred_element_type=F32)

    def finish(item, first, last, sc):
        (h, cols, hh, _), j = item
        if first:
            lse_scr[0:tq, :] = jnp.full((tq, LANES), NEG_INF, F32)
        lo = hh * HEAD_DIM
        rows = slice(j * w, (j + 1) * w)
        s = jnp.where(band_first if j == 0 else band, sc, NEG_INF)
        m = jnp.max(s, axis=-1, keepdims=True)
        e = jnp.exp(s - m)
        den = jnp.sum(e, axis=-1, keepdims=True)
        pv = jnp.dot(e.astype(BF16), key_rows(v_ref, vp_ref, cols, j), preferred_element_type=F32) / den
        o_ref[rows, cols.start + lo:cols.start + lo + HEAD_DIM] = pv[:, lo:lo + HEAD_DIM].astype(BF16)
        lse_scr[rows, :] = jnp.where(lane == h, m + jnp.log(den), lse_scr[rows, :])
        if last:
            lse = lse_scr[0:tq, :]
            mx = jnp.max(lse, axis=-1, keepdims=True)
            tot = jnp.sum(jnp.exp(lse - mx), axis=-1, keepdims=True)
            lse_ref[...] = jnp.broadcast_to(mx + jnp.log(tot) - math.log(n_heads), (tq, LANES))

    items = [(unit, j) for unit in units for j in range(nsub)]
    return [(functools.partial(scores, item), functools.partial(finish, item, n == 0, n == len(items) - 1))
            for n, item in enumerate(items)]


def _run_lookahead(work):
    pending = [scores() for scores, _ in work[:ATTN_LOOKAHEAD]]
    for n, (_, finish) in enumerate(work):
        if n + ATTN_LOOKAHEAD < len(work):
            pending.append(work[n + ATTN_LOOKAHEAD][0]())
        finish(pending.pop(0))


def _attn_kernel(q_ref, k_ref, v_ref, kp_ref, vp_ref, o_ref, lse_ref, lse_scr, *, w, n_heads):
    masks = _band_masks(w, pl.program_id(1) > 0)
    _run_lookahead(_attn_segment(masks, q_ref, k_ref, v_ref, kp_ref, vp_ref, o_ref, lse_ref, lse_scr,
                                 w=w, n_heads=n_heads))


def _attn_group(q, k, v, *, w, n_heads, tq=ATTN_Q_TILE):
    nseq, l, cg = q.shape
    tq = min(tq, l)
    ratio = tq // w
    kern = functools.partial(_attn_kernel, w=w, n_heads=n_heads)
    cur = pl.BlockSpec((None, tq, cg), lambda n, i: (n, i, 0))
    prev = pl.BlockSpec((None, w, cg), lambda n, i: (n, jnp.maximum(i * ratio - 1, 0), 0))
    return pl.pallas_call(
        kern,
        grid=(nseq, l // tq),
        in_specs=[cur, cur, cur, prev, prev],
        out_specs=[cur, pl.BlockSpec((None, tq, LANES), lambda n, i: (n, i, 0))],
        out_shape=[jax.ShapeDtypeStruct((nseq, l, cg), BF16),
                   jax.ShapeDtypeStruct((nseq, l, LANES), F32)],
        scratch_shapes=[pltpu.VMEM((tq, LANES), F32)],
        compiler_params=_params("arbitrary", "arbitrary"),
        name=f"attn_h{n_heads}_l{l}",
    )(q, k, v, k, v)


def _qkv_attn_kernel(x_ref, g_ref, cos_ref, sin_ref, *refs, ts, w, groups, n_fused, n_tiles, per_batch):
    ng = len(groups)
    n_sep = ng - n_fused
    w_refs = refs[:ng]
    sep_out = refs[ng:ng + 3 * n_sep]
    fused_out = refs[ng + 3 * n_sep:ng + 3 * n_sep + 2 * n_fused]
    hs_ref, tab_ref, lse_scr = refs[ng + 3 * n_sep + 2 * n_fused:][:3]
    qkv_scrs = refs[ng + 3 * n_sep + 2 * n_fused + 3:][:n_fused]
    prev_scrs = refs[ng + 3 * n_sep + 2 * n_fused + 3 + n_fused:]
    t = pl.program_id(0)
    cur, ready = t % 2, (t + 1) % 2

    @pl.when(t == 0)
    def _():
        for scr in tuple(prev_scrs) + tuple(qkv_scrs):
            scr[...] = jnp.zeros(scr.shape, BF16)

    def store(gi, part, r, rows):
        if gi < n_fused:
            qkv_scrs[gi][cur, part, r] = rows
        else:
            sep_out[3 * (gi - n_fused) + part][r] = rows

    make_proj = functools.partial(_qkv_steps, x_ref, g_ref, cos_ref, sin_ref, w_refs, store, hs_ref, tab_ref,
                                  ts=ts, groups=groups)

    def attn_work():
        masks = _band_masks(w, (jnp.maximum(t - 1, 0) % per_batch) > 0)
        work = []
        for gi in range(n_fused):
            d, cg = groups[gi]
            scr, prev = qkv_scrs[gi], prev_scrs[gi]
            for r in range(d):
                work += _attn_segment(masks, scr.at[ready, 0, r], scr.at[ready, 1, r], scr.at[ready, 2, r],
                                      prev.at[0, r], prev.at[1, r], fused_out[2 * gi].at[r],
                                      fused_out[2 * gi + 1].at[r], lse_scr, w=w, n_heads=cg // HEAD_DIM)
        return work

    def keep_tails():
        for gi in range(n_fused):
            d, _ = groups[gi]
            rows = ts // d
            for r in range(d):
                for part in (1, 2):
                    prev_scrs[gi][part - 1, r] = qkv_scrs[gi][ready, part, r, rows - w:rows, :]

    work = attn_work()
    issued = []
    for kind in make_proj():
        for finish, sc in issued:
            finish(sc)
        issued = []
        if kind == "dot":
            issued = [(finish, scores()) for scores, finish in work[:ATTN_PER_DOT]]
            work = work[ATTN_PER_DOT:]
    for finish, sc in issued:
        finish(sc)
    _run_lookahead(work)
    keep_tails()


def _qkv_attn_layer(x, g, cos, sin, w_groups, groups, *, n_fused, w, ts=QKV_TILE):
    b, s, d_model = x.shape
    per_batch = s // ts
    n_tiles = b * per_batch
    kern = functools.partial(_qkv_attn_kernel, ts=ts, w=w, groups=groups, n_fused=n_fused,
                             n_tiles=n_tiles, per_batch=per_batch)

    def tile_spec(block, lag):
        lead = len(block) - 2

        def index(t):
            bi, i = _tile_index(jnp.maximum(t - lag, 0), n_tiles, per_batch)
            return (bi,) + (0,) * (lead - 1) + (i, 0)
        return pl.BlockSpec(block, index)

    table_spec = pl.BlockSpec((ts, LANES), lambda t: (_tile_index(t, n_tiles, per_batch)[1], 0))
    out_shapes, out_specs, scratch_qkv, scratch_prev = [], [], [], []
    for d, cg in groups[n_fused:]:
        for _ in range(3):
            out_shapes.append(jax.ShapeDtypeStruct((b, d, s // d, cg), BF16))
            out_specs.append(tile_spec((None, d, ts // d, cg), 0))
    for d, cg in groups[:n_fused]:
        out_shapes += [jax.ShapeDtypeStruct((b, d, s // d, cg), BF16),
                       jax.ShapeDtypeStruct((b, d, s // d, LANES), F32)]
        out_specs += [tile_spec((None, d, ts // d, cg), 1), tile_spec((None, d, ts // d, LANES), 1)]
        scratch_qkv.append(pltpu.VMEM((2, 3, d, ts // d, cg), BF16))
        scratch_prev.append(pltpu.VMEM((2, d, w, cg), BF16))
    return pl.pallas_call(
        kern,
        grid=(n_tiles + 1,),
        in_specs=[tile_spec((None, ts, d_model), 0), _const_spec((1, d_model)), table_spec, table_spec]
        + [_const_spec(wg.shape) for wg in w_groups],
        out_specs=out_specs,
        out_shape=out_shapes,
        scratch_shapes=[pltpu.VMEM((d_model // LANES, ts, LANES), F32), pltpu.VMEM((2, ts, LANES), F32),
                        pltpu.VMEM((ts, LANES), F32)] + scratch_qkv + scratch_prev,
        compiler_params=_params("arbitrary"),
        name="qkv_rope_attn",
    )(x, g.reshape(1, d_model), cos, sin, *w_groups)


def _merge_steps(x_ref, o_refs, l_refs, wout_ref, scratch, x1_ref, *, ts, groups):
    ng = len(groups)
    outs, lses = [], []
    si = 0
    for gi, (d, cg) in enumerate(groups):
        if d == 1:
            outs.append(o_refs[gi][0].astype(F32))
            lses.append(l_refs[gi][0])
            continue
        n = ts // d
        o_scr, l_scr = scratch[si], scratch[si + 1]
        si += 2
        n_chunks = o_scr.shape[0]
        for r in range(d):
            blk = o_refs[gi][r].astype(F32)
            for c in range(n_chunks):
                piece = blk[:, c * LANES:min((c + 1) * LANES, cg)]
                if piece.shape[1] < LANES:
                    piece = jnp.concatenate([piece, jnp.zeros((n, LANES - piece.shape[1]), F32)], axis=1)
                o_scr[c, pl.ds(r, n, stride=d), :] = piece
            l_scr[pl.ds(r, n, stride=d), :] = l_refs[gi][r]
        outs.append(jnp.concatenate([o_scr[c] for c in range(n_chunks)], axis=1)[:, :cg])
        lses.append(l_scr[...])
    m = functools.reduce(jnp.maximum, lses)
    es = [jnp.exp(l - m) for l in lses]
    inv = float(ng) / functools.reduce(lambda a, b2: a + b2, es)
    merged = [(o * (e * inv)[:, :1]).astype(BF16) for o, e in zip(outs, es)]
    yield
    y = x_ref[...]
    row = 0
    for (d, cg), mg in zip(groups, merged):
        y = y + jnp.dot(mg, wout_ref[row:row + cg, :], preferred_element_type=F32)
        row += cg
    x1_ref[...] = y
    yield


def _merge_ffn_kernel(x_ref, *refs, ts, groups, n_tiles):
    ng = len(groups)
    o_refs, l_refs = refs[:ng], refs[ng:2 * ng]
    wout_ref, gf_ref, wg_ref, wu_ref, wd_ref, gfin_ref, y_ref, x1_ref = refs[2 * ng:2 * ng + 8]
    scratch = refs[2 * ng + 8:]
    t = pl.program_id(0)
    make_mixer = functools.partial(_merge_steps, x_ref, o_refs, l_refs, wout_ref, scratch, x1_ref,
                                   ts=ts, groups=groups)
    make_ffn = functools.partial(_swiglu_steps, x1_ref, y_ref, gf_ref, wg_ref, wu_ref, wd_ref,
                                 lambda acc: _rmsnorm(acc, gfin_ref[...]))
    _skewed_step(t, n_tiles, make_mixer, make_ffn)


def _merge_ffn_layer(x, outs, lses, w_out, g_ffn, w_gate, w_up, w_down, g_final, groups, *,
                     attn_layer, layer, ts=TOKEN_TILE):
    b, s, d_model = x.shape
    per_batch = s // ts
    n_tiles = b * per_batch
    kern = functools.partial(_merge_ffn_kernel, ts=ts, groups=groups, n_tiles=n_tiles)

    def mixer_spec(block):
        lead = len(block) - 2
        return pl.BlockSpec(block, lambda t: (_tile_index(t, n_tiles, per_batch)[0],) + (0,) * (lead - 1)
                            + (_tile_index(t, n_tiles, per_batch)[1], 0))

    ffn_tile = pl.BlockSpec((None, ts, d_model),
                            lambda t: (*_tile_index(jnp.maximum(t - 1, 0), n_tiles, per_batch), 0))
    o_specs = [mixer_spec((None, d, ts // d, cg)) for d, cg in groups]
    l_specs = [mixer_spec((None, d, ts // d, LANES)) for d, cg in groups]
    scratch = [pltpu.VMEM((ts, d_model), F32)]
    for d, cg in groups:
        if d > 1:
            scratch += [pltpu.VMEM((pl.cdiv(cg, LANES), ts, LANES), F32), pltpu.VMEM((ts, LANES), F32)]
    return pl.pallas_call(
        kern,
        grid=(n_tiles + 1,),
        in_specs=[mixer_spec((None, ts, d_model))] + o_specs + l_specs + [_layer_spec(w_out, attn_layer)]
        + _ffn_specs(d_model, w_gate, w_up, w_down, layer) + [_const_spec((1, d_model))],
        out_specs=ffn_tile,
        out_shape=jax.ShapeDtypeStruct(x.shape, F32),
        scratch_shapes=scratch,
        compiler_params=_params("arbitrary"),
        name="attn_merge_ffn_final",
    )(x, *outs, *lses, w_out, g_ffn.reshape(1, d_model), w_gate, w_up, w_down, g_final.reshape(1, d_model))


def _rope_tables(s):
    half = HEAD_DIM // 2
    inv_freq = 1.0 / (ROPE_THETA ** (np.arange(0, HEAD_DIM, 2, dtype=np.float64) / HEAD_DIM))
    ang = np.arange(s, dtype=np.float64)[:, None] * inv_freq[None, :]
    cos, sin = np.cos(ang), np.sin(ang)
    reps = LANES // half
    cos_t = np.concatenate([cos] * reps, axis=-1)
    sin_t = np.concatenate([-sin, sin] * (reps // 2), axis=-1)
    return jnp.asarray(cos_t, F32), jnp.asarray(sin_t, F32)


def kernel(x, norm_mix, norm_ffn, norm_final, pool_w_in, pool_w_group, pool_scale, pool_w_out,
           attn_w_qkv, attn_w_out, ffn_w_gate, ffn_w_up, ffn_w_down):
    b, s, d_model = x.shape
    assert norm_mix.shape[0] == 2, "two layers: pooling mixer then dilated attention"
    n_heads = attn_w_out.shape[1] // HEAD_DIM
    d_attn = n_heads * HEAD_DIM
    head_groups = _head_groups(n_heads)
    groups = tuple((dil, nh * HEAD_DIM) for (_, dil), nh in zip(ATTN_PATTERNS, head_groups))
    bf = lambda t: t.astype(BF16)

    x = _pool_ffn_layer(x, norm_mix[0], bf(pool_w_in), bf(pool_w_group), pool_scale[0], bf(pool_w_out),
                        norm_ffn[0], ffn_w_gate, ffn_w_up, ffn_w_down, layer=0)

    w_qkv = attn_w_qkv[0]
    part_scale = (HEAD_DIM ** -0.5, 1.0, 1.0)
    w_groups, col = [], 0
    for _, cg in groups:
        w_groups.append(bf(jnp.concatenate(
            [w_qkv[:, p * d_attn + col:p * d_attn + col + cg] * part_scale[p] for p in range(3)], axis=1)))
        col += cg
    cos_t, sin_t = _rope_tables(s)
    window = ATTN_PATTERNS[0][0] // ATTN_PATTERNS[0][1]
    assert all(win // dil == window for win, dil in ATTN_PATTERNS), "one window length in class rows"
    res = _qkv_attn_layer(x, norm_mix[1], cos_t, sin_t, w_groups, groups, n_fused=N_FUSED_ATTN, w=window)
    n_sep = len(groups) - N_FUSED_ATTN
    outs = [res[3 * n_sep + 2 * gi] for gi in range(N_FUSED_ATTN)]
    lses = [res[3 * n_sep + 2 * gi + 1] for gi in range(N_FUSED_ATTN)]
    for gi in range(n_sep):
        dil, cg = groups[N_FUSED_ATTN + gi]
        q, k, v = (t.reshape(b * dil, s // dil, cg) for t in res[3 * gi:3 * gi + 3])
        o, lse = _attn_group(q, k, v, w=window, n_heads=cg // HEAD_DIM)
        outs.append(o.reshape(b, dil, s // dil, cg))
        lses.append(lse.reshape(b, dil, s // dil, LANES))
    return _merge_ffn_layer(x, outs, lses, bf(attn_w_out), norm_ffn[1], ffn_w_gate, ffn_w_up, ffn_w_down,
                            norm_final, groups, attn_layer=0, layer=1)
```

```python
import functools
import math

import jax
import jax.numpy as jnp
import numpy as np
from jax import lax
from jax.experimental import pallas as pl
from jax.experimental.pallas import tpu as pltpu

EPS = 1e-6
POOL_WINDOWS = (2, 4, 8, 16)
HEAD_DIM = 64
ATTN_PATTERNS = ((128, 1), (512, 4), (2048, 16))
ROPE_THETA = 10000.0
NEG_INF = -1e30

LANES = 128
VMEM_LIMIT_BYTES = 58 * 1024 * 1024
TOKEN_TILE = 512
QKV_TILE = 1024
ATTN_STEP_ROWS = 1024
ATTN_LOOKAHEAD = 2
FFN_CHUNK = 256
MIXER_GAP = 3
DEINT_STRIDE = 4

F32 = jnp.float32
BF16 = jnp.bfloat16


def _head_groups(n_heads):
    n = len(ATTN_PATTERNS)
    return tuple(n_heads // n + (1 if g < n_heads % n else 0) for g in range(n))


def _rmsnorm(x, g):
    ms = jnp.mean(x * x, axis=-1, keepdims=True)
    return x * lax.rsqrt(ms + EPS) * g


def _params(*sem):
    return pltpu.CompilerParams(dimension_semantics=sem, vmem_limit_bytes=VMEM_LIMIT_BYTES)


def _const_spec(shape):
    zeros = (0,) * len(shape)
    return pl.BlockSpec(shape, lambda *_: zeros, pipeline_mode=pl.Buffered(1))


def _layer_spec(stacked, layer):
    idx = (layer,) + (0,) * (stacked.ndim - 1)
    return pl.BlockSpec((None,) + stacked.shape[1:], lambda *_: idx, pipeline_mode=pl.Buffered(1))


def _swiglu_steps(x1_ref, o_ref, g_ref, wg_ref, wu_ref, wd_ref, finish):
    x1 = x1_ref[...]
    h = _rmsnorm(x1, g_ref[...]).astype(BF16)
    acc = x1
    n_chunks = wg_ref.shape[-1] // FFN_CHUNK
    for c in range(n_chunks):
        sl = slice(c * FFN_CHUNK, (c + 1) * FFN_CHUNK)
        gate = jnp.dot(h, wg_ref[:, sl].astype(BF16), preferred_element_type=F32)
        up = jnp.dot(h, wu_ref[:, sl].astype(BF16), preferred_element_type=F32)
        a = (gate * jax.nn.sigmoid(gate) * up).astype(BF16)
        acc = acc + jnp.dot(a, wd_ref[sl, :].astype(BF16), preferred_element_type=F32)
        if c == n_chunks - 1:
            o_ref[...] = finish(acc)
        yield


def _skewed_step(t, n_tiles, make_mixer, make_ffn):
    @pl.when(t == 0)
    def _():
        for _ in make_mixer():
            pass

    @pl.when(t == n_tiles)
    def _():
        for _ in make_ffn():
            pass

    @pl.when(jnp.logical_and(t > 0, t < n_tiles))
    def _():
        mixer = make_mixer()
        for c, _ in enumerate(make_ffn()):
            if c % MIXER_GAP == MIXER_GAP - 1:
                next(mixer, None)
        for _ in mixer:
            pass


def _ffn_specs(d, w_gate, w_up, w_down, layer):
    return [_const_spec((1, d)), _layer_spec(w_gate, layer), _layer_spec(w_up, layer), _layer_spec(w_down, layer)]


def _tile_index(t, n_tiles, per_batch):
    tt = jnp.minimum(t, n_tiles - 1)
    return tt // per_batch, tt % per_batch


def _pool_steps(i, x_ref, g_ref, win_ref, wgrp_ref, scale_ref, wout_ref, ext_ref, x1_ref, *, ts, halo):
    d_model = x_ref.shape[-1]
    gdim = d_model // len(POOL_WINDOWS)
    h = _rmsnorm(x_ref[...], g_ref[...]).astype(BF16)
    ext_ref[halo:halo + ts, :] = jnp.dot(h, win_ref[...], preferred_element_type=F32)
    yield
    pos = i * ts + lax.broadcasted_iota(jnp.int32, (ts, 1), 0)
    zs = []
    for g, w in enumerate(POOL_WINDOWS):
        cols = slice(g * gdim, (g + 1) * gdim)
        s = ext_ref[:, cols]
        k = 1
        while k < w:
            s = s + pltpu.roll(s, k, axis=0)
            k *= 2
        inv_cnt = 1.0 / jnp.minimum(pos + 1, w).astype(F32)
        p = s[halo:, :] * inv_cnt - ext_ref[halo:halo + ts, cols]
        zs.append(jnp.dot(p.astype(BF16), wgrp_ref[g], preferred_element_type=F32))
    yield
    z = jnp.concatenate(zs, axis=1) * scale_ref[...]
    y = jnp.dot(z.astype(BF16), wout_ref[...], preferred_element_type=F32)
    ext_ref[0:halo, :] = ext_ref[ts:ts + halo, :]
    x1_ref[...] = x_ref[...] + y
    yield


def _pool_ffn_kernel(x_ref, g_ref, win_ref, wgrp_ref, scale_ref, wout_ref,
                     gf_ref, wg_ref, wu_ref, wd_ref, o_ref, ext_ref, x1_ref, *, ts, halo, n_tiles, per_batch):
    t = pl.program_id(0)
    i = jnp.minimum(t, n_tiles - 1) % per_batch

    @pl.when(i == 0)
    def _():
        ext_ref[0:halo, :] = jnp.zeros((halo, x_ref.shape[-1]), F32)

    make_mixer = functools.partial(_pool_steps, i, x_ref, g_ref, win_ref, wgrp_ref, scale_ref, wout_ref,
                                   ext_ref, x1_ref, ts=ts, halo=halo)
    make_ffn = functools.partial(_swiglu_steps, x1_ref, o_ref, gf_ref, wg_ref, wu_ref, wd_ref, lambda acc: acc)
    _skewed_step(t, n_tiles, make_mixer, make_ffn)


def _pool_ffn_layer(x, g_mix, w_in, w_group, scale, w_out, g_ffn, w_gate, w_up, w_down, *, layer, ts=TOKEN_TILE):
    b, s, d = x.shape
    halo = max(POOL_WINDOWS)
    per_batch = s // ts
    n_tiles = b * per_batch
    kern = functools.partial(_pool_ffn_kernel, ts=ts, halo=halo, n_tiles=n_tiles, per_batch=per_batch)
    mixer_tile = pl.BlockSpec((None, ts, d), lambda t: (*_tile_index(t, n_tiles, per_batch), 0))
    ffn_tile = pl.BlockSpec((None, ts, d), lambda t: (*_tile_index(jnp.maximum(t - 1, 0), n_tiles, per_batch), 0))
    return pl.pallas_call(
        kern,
        grid=(n_tiles + 1,),
        in_specs=[mixer_tile, _const_spec((1, d)), _layer_spec(w_in, layer), _layer_spec(w_group, layer),
                  _const_spec((1, d)), _layer_spec(w_out, layer)] + _ffn_specs(d, w_gate, w_up, w_down, layer),
        out_specs=ffn_tile,
        out_shape=jax.ShapeDtypeStruct(x.shape, F32),
        scratch_shapes=[pltpu.VMEM((ts + halo, d), F32), pltpu.VMEM((ts, d), F32)],
        compiler_params=_params("arbitrary"),
        name="pool_ffn",
    )(x, g_mix.reshape(1, d), w_in, w_group, scale.reshape(1, d), w_out, g_ffn.reshape(1, d), w_gate, w_up, w_down)


def _class_order(d):
    order, cur = (0,), 1
    while cur < d:
        order = tuple(c + cur * r for c in order for r in range(DEINT_STRIDE))
        cur *= DEINT_STRIDE
    assert cur == d, "dilations must be powers of DEINT_STRIDE"
    return order


def _split_rows(src, n_blocks, blk):
    n = blk // DEINT_STRIDE
    return jnp.concatenate([src[pl.ds(b0 * blk + r, n, stride=DEINT_STRIDE), :]
                            for b0 in range(n_blocks) for r in range(DEINT_STRIDE)], axis=0)


def _qkv_kernel(x_ref, g_ref, cos_ref, sin_ref, *refs, ts, groups):
    ng = len(groups)
    w_refs, out_refs = refs[:ng], refs[ng:-2]
    hs_ref, tab_ref = refs[-2:]
    lane = lax.broadcasted_iota(jnp.int32, (1, LANES), 1)
    first_half = (lane % HEAD_DIM) < (HEAD_DIM // 2)
    n_chunks = x_ref.shape[-1] // LANES
    h = _rmsnorm(x_ref[...], g_ref[...])
    cs, sn = cos_ref[...], sin_ref[...]
    cur_d = 1
    for gi, (d, cg) in enumerate(groups):
        while cur_d < d:
            for c in range(n_chunks):
                hs_ref[c] = h[:, c * LANES:(c + 1) * LANES]
            tab_ref[0], tab_ref[1] = cs, sn
            blk = ts // cur_d
            h = jnp.concatenate([_split_rows(hs_ref.at[c], cur_d, blk) for c in range(n_chunks)], axis=1)
            cs, sn = _split_rows(tab_ref.at[0], cur_d, blk), _split_rows(tab_ref.at[1], cur_d, blk)
            cur_d *= DEINT_STRIDE
        assert cur_d == d, "head groups must come in increasing powers of DEINT_STRIDE"
        qkv = jnp.dot(h.astype(BF16), w_refs[gi][...], preferred_element_type=F32)
        chunks = []
        for c in range(2 * cg // LANES):
            t = qkv[:, c * LANES:(c + 1) * LANES]
            swapped = jnp.where(first_half, pltpu.roll(t, LANES - HEAD_DIM // 2, axis=1),
                                pltpu.roll(t, HEAD_DIM // 2, axis=1))
            chunks.append(t * cs + swapped * sn)
        qk = jnp.concatenate(chunks, axis=1)
        parts = (qk[:, :cg], qk[:, cg:2 * cg], qkv[:, 2 * cg:])
        n = ts // d
        for part, o_ref in zip(parts, out_refs[3 * gi:3 * gi + 3]):
            pb = part.astype(BF16)
            for slot, r in enumerate(_class_order(d)):
                o_ref[r] = pb[slot * n:(slot + 1) * n, :]


def _qkv_layer(x, g, cos, sin, w_groups, groups, *, ts=QKV_TILE):
    b, s, d_model = x.shape
    kern = functools.partial(_qkv_kernel, ts=ts, groups=groups)
    out_shapes, out_specs = [], []
    for d, cg in groups:
        for _ in range(3):
            out_shapes.append(jax.ShapeDtypeStruct((b, d, s // d, cg), BF16))
            out_specs.append(pl.BlockSpec((None, d, ts // d, cg), lambda bi, i: (bi, 0, i, 0)))
    table_spec = pl.BlockSpec((ts, LANES), lambda bi, i: (i, 0))
    return pl.pallas_call(
        kern,
        grid=(b, s // ts),
        in_specs=[pl.BlockSpec((None, ts, d_model), lambda bi, i: (bi, i, 0)), _const_spec((1, d_model)),
                  table_spec, table_spec] + [_const_spec(w.shape) for w in w_groups],
        out_specs=out_specs,
        out_shape=out_shapes,
        scratch_shapes=[pltpu.VMEM((d_model // LANES, ts, LANES), F32), pltpu.VMEM((2, ts, LANES), F32)],
        compiler_params=_params("arbitrary", "arbitrary"),
        name="qkv_rope",
    )(x, g.reshape(1, d_model), cos, sin, *w_groups)


def _band_masks(w, has_prev):
    qi = lax.broadcasted_iota(jnp.int32, (w, 2 * w), 0)
    kj = lax.broadcasted_iota(jnp.int32, (w, 2 * w), 1)
    band = jnp.logical_and(kj >= qi, kj <= qi + w)
    return band, jnp.logical_and(band, jnp.logical_or(kj >= w, has_prev))


def _attn_segment(masks, q_ref, k_ref, v_ref, kp_ref, vp_ref, o_ref, lse_ref, lse_scr, *, w, n_heads):
    band, band_first = masks
    tq = q_ref.shape[0]
    cg = n_heads * HEAD_DIM
    nsub = tq // w
    lane = lax.broadcasted_iota(jnp.int32, (1, LANES), 1)
    nt = (((1,), (1,)), ((), ()))

    units = []
    for c0 in range(0, cg, LANES):
        width = min(LANES, cg - c0)
        for hh in range(width // HEAD_DIM):
            if width == LANES:
                sel = (lane < HEAD_DIM) if hh == 0 else (lane >= HEAD_DIM)
            else:
                sel = None
            units.append((len(units), slice(c0, c0 + width), hh, sel))

    def key_rows(ref, prev_ref, cols, j):
        if j == 0:
            return jnp.concatenate([prev_ref[:, cols], ref[0:w, cols]], axis=0)
        return ref[(j - 1) * w:(j + 1) * w, cols]

    def scores(item):
        (_, cols, _, sel), j = item
        qb = q_ref[j * w:(j + 1) * w, cols]
        if sel is not None:
            qb = jnp.where(sel, qb, jnp.zeros_like(qb))
        return lax.dot_general(qb, key_rows(k_ref, kp_ref, cols, j), nt, preferred_element_type=F32)

    def finish(item, first, last, sc):
        (h, cols, hh, _), j = item
        if first:
            lse_scr[0:tq, :] = jnp.full((tq, LANES), NEG_INF, F32)
        lo = hh * HEAD_DIM
        rows = slice(j * w, (j + 1) * w)
        s = jnp.where(band_first if j == 0 else band, sc, NEG_INF)
        m = jnp.max(s, axis=-1, keepdims=True)
        e = jnp.exp(s - m)
        den = jnp.sum(e, axis=-1, keepdims=True)
        pv = jnp.dot(e.astype(BF16), key_rows(v_ref, vp_ref, cols, j), preferred_element_type=F32) / den
        o_ref[rows, cols.start + lo:cols.start + lo + HEAD_DIM] = pv[:, lo:lo + HEAD_DIM].astype(BF16)
        lse_scr[rows, :] = jnp.where(lane == h, m + jnp.log(den), lse_scr[rows, :])
        if last:
            lse = lse_scr[0:tq, :]
            mx = jnp.max(lse, axis=-1, keepdims=True)
            tot = jnp.sum(jnp.exp(lse - mx), axis=-1, keepdims=True)
            lse_ref[...] = jnp.broadcast_to(mx + jnp.log(tot) - math.log(n_heads), (tq, LANES))

    items = [(unit, j) for unit in units for j in range(nsub)]
    return [(functools.partial(scores, item), functools.partial(finish, item, n == 0, n == len(items) - 1))
            for n, item in enumerate(items)]


def _run_lookahead(work):
    pending = [scores() for scores, _ in work[:ATTN_LOOKAHEAD]]
    for n, (_, finish) in enumerate(work):
        if n + ATTN_LOOKAHEAD < len(work):
            pending.append(work[n + ATTN_LOOKAHEAD][0]())
        finish(pending.pop(0))


def _attn_kernel(q_ref, k_ref, v_ref, kp_ref, vp_ref, o_ref, lse_ref, lse_scr, *, w, n_heads):
    masks = _band_masks(w, pl.program_id(1) > 0)
    work = []
    for sq in range(q_ref.shape[0]):
        work += _attn_segment(masks, q_ref.at[sq], k_ref.at[sq], v_ref.at[sq], kp_ref.at[sq], vp_ref.at[sq],
                              o_ref.at[sq], lse_ref.at[sq], lse_scr, w=w, n_heads=n_heads)
    _run_lookahead(work)


def _attn_group(q, k, v, *, w, n_heads, rows=ATTN_STEP_ROWS):
    nseq, l, cg = q.shape
    tq = min(rows, l)
    nb = rows // tq
    ratio = tq // w
    kern = functools.partial(_attn_kernel, w=w, n_heads=n_heads)
    cur = pl.BlockSpec((nb, tq, cg), lambda n, i: (n, i, 0))
    prev = pl.BlockSpec((nb, w, cg), lambda n, i: (n, jnp.maximum(i * ratio - 1, 0), 0))
    return pl.pallas_call(
        kern,
        grid=(nseq // nb, l // tq),
        in_specs=[cur, cur, cur, prev, prev],
        out_specs=[cur, pl.BlockSpec((nb, tq, LANES), lambda n, i: (n, i, 0))],
        out_shape=[jax.ShapeDtypeStruct((nseq, l, cg), BF16),
                   jax.ShapeDtypeStruct((nseq, l, LANES), F32)],
        scratch_shapes=[pltpu.VMEM((tq, LANES), F32)],
        compiler_params=_params("arbitrary", "arbitrary"),
        name=f"attn_h{n_heads}_l{l}",
    )(q, k, v, k, v)


def _lane_chunks(blk, cg):
    chunks = []
    for c0 in range(0, cg, LANES):
        piece = blk[:, c0:min(c0 + LANES, cg)]
        if piece.shape[1] < LANES:
            piece = jnp.concatenate([piece, jnp.zeros((blk.shape[0], LANES - piece.shape[1]), F32)], axis=1)
        chunks.append(piece)
    return chunks


def _merge_steps(x_ref, o_refs, l_refs, wout_ref, scratch, x1_ref, *, ts, groups):
    ng = len(groups)
    outs, lses = [], []
    si = 0
    for gi, (d, cg) in enumerate(groups):
        if d == 1:
            outs.append(_lane_chunks(o_refs[gi][0].astype(F32), cg))
            lses.append(l_refs[gi][0])
            continue
        n = ts // d
        o_scr, l_scr = scratch[si], scratch[si + 1]
        si += 2
        for r in range(d):
            for c, piece in enumerate(_lane_chunks(o_refs[gi][r].astype(F32), cg)):
                o_scr[c, pl.ds(r, n, stride=d), :] = piece
            l_scr[pl.ds(r, n, stride=d), :] = l_refs[gi][r]
        outs.append([o_scr[c] for c in range(o_scr.shape[0])])
        lses.append(l_scr[...])
    m = functools.reduce(jnp.maximum, lses)
    es = [jnp.exp(l - m) for l in lses]
    inv = float(ng) / functools.reduce(lambda a, b2: a + b2, es)
    lane = lax.broadcasted_iota(jnp.int32, (1, LANES), 1)
    full, half = [], None
    for (d, cg), chunks, e in zip(groups, outs, es):
        alpha = (e * inv)[:, :1]
        for c, chunk in enumerate(chunks):
            chunk = chunk * alpha
            whole = cg - c * LANES >= LANES
            if half is None:
                if whole:
                    full.append(chunk)
                else:
                    half = chunk
            else:
                rot = pltpu.roll(chunk, HEAD_DIM, axis=1)
                full.append(jnp.where(lane < HEAD_DIM, half, rot))
                half = rot if whole else None
    assert half is None, "attention width must be a multiple of 128"
    merged = jnp.concatenate(full, axis=1).astype(BF16)
    yield
    x1_ref[...] = x_ref[...] + jnp.dot(merged, wout_ref[...], preferred_element_type=F32)
    yield


def _merge_ffn_kernel(x_ref, *refs, ts, groups, n_tiles):
    ng = len(groups)
    o_refs, l_refs = refs[:ng], refs[ng:2 * ng]
    wout_ref, gf_ref, wg_ref, wu_ref, wd_ref, gfin_ref, y_ref, x1_ref = refs[2 * ng:2 * ng + 8]
    scratch = refs[2 * ng + 8:]
    t = pl.program_id(0)
    make_mixer = functools.partial(_merge_steps, x_ref, o_refs, l_refs, wout_ref, scratch, x1_ref,
                                   ts=ts, groups=groups)
    make_ffn = functools.partial(_swiglu_steps, x1_ref, y_ref, gf_ref, wg_ref, wu_ref, wd_ref,
                                 lambda acc: _rmsnorm(acc, gfin_ref[...]))
    _skewed_step(t, n_tiles, make_mixer, make_ffn)


def _merge_ffn_layer(x, outs, lses, w_out, g_ffn, w_gate, w_up, w_down, g_final, groups, *,
                     attn_layer, layer, ts=TOKEN_TILE):
    b, s, d_model = x.shape
    per_batch = s // ts
    n_tiles = b * per_batch
    kern = functools.partial(_merge_ffn_kernel, ts=ts, groups=groups, n_tiles=n_tiles)

    def mixer_spec(block):
        lead = len(block) - 2
        return pl.BlockSpec(block, lambda t: (_tile_index(t, n_tiles, per_batch)[0],) + (0,) * (lead - 1)
                            + (_tile_index(t, n_tiles, per_batch)[1], 0))

    ffn_tile = pl.BlockSpec((None, ts, d_model),
                            lambda t: (*_tile_index(jnp.maximum(t - 1, 0), n_tiles, per_batch), 0))
    o_specs = [mixer_spec((None, d, ts // d, cg)) for d, cg in groups]
    l_specs = [mixer_spec((None, d, ts // d, LANES)) for d, cg in groups]
    scratch = [pltpu.VMEM((ts, d_model), F32)]
    for d, cg in groups:
        if d > 1:
            scratch += [pltpu.VMEM((pl.cdiv(cg, LANES), ts, LANES), F32), pltpu.VMEM((ts, LANES), F32)]
    return pl.pallas_call(
        kern,
        grid=(n_tiles + 1,),
        in_specs=[mixer_spec((None, ts, d_model))] + o_specs + l_specs + [_layer_spec(w_out, attn_layer)]
        + _ffn_specs(d_model, w_gate, w_up, w_down, layer) + [_const_spec((1, d_model))],
        out_specs=ffn_tile,
        out_shape=jax.ShapeDtypeStruct(x.shape, F32),
        scratch_shapes=scratch,
        compiler_params=_params("arbitrary"),
        name="attn_merge_ffn_final",
    )(x, *outs, *lses, w_out, g_ffn.reshape(1, d_model), w_gate, w_up, w_down, g_final.reshape(1, d_model))


def _rope_tables(s):
    half = HEAD_DIM // 2
    inv_freq = 1.0 / (ROPE_THETA ** (np.arange(0, HEAD_DIM, 2, dtype=np.float64) / HEAD_DIM))
    ang = np.arange(s, dtype=np.float64)[:, None] * inv_freq[None, :]
    cos, sin = np.cos(ang), np.sin(ang)
    reps = LANES // half
    cos_t = np.concatenate([cos] * reps, axis=-1)
    sin_t = np.concatenate([-sin, sin] * (reps // 2), axis=-1)
    return jnp.asarray(cos_t, F32), jnp.asarray(sin_t, F32)


def kernel(x, norm_mix, norm_ffn, norm_final, pool_w_in, pool_w_group, pool_scale, pool_w_out,
           attn_w_qkv, attn_w_out, ffn_w_gate, ffn_w_up, ffn_w_down):
    b, s, d_model = x.shape
    assert norm_mix.shape[0] == 2, "two layers: pooling mixer then dilated attention"
    n_heads = attn_w_out.shape[1] // HEAD_DIM
    d_attn = n_heads * HEAD_DIM
    head_groups = _head_groups(n_heads)
    groups = tuple((dil, nh * HEAD_DIM) for (_, dil), nh in zip(ATTN_PATTERNS, head_groups))
    bf = lambda t: t.astype(BF16)

    x = _pool_ffn_layer(x, norm_mix[0], bf(pool_w_in), bf(pool_w_group), pool_scale[0], bf(pool_w_out),
                        norm_ffn[0], ffn_w_gate, ffn_w_up, ffn_w_down, layer=0)

    w_qkv = attn_w_qkv[0]
    part_scale = (HEAD_DIM ** -0.5, 1.0, 1.0)
    w_groups, col = [], 0
    for _, cg in groups:
        w_groups.append(bf(jnp.concatenate(
            [w_qkv[:, p * d_attn + col:p * d_attn + col + cg] * part_scale[p] for p in range(3)], axis=1)))
        col += cg
    cos_t, sin_t = _rope_tables(s)
    qkv = _qkv_layer(x, norm_mix[1], cos_t, sin_t, w_groups, groups)
    outs, lses = [], []
    for gi, ((window, dil), nh) in enumerate(zip(ATTN_PATTERNS, head_groups)):
        cg = nh * HEAD_DIM
        q, k, v = (t.reshape(b * dil, s // dil, cg) for t in qkv[3 * gi:3 * gi + 3])
        o, lse = _attn_group(q, k, v, w=window // dil, n_heads=nh)
        outs.append(o.reshape(b, dil, s // dil, cg))
        lses.append(lse.reshape(b, dil, s // dil, LANES))
    return _merge_ffn_layer(x, outs, lses, bf(attn_w_out), norm_ffn[1], ffn_w_gate, ffn_w_up, ffn_w_down,
                            norm_final, groups, attn_layer=0, layer=1)
```

```python
import functools
import math

import jax
import jax.numpy as jnp
import numpy as np
from jax import lax
from jax.experimental import pallas as pl
from jax.experimental.pallas import tpu as pltpu

EPS = 1e-6
POOL_WINDOWS = (2, 4, 8, 16)
HEAD_DIM = 64
ATTN_PATTERNS = ((128, 1), (512, 4), (2048, 16))
ROPE_THETA = 10000.0
NEG_INF = -1e30

LANES = 128
VMEM_LIMIT_BYTES = 58 * 1024 * 1024
TOKEN_TILE = 512
QKV_TILE = 1024
QKV_SUB_TILE = 512
ATTN_STEP_ROWS = 2048
ATTN_STEP_SEQS = 2
ATTN_LOOKAHEAD = 2
FFN_CHUNK = 256
MIXER_GAP = 3
DEINT_STRIDE = 4

F32 = jnp.float32
BF16 = jnp.bfloat16


def _head_groups(n_heads):
    n = len(ATTN_PATTERNS)
    return tuple(n_heads // n + (1 if g < n_heads % n else 0) for g in range(n))


def _rmsnorm(x, g):
    ms = jnp.mean(x * x, axis=-1, keepdims=True)
    return x * lax.rsqrt(ms + EPS) * g


def _params(*sem):
    return pltpu.CompilerParams(dimension_semantics=sem, vmem_limit_bytes=VMEM_LIMIT_BYTES)


def _const_spec(shape):
    zeros = (0,) * len(shape)
    return pl.BlockSpec(shape, lambda *_: zeros, pipeline_mode=pl.Buffered(1))


def _layer_spec(stacked, layer):
    idx = (layer,) + (0,) * (stacked.ndim - 1)
    return pl.BlockSpec((None,) + stacked.shape[1:], lambda *_: idx, pipeline_mode=pl.Buffered(1))


def _swiglu_steps(x1_ref, o_ref, g_ref, wg_ref, wu_ref, wd_ref, finish):
    x1 = x1_ref[...]
    h = _rmsnorm(x1, g_ref[...]).astype(BF16)
    acc = x1
    n_chunks = wg_ref.shape[-1] // FFN_CHUNK
    for c in range(n_chunks):
        sl = slice(c * FFN_CHUNK, (c + 1) * FFN_CHUNK)
        gate = jnp.dot(h, wg_ref[:, sl].astype(BF16), preferred_element_type=F32)
        up = jnp.dot(h, wu_ref[:, sl].astype(BF16), preferred_element_type=F32)
        a = (gate * jax.nn.sigmoid(gate) * up).astype(BF16)
        acc = acc + jnp.dot(a, wd_ref[sl, :].astype(BF16), preferred_element_type=F32)
        if c == n_chunks - 1:
            o_ref[...] = finish(acc)
        yield


def _skewed_step(t, n_tiles, make_mixer, make_ffn):
    @pl.when(t == 0)
    def _():
        for _ in make_mixer():
            pass

    @pl.when(t == n_tiles)
    def _():
        for _ in make_ffn():
            pass

    @pl.when(jnp.logical_and(t > 0, t < n_tiles))
    def _():
        mixer = make_mixer()
        for c, _ in enumerate(make_ffn()):
            if c % MIXER_GAP == MIXER_GAP - 1:
                next(mixer, None)
        for _ in mixer:
            pass


def _ffn_specs(d, w_gate, w_up, w_down, layer):
    return [_const_spec((1, d)), _layer_spec(w_gate, layer), _layer_spec(w_up, layer), _layer_spec(w_down, layer)]


def _tile_index(t, n_tiles, per_batch):
    tt = jnp.minimum(t, n_tiles - 1)
    return tt // per_batch, tt % per_batch


def _pool_steps(i, x_ref, g_ref, win_ref, wgrp_ref, scale_ref, wout_ref, ext_ref, x1_ref, *, ts, halo):
    d_model = x_ref.shape[-1]
    gdim = d_model // len(POOL_WINDOWS)
    h = _rmsnorm(x_ref[...], g_ref[...]).astype(BF16)
    ext_ref[halo:halo + ts, :] = jnp.dot(h, win_ref[...], preferred_element_type=F32)
    yield
    pos = i * ts + lax.broadcasted_iota(jnp.int32, (ts, 1), 0)
    zs = []
    for g, w in enumerate(POOL_WINDOWS):
        cols = slice(g * gdim, (g + 1) * gdim)
        s = ext_ref[:, cols]
        k = 1
        while k < w:
            s = s + pltpu.roll(s, k, axis=0)
            k *= 2
        inv_cnt = 1.0 / jnp.minimum(pos + 1, w).astype(F32)
        p = s[halo:, :] * inv_cnt - ext_ref[halo:halo + ts, cols]
        zs.append(jnp.dot(p.astype(BF16), wgrp_ref[g], preferred_element_type=F32))
    yield
    z = jnp.concatenate(zs, axis=1) * scale_ref[...]
    y = jnp.dot(z.astype(BF16), wout_ref[...], preferred_element_type=F32)
    ext_ref[0:halo, :] = ext_ref[ts:ts + halo, :]
    x1_ref[...] = x_ref[...] + y
    yield


def _pool_ffn_kernel(x_ref, g_ref, win_ref, wgrp_ref, scale_ref, wout_ref,
                     gf_ref, wg_ref, wu_ref, wd_ref, o_ref, ext_ref, x1_ref, *, ts, halo, n_tiles, per_batch):
    t = pl.program_id(0)
    i = jnp.minimum(t, n_tiles - 1) % per_batch

    @pl.when(i == 0)
    def _():
        ext_ref[0:halo, :] = jnp.zeros((halo, x_ref.shape[-1]), F32)

    make_mixer = functools.partial(_pool_steps, i, x_ref, g_ref, win_ref, wgrp_ref, scale_ref, wout_ref,
                                   ext_ref, x1_ref, ts=ts, halo=halo)
    make_ffn = functools.partial(_swiglu_steps, x1_ref, o_ref, gf_ref, wg_ref, wu_ref, wd_ref, lambda acc: acc)
    _skewed_step(t, n_tiles, make_mixer, make_ffn)


def _pool_ffn_layer(x, g_mix, w_in, w_group, scale, w_out, g_ffn, w_gate, w_up, w_down, *, layer, ts=TOKEN_TILE):
    b, s, d = x.shape
    halo = max(POOL_WINDOWS)
    per_batch = s // ts
    n_tiles = b * per_batch
    kern = functools.partial(_pool_ffn_kernel, ts=ts, halo=halo, n_tiles=n_tiles, per_batch=per_batch)
    mixer_tile = pl.BlockSpec((None, ts, d), lambda t: (*_tile_index(t, n_tiles, per_batch), 0))
    ffn_tile = pl.BlockSpec((None, ts, d), lambda t: (*_tile_index(jnp.maximum(t - 1, 0), n_tiles, per_batch), 0))
    return pl.pallas_call(
        kern,
        grid=(n_tiles + 1,),
        in_specs=[mixer_tile, _const_spec((1, d)), _layer_spec(w_in, layer), _layer_spec(w_group, layer),
                  _const_spec((1, d)), _layer_spec(w_out, layer)] + _ffn_specs(d, w_gate, w_up, w_down, layer),
        out_specs=ffn_tile,
        out_shape=jax.ShapeDtypeStruct(x.shape, F32),
        scratch_shapes=[pltpu.VMEM((ts + halo, d), F32), pltpu.VMEM((ts, d), F32)],
        compiler_params=_params("arbitrary"),
        name="pool_ffn",
    )(x, g_mix.reshape(1, d), w_in, w_group, scale.reshape(1, d), w_out, g_ffn.reshape(1, d), w_gate, w_up, w_down)


def _class_order(d):
    order, cur = (0,), 1
    while cur < d:
        order = tuple(c + cur * r for c in order for r in range(DEINT_STRIDE))
        cur *= DEINT_STRIDE
    assert cur == d, "dilations must be powers of DEINT_STRIDE"
    return order


def _split_rows(src, n_blocks, blk):
    n = blk // DEINT_STRIDE
    return jnp.concatenate([src[pl.ds(b0 * blk + r, n, stride=DEINT_STRIDE), :]
                            for b0 in range(n_blocks) for r in range(DEINT_STRIDE)], axis=0)


def _qkv_kernel(x_ref, g_ref, cos_ref, sin_ref, *refs, ts, sub, groups):
    ng = len(groups)
    w_refs, out_refs = refs[:ng], refs[ng:-2]
    hs_all, tab_all = refs[-2:]
    lane = lax.broadcasted_iota(jnp.int32, (1, LANES), 1)
    first_half = (lane % HEAD_DIM) < (HEAD_DIM // 2)
    n_chunks = x_ref.shape[-1] // LANES
    for si, s0 in enumerate(range(0, ts, sub)):
        hs_ref, tab_ref = hs_all.at[si], tab_all.at[si]
        h = _rmsnorm(x_ref[s0:s0 + sub, :], g_ref[...])
        cs, sn = cos_ref[s0:s0 + sub, :], sin_ref[s0:s0 + sub, :]
        cur_d = 1
        for gi, (d, cg) in enumerate(groups):
            while cur_d < d:
                for c in range(n_chunks):
                    hs_ref[c] = h[:, c * LANES:(c + 1) * LANES]
                tab_ref[0], tab_ref[1] = cs, sn
                blk = sub // cur_d
                h = jnp.concatenate([_split_rows(hs_ref.at[c], cur_d, blk) for c in range(n_chunks)], axis=1)
                cs, sn = _split_rows(tab_ref.at[0], cur_d, blk), _split_rows(tab_ref.at[1], cur_d, blk)
                cur_d *= DEINT_STRIDE
            assert cur_d == d, "head groups must come in increasing powers of DEINT_STRIDE"
            qkv = jnp.dot(h.astype(BF16), w_refs[gi][...], preferred_element_type=F32)
            chunks = []
            for c in range(2 * cg // LANES):
                t = qkv[:, c * LANES:(c + 1) * LANES]
                swapped = jnp.where(first_half, pltpu.roll(t, LANES - HEAD_DIM // 2, axis=1),
                                    pltpu.roll(t, HEAD_DIM // 2, axis=1))
                chunks.append(t * cs + swapped * sn)
            qk = jnp.concatenate(chunks, axis=1)
            parts = (qk[:, :cg], qk[:, cg:2 * cg], qkv[:, 2 * cg:])
            n = sub // d
            for part, o_ref in zip(parts, out_refs[3 * gi:3 * gi + 3]):
                pb = part.astype(BF16)
                for slot, r in enumerate(_class_order(d)):
                    o_ref[r, si * n:(si + 1) * n, :] = pb[slot * n:(slot + 1) * n, :]


def _qkv_layer(x, g, cos, sin, w_groups, groups, *, ts=QKV_TILE, sub=QKV_SUB_TILE):
    b, s, d_model = x.shape
    kern = functools.partial(_qkv_kernel, ts=ts, sub=sub, groups=groups)
    out_shapes, out_specs = [], []
    for d, cg in groups:
        for _ in range(3):
            out_shapes.append(jax.ShapeDtypeStruct((b, d, s // d, cg), BF16))
            out_specs.append(pl.BlockSpec((None, d, ts // d, cg), lambda bi, i: (bi, 0, i, 0)))
    table_spec = pl.BlockSpec((ts, LANES), lambda bi, i: (i, 0))
    return pl.pallas_call(
        kern,
        grid=(b, s // ts),
        in_specs=[pl.BlockSpec((None, ts, d_model), lambda bi, i: (bi, i, 0)), _const_spec((1, d_model)),
                  table_spec, table_spec] + [_const_spec(w.shape) for w in w_groups],
        out_specs=out_specs,
        out_shape=out_shapes,
        scratch_shapes=[pltpu.VMEM((ts // sub, d_model // LANES, sub, LANES), F32),
                        pltpu.VMEM((ts // sub, 2, sub, LANES), F32)],
        compiler_params=_params("arbitrary", "arbitrary"),
        name="qkv_rope",
    )(x, g.reshape(1, d_model), cos, sin, *w_groups)


def _band_masks(w, has_prev):
    qi = lax.broadcasted_iota(jnp.int32, (w, 2 * w), 0)
    kj = lax.broadcasted_iota(jnp.int32, (w, 2 * w), 1)
    band = jnp.logical_and(kj >= qi, kj <= qi + w)
    return band, jnp.logical_and(band, jnp.logical_or(kj >= w, has_prev))


def _attn_segment(masks, q_ref, k_ref, v_ref, kp_ref, vp_ref, o_ref, lse_ref, lse_scr, *, w, n_heads):
    band, band_first = masks
    tq = q_ref.shape[0]
    cg = n_heads * HEAD_DIM
    nsub = tq // w
    lane = lax.broadcasted_iota(jnp.int32, (1, LANES), 1)
    nt = (((1,), (1,)), ((), ()))

    units = []
    for c0 in range(0, cg, LANES):
        width = min(LANES, cg - c0)
        for hh in range(width // HEAD_DIM):
            if width == LANES:
                sel = (lane < HEAD_DIM) if hh == 0 else (lane >= HEAD_DIM)
            else:
                sel = None
            units.append((len(units), slice(c0, c0 + width), hh, sel))

    def key_rows(ref, prev_ref, cols, j):
        if j == 0:
            return jnp.concatenate([prev_ref[:, cols], ref[0:w, cols]], axis=0)
        return ref[(j - 1) * w:(j + 1) * w, cols]

    def scores(item):
        (_, cols, _, sel), j = item
        qb = q_ref[j * w:(j + 1) * w, cols]
        if sel is not None:
            qb = jnp.where(sel, qb, jnp.zeros_like(qb))
        return lax.dot_general(qb, key_rows(k_ref, kp_ref, cols, j), nt, preferred_element_type=F32)

    def finish(item, first, last, sc):
        (h, cols, hh, _), j = item
        if first:
            lse_scr[0:tq, :] = jnp.full((tq, LANES), NEG_INF, F32)
        lo = hh * HEAD_DIM
        rows = slice(j * w, (j + 1) * w)
        s = jnp.where(band_first if j == 0 else band, sc, NEG_INF)
        m = jnp.max(s, axis=-1, keepdims=True)
        e = jnp.exp(s - m)
        den = jnp.sum(e, axis=-1, keepdims=True)
        pv = jnp.dot(e.astype(BF16), key_rows(v_ref, vp_ref, cols, j), preferred_element_type=F32) / den
        o_ref[rows, cols.start + lo:cols.start + lo + HEAD_DIM] = pv[:, lo:lo + HEAD_DIM].astype(BF16)
        lse_scr[rows, :] = jnp.where(lane == h, m + jnp.log(den), lse_scr[rows, :])
        if last:
            lse = lse_scr[0:tq, :]
            mx = jnp.max(lse, axis=-1, keepdims=True)
            tot = jnp.sum(jnp.exp(lse - mx), axis=-1, keepdims=True)
            lse_ref[...] = jnp.broadcast_to(mx + jnp.log(tot) - math.log(n_heads), (tq, LANES))

    items = [(unit, j) for unit in units for j in range(nsub)]
    return [(functools.partial(scores, item), functools.partial(finish, item, n == 0, n == len(items) - 1))
            for n, item in enumerate(items)]


def _run_lookahead(work):
    pending = [scores() for scores, _ in work[:ATTN_LOOKAHEAD]]
    for n, (_, finish) in enumerate(work):
        if n + ATTN_LOOKAHEAD < len(work):
            pending.append(work[n + ATTN_LOOKAHEAD][0]())
        finish(pending.pop(0))


def _attn_kernel(q_ref, k_ref, v_ref, kp_ref, vp_ref, o_ref, lse_ref, lse_scr, *, w, n_heads):
    masks = _band_masks(w, pl.program_id(1) > 0)
    work = []
    for sq in range(q_ref.shape[0]):
        work += _attn_segment(masks, q_ref.at[sq], k_ref.at[sq], v_ref.at[sq], kp_ref.at[sq], vp_ref.at[sq],
                              o_ref.at[sq], lse_ref.at[sq], lse_scr.at[sq], w=w, n_heads=n_heads)
    _run_lookahead(work)


def _attn_group(q, k, v, *, w, n_heads, rows=ATTN_STEP_ROWS):
    nseq, l, cg = q.shape
    tq = min(rows, l)
    nb = min(rows // tq, ATTN_STEP_SEQS)
    ratio = tq // w
    kern = functools.partial(_attn_kernel, w=w, n_heads=n_heads)
    cur = pl.BlockSpec((nb, tq, cg), lambda n, i: (n, i, 0))
    prev = pl.BlockSpec((nb, w, cg), lambda n, i: (n, jnp.maximum(i * ratio - 1, 0), 0))
    return pl.pallas_call(
        kern,
        grid=(nseq // nb, l // tq),
        in_specs=[cur, cur, cur, prev, prev],
        out_specs=[cur, pl.BlockSpec((nb, tq, LANES), lambda n, i: (n, i, 0))],
        out_shape=[jax.ShapeDtypeStruct((nseq, l, cg), BF16),
                   jax.ShapeDtypeStruct((nseq, l, LANES), F32)],
        scratch_shapes=[pltpu.VMEM((nb, tq, LANES), F32)],
        compiler_params=_params("arbitrary", "arbitrary"),
        name=f"attn_h{n_heads}_l{l}",
    )(q, k, v, k, v)


def _lane_chunks(blk, cg):
    chunks = []
    for c0 in range(0, cg, LANES):
        piece = blk[:, c0:min(c0 + LANES, cg)]
        if piece.shape[1] < LANES:
            piece = jnp.concatenate([piece, jnp.zeros((blk.shape[0], LANES - piece.shape[1]), F32)], axis=1)
        chunks.append(piece)
    return chunks


def _merge_steps(x_ref, o_refs, l_refs, wout_ref, scratch, x1_ref, *, ts, groups):
    ng = len(groups)
    outs, lses = [], []
    si = 0
    for gi, (d, cg) in enumerate(groups):
        if d == 1:
            outs.append(_lane_chunks(o_refs[gi][0].astype(F32), cg))
            lses.append(l_refs[gi][0])
            continue
        n = ts // d
        o_scr, l_scr = scratch[si], scratch[si + 1]
        si += 2
        for r in range(d):
            for c, piece in enumerate(_lane_chunks(o_refs[gi][r].astype(F32), cg)):
                o_scr[c, pl.ds(r, n, stride=d), :] = piece
            l_scr[pl.ds(r, n, stride=d), :] = l_refs[gi][r]
        outs.append([o_scr[c] for c in range(o_scr.shape[0])])
        lses.append(l_scr[...])
    m = functools.reduce(jnp.maximum, lses)
    es = [jnp.exp(l - m) for l in lses]
    inv = float(ng) / functools.reduce(lambda a, b2: a + b2, es)
    lane = lax.broadcasted_iota(jnp.int32, (1, LANES), 1)
    full, half = [], None
    for (d, cg), chunks, e in zip(groups, outs, es):
        alpha = (e * inv)[:, :1]
        for c, chunk in enumerate(chunks):
            chunk = chunk * alpha
            whole = cg - c * LANES >= LANES
            if half is None:
                if whole:
                    full.append(chunk)
                else:
                    half = chunk
            else:
                rot = pltpu.roll(chunk, HEAD_DIM, axis=1)
                full.append(jnp.where(lane < HEAD_DIM, half, rot))
                half = rot if whole else None
    assert half is None, "attention width must be a multiple of 128"
    merged = jnp.concatenate(full, axis=1).astype(BF16)
    yield
    x1_ref[...] = x_ref[...] + jnp.dot(merged, wout_ref[...], preferred_element_type=F32)
    yield


def _merge_ffn_kernel(x_ref, *refs, ts, groups, n_tiles):
    ng = len(groups)
    o_refs, l_refs = refs[:ng], refs[ng:2 * ng]
    wout_ref, gf_ref, wg_ref, wu_ref, wd_ref, gfin_ref, y_ref, x1_ref = refs[2 * ng:2 * ng + 8]
    scratch = refs[2 * ng + 8:]
    t = pl.program_id(0)
    make_mixer = functools.partial(_merge_steps, x_ref, o_refs, l_refs, wout_ref, scratch, x1_ref,
                                   ts=ts, groups=groups)
    make_ffn = functools.partial(_swiglu_steps, x1_ref, y_ref, gf_ref, wg_ref, wu_ref, wd_ref,
                                 lambda acc: _rmsnorm(acc, gfin_ref[...]))
    _skewed_step(t, n_tiles, make_mixer, make_ffn)


def _merge_ffn_layer(x, outs, lses, w_out, g_ffn, w_gate, w_up, w_down, g_final, groups, *,
                     attn_layer, layer, ts=TOKEN_TILE):
    b, s, d_model = x.shape
    per_batch = s // ts
    n_tiles = b * per_batch
    kern = functools.partial(_merge_ffn_kernel, ts=ts, groups=groups, n_tiles=n_tiles)

    def mixer_spec(block):
        lead = len(block) - 2
        return pl.BlockSpec(block, lambda t: (_tile_index(t, n_tiles, per_batch)[0],) + (0,) * (lead - 1)
                            + (_tile_index(t, n_tiles, per_batch)[1], 0))

    ffn_tile = pl.BlockSpec((None, ts, d_model),
                            lambda t: (*_tile_index(jnp.maximum(t - 1, 0), n_tiles, per_batch), 0))
    o_specs = [mixer_spec((None, d, ts // d, cg)) for d, cg in groups]
    l_specs = [mixer_spec((None, d, ts // d, LANES)) for d, cg in groups]
    scratch = [pltpu.VMEM((ts, d_model), F32)]
    for d, cg in groups:
        if d > 1:
            scratch += [pltpu.VMEM((pl.cdiv(cg, LANES), ts, LANES), F32), pltpu.VMEM((ts, LANES), F32)]
    return pl.pallas_call(
        kern,
        grid=(n_tiles + 1,),
        in_specs=[mixer_spec((None, ts, d_model))] + o_specs + l_specs + [_layer_spec(w_out, attn_layer)]
        + _ffn_specs(d_model, w_gate, w_up, w_down, layer) + [_const_spec((1, d_model))],
        out_specs=ffn_tile,
        out_shape=jax.ShapeDtypeStruct(x.shape, F32),
        scratch_shapes=scratch,
        compiler_params=_params("arbitrary"),
        name="attn_merge_ffn_final",
    )(x, *outs, *lses, w_out, g_ffn.reshape(1, d_model), w_gate, w_up, w_down, g_final.reshape(1, d_model))


def _rope_tables(s):
    half = HEAD_DIM // 2
    inv_freq = 1.0 / (ROPE_THETA ** (np.arange(0, HEAD_DIM, 2, dtype=np.float64) / HEAD_DIM))
    ang = np.arange(s, dtype=np.float64)[:, None] * inv_freq[None, :]
    cos, sin = np.cos(ang), np.sin(ang)
    reps = LANES // half
    cos_t = np.concatenate([cos] * reps, axis=-1)
    sin_t = np.concatenate([-sin, sin] * (reps // 2), axis=-1)
    return jnp.asarray(cos_t, F32), jnp.asarray(sin_t, F32)


def kernel(x, norm_mix, norm_ffn, norm_final, pool_w_in, pool_w_group, pool_scale, pool_w_out,
           attn_w_qkv, attn_w_out, ffn_w_gate, ffn_w_up, ffn_w_down):
    b, s, d_model = x.shape
    assert norm_mix.shape[0] == 2, "two layers: pooling mixer then dilated attention"
    n_heads = attn_w_out.shape[1] // HEAD_DIM
    d_attn = n_heads * HEAD_DIM
    head_groups = _head_groups(n_heads)
    groups = tuple((dil, nh * HEAD_DIM) for (_, dil), nh in zip(ATTN_PATTERNS, head_groups))
    bf = lambda t: t.astype(BF16)

    x = _pool_ffn_layer(x, norm_mix[0], bf(pool_w_in), bf(pool_w_group), pool_scale[0], bf(pool_w_out),
                        norm_ffn[0], ffn_w_gate, ffn_w_up, ffn_w_down, layer=0)

    w_qkv = attn_w_qkv[0]
    part_scale = (HEAD_DIM ** -0.5, 1.0, 1.0)
    w_groups, col = [], 0
    for _, cg in groups:
        w_groups.append(bf(jnp.concatenate(
            [w_qkv[:, p * d_attn + col:p * d_attn + col + cg] * part_scale[p] for p in range(3)], axis=1)))
        col += cg
    cos_t, sin_t = _rope_tables(s)
    qkv = _qkv_layer(x, norm_mix[1], cos_t, sin_t, w_groups, groups)
    outs, lses = [], []
    for gi, ((window, dil), nh) in enumerate(zip(ATTN_PATTERNS, head_groups)):
        cg = nh * HEAD_DIM
        q, k, v = (t.reshape(b * dil, s // dil, cg) for t in qkv[3 * gi:3 * gi + 3])
        o, lse = _attn_group(q, k, v, w=window // dil, n_heads=nh)
        outs.append(o.reshape(b, dil, s // dil, cg))
        lses.append(lse.reshape(b, dil, s // dil, LANES))
    return _merge_ffn_layer(x, outs, lses, bf(attn_w_out), norm_ffn[1], ffn_w_gate, ffn_w_up, ffn_w_down,
                            norm_final, groups, attn_layer=0, layer=1)
```

```python
import functools
import math

import jax
import jax.numpy as jnp
import numpy as np
from jax import lax
from jax.experimental import pallas as pl
from jax.experimental.pallas import tpu as pltpu

EPS = 1e-6
POOL_WINDOWS = (2, 4, 8, 16)
HEAD_DIM = 64
ATTN_PATTERNS = ((128, 1), (512, 4), (2048, 16))
ROPE_THETA = 10000.0
NEG_INF = -1e30

LANES = 128
VMEM_LIMIT_BYTES = 58 * 1024 * 1024
TOKEN_TILE = 512
QKV_TILE = 1024
QKV_SUB_TILE = 256
ATTN_STEP_ROWS = 2048
ATTN_STEP_SEQS = 2
ATTN_LOOKAHEAD = 2
FFN_CHUNK = 256
MIXER_GAP = 3
DEINT_STRIDE = 4

F32 = jnp.float32
BF16 = jnp.bfloat16


def _head_groups(n_heads):
    n = len(ATTN_PATTERNS)
    return tuple(n_heads // n + (1 if g < n_heads % n else 0) for g in range(n))


def _rmsnorm(x, g):
    ms = jnp.mean(x * x, axis=-1, keepdims=True)
    return x * lax.rsqrt(ms + EPS) * g


def _params(*sem):
    return pltpu.CompilerParams(dimension_semantics=sem, vmem_limit_bytes=VMEM_LIMIT_BYTES)


def _const_spec(shape):
    zeros = (0,) * len(shape)
    return pl.BlockSpec(shape, lambda *_: zeros, pipeline_mode=pl.Buffered(1))


def _layer_spec(stacked, layer):
    idx = (layer,) + (0,) * (stacked.ndim - 1)
    return pl.BlockSpec((None,) + stacked.shape[1:], lambda *_: idx, pipeline_mode=pl.Buffered(1))


def _swiglu_steps(x1_ref, o_ref, g_ref, wg_ref, wu_ref, wd_ref, finish):
    x1 = x1_ref[...]
    h = _rmsnorm(x1, g_ref[...]).astype(BF16)
    acc = x1
    n_chunks = wg_ref.shape[-1] // FFN_CHUNK
    for c in range(n_chunks):
        sl = slice(c * FFN_CHUNK, (c + 1) * FFN_CHUNK)
        gate = jnp.dot(h, wg_ref[:, sl].astype(BF16), preferred_element_type=F32)
        up = jnp.dot(h, wu_ref[:, sl].astype(BF16), preferred_element_type=F32)
        a = (gate * jax.nn.sigmoid(gate) * up).astype(BF16)
        acc = acc + jnp.dot(a, wd_ref[sl, :].astype(BF16), preferred_element_type=F32)
        if c == n_chunks - 1:
            o_ref[...] = finish(acc)
        yield


def _skewed_step(t, n_tiles, make_mixer, make_ffn):
    @pl.when(t == 0)
    def _():
        for _ in make_mixer():
            pass

    @pl.when(t == n_tiles)
    def _():
        for _ in make_ffn():
            pass

    @pl.when(jnp.logical_and(t > 0, t < n_tiles))
    def _():
        mixer = make_mixer()
        for c, _ in enumerate(make_ffn()):
            if c % MIXER_GAP == MIXER_GAP - 1:
                next(mixer, None)
        for _ in mixer:
            pass


def _ffn_specs(d, w_gate, w_up, w_down, layer):
    return [_const_spec((1, d)), _layer_spec(w_gate, layer), _layer_spec(w_up, layer), _layer_spec(w_down, layer)]


def _tile_index(t, n_tiles, per_batch):
    tt = jnp.minimum(t, n_tiles - 1)
    return tt // per_batch, tt % per_batch


def _pool_steps(i, x_ref, g_ref, win_ref, wgrp_ref, scale_ref, wout_ref, ext_ref, x1_ref, *, ts, halo):
    d_model = x_ref.shape[-1]
    gdim = d_model // len(POOL_WINDOWS)
    h = _rmsnorm(x_ref[...], g_ref[...]).astype(BF16)
    ext_ref[halo:halo + ts, :] = jnp.dot(h, win_ref[...], preferred_element_type=F32)
    yield
    pos = i * ts + lax.broadcasted_iota(jnp.int32, (ts, 1), 0)
    zs = []
    for g, w in enumerate(POOL_WINDOWS):
        cols = slice(g * gdim, (g + 1) * gdim)
        s = ext_ref[:, cols]
        k = 1
        while k < w:
            s = s + pltpu.roll(s, k, axis=0)
            k *= 2
        inv_cnt = 1.0 / jnp.minimum(pos + 1, w).astype(F32)
        p = s[halo:, :] * inv_cnt - ext_ref[halo:halo + ts, cols]
        zs.append(jnp.dot(p.astype(BF16), wgrp_ref[g], preferred_element_type=F32))
    yield
    z = jnp.concatenate(zs, axis=1) * scale_ref[...]
    y = jnp.dot(z.astype(BF16), wout_ref[...], preferred_element_type=F32)
    ext_ref[0:halo, :] = ext_ref[ts:ts + halo, :]
    x1_ref[...] = x_ref[...] + y
    yield


def _pool_ffn_kernel(x_ref, g_ref, win_ref, wgrp_ref, scale_ref, wout_ref,
                     gf_ref, wg_ref, wu_ref, wd_ref, o_ref, ext_ref, x1_ref, *, ts, halo, n_tiles, per_batch):
    t = pl.program_id(0)
    i = jnp.minimum(t, n_tiles - 1) % per_batch

    @pl.when(i == 0)
    def _():
        ext_ref[0:halo, :] = jnp.zeros((halo, x_ref.shape[-1]), F32)

    make_mixer = functools.partial(_pool_steps, i, x_ref, g_ref, win_ref, wgrp_ref, scale_ref, wout_ref,
                                   ext_ref, x1_ref, ts=ts, halo=halo)
    make_ffn = functools.partial(_swiglu_steps, x1_ref, o_ref, gf_ref, wg_ref, wu_ref, wd_ref, lambda acc: acc)
    _skewed_step(t, n_tiles, make_mixer, make_ffn)


def _pool_ffn_layer(x, g_mix, w_in, w_group, scale, w_out, g_ffn, w_gate, w_up, w_down, *, layer, ts=TOKEN_TILE):
    b, s, d = x.shape
    halo = max(POOL_WINDOWS)
    per_batch = s // ts
    n_tiles = b * per_batch
    kern = functools.partial(_pool_ffn_kernel, ts=ts, halo=halo, n_tiles=n_tiles, per_batch=per_batch)
    mixer_tile = pl.BlockSpec((None, ts, d), lambda t: (*_tile_index(t, n_tiles, per_batch), 0))
    ffn_tile = pl.BlockSpec((None, ts, d), lambda t: (*_tile_index(jnp.maximum(t - 1, 0), n_tiles, per_batch), 0))
    return pl.pallas_call(
        kern,
        grid=(n_tiles + 1,),
        in_specs=[mixer_tile, _const_spec((1, d)), _layer_spec(w_in, layer), _layer_spec(w_group, layer),
                  _const_spec((1, d)), _layer_spec(w_out, layer)] + _ffn_specs(d, w_gate, w_up, w_down, layer),
        out_specs=ffn_tile,
        out_shape=jax.ShapeDtypeStruct(x.shape, F32),
        scratch_shapes=[pltpu.VMEM((ts + halo, d), F32), pltpu.VMEM((ts, d), F32)],
        compiler_params=_params("arbitrary"),
        name="pool_ffn",
    )(x, g_mix.reshape(1, d), w_in, w_group, scale.reshape(1, d), w_out, g_ffn.reshape(1, d), w_gate, w_up, w_down)


def _class_order(d):
    order, cur = (0,), 1
    while cur < d:
        order = tuple(c + cur * r for c in order for r in range(DEINT_STRIDE))
        cur *= DEINT_STRIDE
    assert cur == d, "dilations must be powers of DEINT_STRIDE"
    return order


def _split_rows(src, n_blocks, blk):
    n = blk // DEINT_STRIDE
    return jnp.concatenate([src[pl.ds(b0 * blk + r, n, stride=DEINT_STRIDE), :]
                            for b0 in range(n_blocks) for r in range(DEINT_STRIDE)], axis=0)


def _qkv_kernel(x_ref, g_ref, cos_ref, sin_ref, *refs, ts, sub, groups):
    ng = len(groups)
    w_refs, out_refs = refs[:ng], refs[ng:-2]
    hs_all, tab_all = refs[-2:]
    lane = lax.broadcasted_iota(jnp.int32, (1, LANES), 1)
    first_half = (lane % HEAD_DIM) < (HEAD_DIM // 2)
    n_chunks = x_ref.shape[-1] // LANES
    for si, s0 in enumerate(range(0, ts, sub)):
        hs_ref, tab_ref = hs_all.at[si], tab_all.at[si]
        h = _rmsnorm(x_ref[s0:s0 + sub, :], g_ref[...])
        cs, sn = cos_ref[s0:s0 + sub, :], sin_ref[s0:s0 + sub, :]
        cur_d = 1
        for gi, (d, cg) in enumerate(groups):
            while cur_d < d:
                for c in range(n_chunks):
                    hs_ref[c] = h[:, c * LANES:(c + 1) * LANES]
                tab_ref[0], tab_ref[1] = cs, sn
                blk = sub // cur_d
                h = jnp.concatenate([_split_rows(hs_ref.at[c], cur_d, blk) for c in range(n_chunks)], axis=1)
                cs, sn = _split_rows(tab_ref.at[0], cur_d, blk), _split_rows(tab_ref.at[1], cur_d, blk)
                cur_d *= DEINT_STRIDE
            assert cur_d == d, "head groups must come in increasing powers of DEINT_STRIDE"
            qkv = jnp.dot(h.astype(BF16), w_refs[gi][...], preferred_element_type=F32)
            chunks = []
            for c in range(2 * cg // LANES):
                t = qkv[:, c * LANES:(c + 1) * LANES]
                swapped = jnp.where(first_half, pltpu.roll(t, LANES - HEAD_DIM // 2, axis=1),
                                    pltpu.roll(t, HEAD_DIM // 2, axis=1))
                chunks.append(t * cs + swapped * sn)
            qk = jnp.concatenate(chunks, axis=1)
            parts = (qk[:, :cg], qk[:, cg:2 * cg], qkv[:, 2 * cg:])
            n = sub // d
            for part, o_ref in zip(parts, out_refs[3 * gi:3 * gi + 3]):
                pb = part.astype(BF16)
                for slot, r in enumerate(_class_order(d)):
                    o_ref[r, si * n:(si + 1) * n, :] = pb[slot * n:(slot + 1) * n, :]


def _qkv_layer(x, g, cos, sin, w_groups, groups, *, ts=QKV_TILE, sub=QKV_SUB_TILE):
    b, s, d_model = x.shape
    kern = functools.partial(_qkv_kernel, ts=ts, sub=sub, groups=groups)
    out_shapes, out_specs = [], []
    for d, cg in groups:
        for _ in range(3):
            out_shapes.append(jax.ShapeDtypeStruct((b, d, s // d, cg), BF16))
            out_specs.append(pl.BlockSpec((None, d, ts // d, cg), lambda bi, i: (bi, 0, i, 0)))
    table_spec = pl.BlockSpec((ts, LANES), lambda bi, i: (i, 0))
    return pl.pallas_call(
        kern,
        grid=(b, s // ts),
        in_specs=[pl.BlockSpec((None, ts, d_model), lambda bi, i: (bi, i, 0)), _const_spec((1, d_model)),
                  table_spec, table_spec] + [_const_spec(w.shape) for w in w_groups],
        out_specs=out_specs,
        out_shape=out_shapes,
        scratch_shapes=[pltpu.VMEM((ts // sub, d_model // LANES, sub, LANES), F32),
                        pltpu.VMEM((ts // sub, 2, sub, LANES), F32)],
        compiler_params=_params("arbitrary", "arbitrary"),
        name="qkv_rope",
    )(x, g.reshape(1, d_model), cos, sin, *w_groups)


def _band_masks(w, has_prev):
    qi = lax.broadcasted_iota(jnp.int32, (w, 2 * w), 0)
    kj = lax.broadcasted_iota(jnp.int32, (w, 2 * w), 1)
    band = jnp.logical_and(kj >= qi, kj <= qi + w)
    return band, jnp.logical_and(band, jnp.logical_or(kj >= w, has_prev))


def _attn_segment(masks, q_ref, k_ref, v_ref, kp_ref, vp_ref, o_ref, lse_ref, lse_scr, *, w, n_heads):
    band, band_first = masks
    tq = q_ref.shape[0]
    cg = n_heads * HEAD_DIM
    nsub = tq // w
    lane = lax.broadcasted_iota(jnp.int32, (1, LANES), 1)
    nt = (((1,), (1,)), ((), ()))

    units = []
    for c0 in range(0, cg, LANES):
        width = min(LANES, cg - c0)
        for hh in range(width // HEAD_DIM):
            if width == LANES:
                sel = (lane < HEAD_DIM) if hh == 0 else (lane >= HEAD_DIM)
            else:
                sel = None
            units.append((len(units), slice(c0, c0 + width), hh, sel))

    def key_rows(ref, prev_ref, cols, j):
        if j == 0:
            return jnp.concatenate([prev_ref[:, cols], ref[0:w, cols]], axis=0)
        return ref[(j - 1) * w:(j + 1) * w, cols]

    def scores(item):
        (_, cols, _, sel), j = item
        qb = q_ref[j * w:(j + 1) * w, cols]
        if sel is not None:
            qb = jnp.where(sel, qb, jnp.zeros_like(qb))
        return lax.dot_general(qb, key_rows(k_ref, kp_ref, cols, j), nt, preferred_element_type=F32)

    def finish(item, first, last, sc):
        (h, cols, hh, _), j = item
        if first:
            lse_scr[0:tq, :] = jnp.full((tq, LANES), NEG_INF, F32)
        lo = hh * HEAD_DIM
        rows = slice(j * w, (j + 1) * w)
        s = jnp.where(band_first if j == 0 else band, sc, NEG_INF)
        m = jnp.max(s, axis=-1, keepdims=True)
        e = jnp.exp(s - m)
        den = jnp.sum(e, axis=-1, keepdims=True)
        pv = jnp.dot(e.astype(BF16), key_rows(v_ref, vp_ref, cols, j), preferred_element_type=F32) / den
        o_ref[rows, cols.start + lo:cols.start + lo + HEAD_DIM] = pv[:, lo:lo + HEAD_DIM].astype(BF16)
        lse_scr[rows, :] = jnp.where(lane == h, m + jnp.log(den), lse_scr[rows, :])
        if last:
            lse = lse_scr[0:tq, :]
            mx = jnp.max(lse, axis=-1, keepdims=True)
            tot = jnp.sum(jnp.exp(lse - mx), axis=-1, keepdims=True)
            lse_ref[...] = jnp.broadcast_to(mx + jnp.log(tot) - math.log(n_heads), (tq, LANES))

    items = [(unit, j) for unit in units for j in range(nsub)]
    return [(functools.partial(scores, item), functools.partial(finish, item, n == 0, n == len(items) - 1))
            for n, item in enumerate(items)]


def _run_lookahead(work):
    pending = [scores() for scores, _ in work[:ATTN_LOOKAHEAD]]
    for n, (_, finish) in enumerate(work):
        if n + ATTN_LOOKAHEAD < len(work):
            pending.append(work[n + ATTN_LOOKAHEAD][0]())
        finish(pending.pop(0))


def _attn_kernel(q_ref, k_ref, v_ref, kp_ref, vp_ref, o_ref, lse_ref, lse_scr, *, w, n_heads):
    masks = _band_masks(w, pl.program_id(1) > 0)
    work = []
    for sq in range(q_ref.shape[0]):
        work += _attn_segment(masks, q_ref.at[sq], k_ref.at[sq], v_ref.at[sq], kp_ref.at[sq], vp_ref.at[sq],
                              o_ref.at[sq], lse_ref.at[sq], lse_scr.at[sq], w=w, n_heads=n_heads)
    _run_lookahead(work)


def _attn_group(q, k, v, *, w, n_heads, rows=ATTN_STEP_ROWS):
    nseq, l, cg = q.shape
    tq = min(rows, l)
    nb = min(rows // tq, ATTN_STEP_SEQS)
    ratio = tq // w
    kern = functools.partial(_attn_kernel, w=w, n_heads=n_heads)
    cur = pl.BlockSpec((nb, tq, cg), lambda n, i: (n, i, 0))
    prev = pl.BlockSpec((nb, w, cg), lambda n, i: (n, jnp.maximum(i * ratio - 1, 0), 0))
    return pl.pallas_call(
        kern,
        grid=(nseq // nb, l // tq),
        in_specs=[cur, cur, cur, prev, prev],
        out_specs=[cur, pl.BlockSpec((nb, tq, LANES), lambda n, i: (n, i, 0))],
        out_shape=[jax.ShapeDtypeStruct((nseq, l, cg), BF16),
                   jax.ShapeDtypeStruct((nseq, l, LANES), F32)],
        scratch_shapes=[pltpu.VMEM((nb, tq, LANES), F32)],
        compiler_params=_params("arbitrary", "arbitrary"),
        name=f"attn_h{n_heads}_l{l}",
    )(q, k, v, k, v)


def _lane_chunks(blk, cg):
    chunks = []
    for c0 in range(0, cg, LANES):
        piece = blk[:, c0:min(c0 + LANES, cg)]
        if piece.shape[1] < LANES:
            piece = jnp.concatenate([piece, jnp.zeros((blk.shape[0], LANES - piece.shape[1]), F32)], axis=1)
        chunks.append(piece)
    return chunks


def _merge_steps(x_ref, o_refs, l_refs, wout_ref, scratch, x1_ref, *, ts, groups):
    ng = len(groups)
    outs, lses = [], []
    si = 0
    for gi, (d, cg) in enumerate(groups):
        if d == 1:
            outs.append(_lane_chunks(o_refs[gi][0].astype(F32), cg))
            lses.append(l_refs[gi][0])
            continue
        n = ts // d
        o_scr, l_scr = scratch[si], scratch[si + 1]
        si += 2
        for r in range(d):
            for c, piece in enumerate(_lane_chunks(o_refs[gi][r].astype(F32), cg)):
                o_scr[c, pl.ds(r, n, stride=d), :] = piece
            l_scr[pl.ds(r, n, stride=d), :] = l_refs[gi][r]
        outs.append([o_scr[c] for c in range(o_scr.shape[0])])
        lses.append(l_scr[...])
    m = functools.reduce(jnp.maximum, lses)
    es = [jnp.exp(l - m) for l in lses]
    inv = float(ng) / functools.reduce(lambda a, b2: a + b2, es)
    lane = lax.broadcasted_iota(jnp.int32, (1, LANES), 1)
    full, half = [], None
    for (d, cg), chunks, e in zip(groups, outs, es):
        alpha = (e * inv)[:, :1]
        for c, chunk in enumerate(chunks):
            chunk = chunk * alpha
            whole = cg - c * LANES >= LANES
            if half is None:
                if whole:
                    full.append(chunk)
                else:
                    half = chunk
            else:
                rot = pltpu.roll(chunk, HEAD_DIM, axis=1)
                full.append(jnp.where(lane < HEAD_DIM, half, rot))
                half = rot if whole else None
    assert half is None, "attention width must be a multiple of 128"
    merged = jnp.concatenate(full, axis=1).astype(BF16)
    yield
    x1_ref[...] = x_ref[...] + jnp.dot(merged, wout_ref[...], preferred_element_type=F32)
    yield


def _merge_ffn_kernel(x_ref, *refs, ts, groups, n_tiles):
    ng = len(groups)
    o_refs, l_refs = refs[:ng], refs[ng:2 * ng]
    wout_ref, gf_ref, wg_ref, wu_ref, wd_ref, gfin_ref, y_ref, x1_ref = refs[2 * ng:2 * ng + 8]
    scratch = refs[2 * ng + 8:]
    t = pl.program_id(0)
    make_mixer = functools.partial(_merge_steps, x_ref, o_refs, l_refs, wout_ref, scratch, x1_ref,
                                   ts=ts, groups=groups)
    make_ffn = functools.partial(_swiglu_steps, x1_ref, y_ref, gf_ref, wg_ref, wu_ref, wd_ref,
                                 lambda acc: _rmsnorm(acc, gfin_ref[...]))
    _skewed_step(t, n_tiles, make_mixer, make_ffn)


def _merge_ffn_layer(x, outs, lses, w_out, g_ffn, w_gate, w_up, w_down, g_final, groups, *,
                     attn_layer, layer, ts=TOKEN_TILE):
    b, s, d_model = x.shape
    per_batch = s // ts
    n_tiles = b * per_batch
    kern = functools.partial(_merge_ffn_kernel, ts=ts, groups=groups, n_tiles=n_tiles)

    def mixer_spec(block):
        lead = len(block) - 2
        return pl.BlockSpec(block, lambda t: (_tile_index(t, n_tiles, per_batch)[0],) + (0,) * (lead - 1)
                            + (_tile_index(t, n_tiles, per_batch)[1], 0))

    ffn_tile = pl.BlockSpec((None, ts, d_model),
                            lambda t: (*_tile_index(jnp.maximum(t - 1, 0), n_tiles, per_batch), 0))
    o_specs = [mixer_spec((None, d, ts // d, cg)) for d, cg in groups]
    l_specs = [mixer_spec((None, d, ts // d, LANES)) for d, cg in groups]
    scratch = [pltpu.VMEM((ts, d_model), F32)]
    for d, cg in groups:
        if d > 1:
            scratch += [pltpu.VMEM((pl.cdiv(cg, LANES), ts, LANES), F32), pltpu.VMEM((ts, LANES), F32)]
    return pl.pallas_call(
        kern,
        grid=(n_tiles + 1,),
        in_specs=[mixer_spec((None, ts, d_model))] + o_specs + l_specs + [_layer_spec(w_out, attn_layer)]
        + _ffn_specs(d_model, w_gate, w_up, w_down, layer) + [_const_spec((1, d_model))],
        out_specs=ffn_tile,
        out_shape=jax.ShapeDtypeStruct(x.shape, F32),
        scratch_shapes=scratch,
        compiler_params=_params("arbitrary"),
        name="attn_merge_ffn_final",
    )(x, *outs, *lses, w_out, g_ffn.reshape(1, d_model), w_gate, w_up, w_down, g_final.reshape(1, d_model))


def _rope_tables(s):
    half = HEAD_DIM // 2
    inv_freq = 1.0 / (ROPE_THETA ** (np.arange(0, HEAD_DIM, 2, dtype=np.float64) / HEAD_DIM))
    ang = np.arange(s, dtype=np.float64)[:, None] * inv_freq[None, :]
    cos, sin = np.cos(ang), np.sin(ang)
    reps = LANES // half
    cos_t = np.concatenate([cos] * reps, axis=-1)
    sin_t = np.concatenate([-sin, sin] * (reps // 2), axis=-1)
    return jnp.asarray(cos_t, F32), jnp.asarray(sin_t, F32)


def kernel(x, norm_mix, norm_ffn, norm_final, pool_w_in, pool_w_group, pool_scale, pool_w_out,
           attn_w_qkv, attn_w_out, ffn_w_gate, ffn_w_up, ffn_w_down):
    b, s, d_model = x.shape
    assert norm_mix.shape[0] == 2, "two layers: pooling mixer then dilated attention"
    n_heads = attn_w_out.shape[1] // HEAD_DIM
    d_attn = n_heads * HEAD_DIM
    head_groups = _head_groups(n_heads)
    groups = tuple((dil, nh * HEAD_DIM) for (_, dil), nh in zip(ATTN_PATTERNS, head_groups))
    bf = lambda t: t.astype(BF16)

    x = _pool_ffn_layer(x, norm_mix[0], bf(pool_w_in), bf(pool_w_group), pool_scale[0], bf(pool_w_out),
                        norm_ffn[0], ffn_w_gate, ffn_w_up, ffn_w_down, layer=0)

    w_qkv = attn_w_qkv[0]
    part_scale = (HEAD_DIM ** -0.5, 1.0, 1.0)
    w_groups, col = [], 0
    for _, cg in groups:
        w_groups.append(bf(jnp.concatenate(
            [w_qkv[:, p * d_attn + col:p * d_attn + col + cg] * part_scale[p] for p in range(3)], axis=1)))
        col += cg
    cos_t, sin_t = _rope_tables(s)
    qkv = _qkv_layer(x, norm_mix[1], cos_t, sin_t, w_groups, groups)
    outs, lses = [], []
    for gi, ((window, dil), nh) in enumerate(zip(ATTN_PATTERNS, head_groups)):
        cg = nh * HEAD_DIM
        q, k, v = (t.reshape(b * dil, s // dil, cg) for t in qkv[3 * gi:3 * gi + 3])
        o, lse = _attn_group(q, k, v, w=window // dil, n_heads=nh)
        outs.append(o.reshape(b, dil, s // dil, cg))
        lses.append(lse.reshape(b, dil, s // dil, LANES))
    return _merge_ffn_layer(x, outs, lses, bf(attn_w_out), norm_ffn[1], ffn_w_gate, ffn_w_up, ffn_w_down,
                            norm_final, groups, attn_layer=0, layer=1)
```

```python
import functools
import math

import jax
import jax.numpy as jnp
import numpy as np
from jax import lax
from jax.experimental import pallas as pl
from jax.experimental.pallas import tpu as pltpu

EPS = 1e-6
POOL_WINDOWS = (2, 4, 8, 16)
HEAD_DIM = 64
ATTN_PATTERNS = ((128, 1), (512, 4), (2048, 16))
ROPE_THETA = 10000.0
NEG_INF = -1e30

LANES = 128
VMEM_LIMIT_BYTES = 58 * 1024 * 1024
TOKEN_TILE = 512
QKV_TILE = 1024
QKV_SUB_TILE = 256
ATTN_STEP_ROWS = 2048
ATTN_STEP_SEQS = 2
ATTN_LOOKAHEAD = 2
FFN_CHUNK = 256
MIXER_GAP = 3
DEINT_STRIDE = 4

F32 = jnp.float32
BF16 = jnp.bfloat16


def _head_groups(n_heads):
    n = len(ATTN_PATTERNS)
    return tuple(n_heads // n + (1 if g < n_heads % n else 0) for g in range(n))


def _rmsnorm(x, g):
    ms = jnp.mean(x * x, axis=-1, keepdims=True)
    return x * lax.rsqrt(ms + EPS) * g


def _params(*sem):
    return pltpu.CompilerParams(dimension_semantics=sem, vmem_limit_bytes=VMEM_LIMIT_BYTES)


def _const_spec(shape):
    zeros = (0,) * len(shape)
    return pl.BlockSpec(shape, lambda *_: zeros, pipeline_mode=pl.Buffered(1))


def _layer_spec(stacked, layer):
    idx = (layer,) + (0,) * (stacked.ndim - 1)
    return pl.BlockSpec((None,) + stacked.shape[1:], lambda *_: idx, pipeline_mode=pl.Buffered(1))


def _swiglu_steps(x1_ref, o_ref, g_ref, wg_ref, wu_ref, wd_ref, finish):
    x1 = x1_ref[...]
    h = _rmsnorm(x1, g_ref[...]).astype(BF16)
    acc = x1
    n_chunks = wg_ref.shape[-1] // FFN_CHUNK
    for c in range(n_chunks):
        sl = slice(c * FFN_CHUNK, (c + 1) * FFN_CHUNK)
        gate = jnp.dot(h, wg_ref[:, sl], preferred_element_type=F32)
        up = jnp.dot(h, wu_ref[:, sl], preferred_element_type=F32)
        a = (gate * jax.nn.sigmoid(gate) * up).astype(BF16)
        acc = acc + jnp.dot(a, wd_ref[sl, :], preferred_element_type=F32)
        if c == n_chunks - 1:
            o_ref[...] = finish(acc)
        yield


def _load_ffn_weights(layer, hbm_refs, bf_refs, stage_refs, sem, meanwhile):
    n_chunks = hbm_refs[0].shape[-1] // FFN_CHUNK

    def chunk(c):
        return slice(c * FFN_CHUNK, (c + 1) * FFN_CHUNK)

    def copies(c):
        srcs = (hbm_refs[0].at[layer, :, chunk(c)], hbm_refs[1].at[layer, :, chunk(c)],
                hbm_refs[2].at[layer, chunk(c), :])
        return [pltpu.make_async_copy(src, stage.at[c % 2], sem.at[k, c % 2])
                for k, (src, stage) in enumerate(zip(srcs, stage_refs))]

    for cp in copies(0):
        cp.start()
    meanwhile()
    for c in range(n_chunks):
        if c + 1 < n_chunks:
            for cp in copies(c + 1):
                cp.start()
        for cp in copies(c):
            cp.wait()
        bf_refs[0][:, chunk(c)] = stage_refs[0][c % 2].astype(BF16)
        bf_refs[1][:, chunk(c)] = stage_refs[1][c % 2].astype(BF16)
        bf_refs[2][chunk(c), :] = stage_refs[2][c % 2].astype(BF16)


def _skewed_step(t, n_tiles, make_mixer, make_ffn, load_weights):
    @pl.when(t == 0)
    def _():
        def run_mixer():
            for _ in make_mixer():
                pass
        load_weights(run_mixer)

    @pl.when(t == n_tiles)
    def _():
        for _ in make_ffn():
            pass

    @pl.when(jnp.logical_and(t > 0, t < n_tiles))
    def _():
        mixer = make_mixer()
        for c, _ in enumerate(make_ffn()):
            if c % MIXER_GAP == MIXER_GAP - 1:
                next(mixer, None)
        for _ in mixer:
            pass


def _ffn_specs(d):
    hbm = pl.BlockSpec(memory_space=pl.ANY)
    return [_const_spec((1, d)), hbm, hbm, hbm]


def _ffn_scratch(d, f):
    return [pltpu.VMEM((d, f), BF16), pltpu.VMEM((d, f), BF16), pltpu.VMEM((f, d), BF16),
            pltpu.VMEM((2, d, FFN_CHUNK), F32), pltpu.VMEM((2, d, FFN_CHUNK), F32),
            pltpu.VMEM((2, FFN_CHUNK, d), F32), pltpu.SemaphoreType.DMA((3, 2))]


def _tile_index(t, n_tiles, per_batch):
    tt = jnp.minimum(t, n_tiles - 1)
    return tt // per_batch, tt % per_batch


def _pool_steps(i, x_ref, g_ref, win_ref, wgrp_ref, scale_ref, wout_ref, ext_ref, x1_ref, *, ts, halo):
    d_model = x_ref.shape[-1]
    gdim = d_model // len(POOL_WINDOWS)
    h = _rmsnorm(x_ref[...], g_ref[...]).astype(BF16)
    ext_ref[halo:halo + ts, :] = jnp.dot(h, win_ref[...], preferred_element_type=F32)
    yield
    pos = i * ts + lax.broadcasted_iota(jnp.int32, (ts, 1), 0)
    zs = []
    for g, w in enumerate(POOL_WINDOWS):
        cols = slice(g * gdim, (g + 1) * gdim)
        s = ext_ref[:, cols]
        k = 1
        while k < w:
            s = s + pltpu.roll(s, k, axis=0)
            k *= 2
        inv_cnt = 1.0 / jnp.minimum(pos + 1, w).astype(F32)
        p = s[halo:, :] * inv_cnt - ext_ref[halo:halo + ts, cols]
        zs.append(jnp.dot(p.astype(BF16), wgrp_ref[g], preferred_element_type=F32))
    yield
    z = jnp.concatenate(zs, axis=1) * scale_ref[...]
    y = jnp.dot(z.astype(BF16), wout_ref[...], preferred_element_type=F32)
    ext_ref[0:halo, :] = ext_ref[ts:ts + halo, :]
    x1_ref[...] = x_ref[...] + y
    yield


def _pool_ffn_kernel(x_ref, g_ref, win_ref, wgrp_ref, scale_ref, wout_ref, gf_ref, *refs,
                     layer, ts, halo, n_tiles, per_batch):
    hbm_refs, (o_ref, ext_ref, x1_ref) = refs[:3], refs[3:6]
    (wg_ref, wu_ref, wd_ref), stage_refs, sem = refs[6:9], refs[9:12], refs[12]
    t = pl.program_id(0)
    i = jnp.minimum(t, n_tiles - 1) % per_batch

    @pl.when(i == 0)
    def _():
        ext_ref[0:halo, :] = jnp.zeros((halo, x_ref.shape[-1]), F32)

    make_mixer = functools.partial(_pool_steps, i, x_ref, g_ref, win_ref, wgrp_ref, scale_ref, wout_ref,
                                   ext_ref, x1_ref, ts=ts, halo=halo)
    make_ffn = functools.partial(_swiglu_steps, x1_ref, o_ref, gf_ref, wg_ref, wu_ref, wd_ref, lambda acc: acc)
    load_weights = functools.partial(_load_ffn_weights, layer, hbm_refs, (wg_ref, wu_ref, wd_ref), stage_refs, sem)
    _skewed_step(t, n_tiles, make_mixer, make_ffn, load_weights)


def _pool_ffn_layer(x, g_mix, w_in, w_group, scale, w_out, g_ffn, w_gate, w_up, w_down, *, layer, ts=TOKEN_TILE):
    b, s, d = x.shape
    halo = max(POOL_WINDOWS)
    per_batch = s // ts
    n_tiles = b * per_batch
    kern = functools.partial(_pool_ffn_kernel, layer=layer, ts=ts, halo=halo, n_tiles=n_tiles, per_batch=per_batch)
    mixer_tile = pl.BlockSpec((None, ts, d), lambda t: (*_tile_index(t, n_tiles, per_batch), 0))
    ffn_tile = pl.BlockSpec((None, ts, d), lambda t: (*_tile_index(jnp.maximum(t - 1, 0), n_tiles, per_batch), 0))
    return pl.pallas_call(
        kern,
        grid=(n_tiles + 1,),
        in_specs=[mixer_tile, _const_spec((1, d)), _layer_spec(w_in, layer), _layer_spec(w_group, layer),
                  _const_spec((1, d)), _layer_spec(w_out, layer)] + _ffn_specs(d),
        out_specs=ffn_tile,
        out_shape=jax.ShapeDtypeStruct(x.shape, F32),
        scratch_shapes=[pltpu.VMEM((ts + halo, d), F32), pltpu.VMEM((ts, d), F32)]
        + _ffn_scratch(d, w_gate.shape[-1]),
        compiler_params=_params("arbitrary"),
        name="pool_ffn",
    )(x, g_mix.reshape(1, d), w_in, w_group, scale.reshape(1, d), w_out, g_ffn.reshape(1, d), w_gate, w_up, w_down)


def _class_order(d):
    order, cur = (0,), 1
    while cur < d:
        order = tuple(c + cur * r for c in order for r in range(DEINT_STRIDE))
        cur *= DEINT_STRIDE
    assert cur == d, "dilations must be powers of DEINT_STRIDE"
    return order


def _split_rows(src, n_blocks, blk):
    n = blk // DEINT_STRIDE
    return jnp.concatenate([src[pl.ds(b0 * blk + r, n, stride=DEINT_STRIDE), :]
                            for b0 in range(n_blocks) for r in range(DEINT_STRIDE)], axis=0)


def _qkv_kernel(x_ref, g_ref, cos_ref, sin_ref, *refs, ts, sub, groups):
    ng = len(groups)
    w_refs, out_refs = refs[:ng], refs[ng:-2]
    hs_all, tab_all = refs[-2:]
    lane = lax.broadcasted_iota(jnp.int32, (1, LANES), 1)
    first_half = (lane % HEAD_DIM) < (HEAD_DIM // 2)
    n_chunks = x_ref.shape[-1] // LANES
    for si, s0 in enumerate(range(0, ts, sub)):
        hs_ref, tab_ref = hs_all.at[si], tab_all.at[si]
        h = _rmsnorm(x_ref[s0:s0 + sub, :], g_ref[...])
        cs, sn = cos_ref[s0:s0 + sub, :], sin_ref[s0:s0 + sub, :]
        cur_d = 1
        for gi, (d, cg) in enumerate(groups):
            while cur_d < d:
                for c in range(n_chunks):
                    hs_ref[c] = h[:, c * LANES:(c + 1) * LANES]
                tab_ref[0], tab_ref[1] = cs, sn
                blk = sub // cur_d
                h = jnp.concatenate([_split_rows(hs_ref.at[c], cur_d, blk) for c in range(n_chunks)], axis=1)
                cs, sn = _split_rows(tab_ref.at[0], cur_d, blk), _split_rows(tab_ref.at[1], cur_d, blk)
                cur_d *= DEINT_STRIDE
            assert cur_d == d, "head groups must come in increasing powers of DEINT_STRIDE"
            qkv = jnp.dot(h.astype(BF16), w_refs[gi][...], preferred_element_type=F32)
            chunks = []
            for c in range(2 * cg // LANES):
                t = qkv[:, c * LANES:(c + 1) * LANES]
                swapped = jnp.where(first_half, pltpu.roll(t, LANES - HEAD_DIM // 2, axis=1),
                                    pltpu.roll(t, HEAD_DIM // 2, axis=1))
                chunks.append(t * cs + swapped * sn)
            qk = jnp.concatenate(chunks, axis=1)
            parts = (qk[:, :cg], qk[:, cg:2 * cg], qkv[:, 2 * cg:])
            n = sub // d
            for part, o_ref in zip(parts, out_refs[3 * gi:3 * gi + 3]):
                pb = part.astype(BF16)
                for slot, r in enumerate(_class_order(d)):
                    o_ref[r, si * n:(si + 1) * n, :] = pb[slot * n:(slot + 1) * n, :]


def _qkv_layer(x, g, cos, sin, w_groups, groups, *, ts=QKV_TILE, sub=QKV_SUB_TILE):
    b, s, d_model = x.shape
    kern = functools.partial(_qkv_kernel, ts=ts, sub=sub, groups=groups)
    out_shapes, out_specs = [], []
    for d, cg in groups:
        for _ in range(3):
            out_shapes.append(jax.ShapeDtypeStruct((b, d, s // d, cg), BF16))
            out_specs.append(pl.BlockSpec((None, d, ts // d, cg), lambda bi, i: (bi, 0, i, 0)))
    table_spec = pl.BlockSpec((ts, LANES), lambda bi, i: (i, 0))
    return pl.pallas_call(
        kern,
        grid=(b, s // ts),
        in_specs=[pl.BlockSpec((None, ts, d_model), lambda bi, i: (bi, i, 0)), _const_spec((1, d_model)),
                  table_spec, table_spec] + [_const_spec(w.shape) for w in w_groups],
        out_specs=out_specs,
        out_shape=out_shapes,
        scratch_shapes=[pltpu.VMEM((ts // sub, d_model // LANES, sub, LANES), F32),
                        pltpu.VMEM((ts // sub, 2, sub, LANES), F32)],
        compiler_params=_params("arbitrary", "arbitrary"),
        name="qkv_rope",
    )(x, g.reshape(1, d_model), cos, sin, *w_groups)


def _band_masks(w, has_prev):
    qi = lax.broadcasted_iota(jnp.int32, (w, 2 * w), 0)
    kj = lax.broadcasted_iota(jnp.int32, (w, 2 * w), 1)
    band = jnp.logical_and(kj >= qi, kj <= qi + w)
    return band, jnp.logical_and(band, jnp.logical_or(kj >= w, has_prev))


def _attn_segment(masks, q_ref, k_ref, v_ref, kp_ref, vp_ref, o_ref, lse_ref, lse_scr, *, w, n_heads):
    band, band_first = masks
    tq = q_ref.shape[0]
    cg = n_heads * HEAD_DIM
    nsub = tq // w
    lane = lax.broadcasted_iota(jnp.int32, (1, LANES), 1)
    nt = (((1,), (1,)), ((), ()))

    units = []
    for c0 in range(0, cg, LANES):
        width = min(LANES, cg - c0)
        for hh in range(width // HEAD_DIM):
            if width == LANES:
                sel = (lane < HEAD_DIM) if hh == 0 else (lane >= HEAD_DIM)
            else:
                sel = None
            units.append((len(units), slice(c0, c0 + width), hh, sel))

    def key_rows(ref, prev_ref, cols, j):
        if j == 0:
            return jnp.concatenate([prev_ref[:, cols], ref[0:w, cols]], axis=0)
        return ref[(j - 1) * w:(j + 1) * w, cols]

    def scores(item):
        (_, cols, _, sel), j = item
        qb = q_ref[j * w:(j + 1) * w, cols]
        if sel is not None:
            qb = jnp.where(sel, qb, jnp.zeros_like(qb))
        return lax.dot_general(qb, key_rows(k_ref, kp_ref, cols, j), nt, preferred_element_type=F32)

    def finish(item, first, last, sc):
        (h, cols, hh, _), j = item
        if first:
            lse_scr[0:tq, :] = jnp.full((tq, LANES), NEG_INF, F32)
        lo = hh * HEAD_DIM
        rows = slice(j * w, (j + 1) * w)
        s = jnp.where(band_first if j == 0 else band, sc, NEG_INF)
        m = jnp.max(s, axis=-1, keepdims=True)
        e = jnp.exp(s - m)
        den = jnp.sum(e, axis=-1, keepdims=True)
        pv = jnp.dot(e.astype(BF16), key_rows(v_ref, vp_ref, cols, j), preferred_element_type=F32) / den
        o_ref[rows, cols.start + lo:cols.start + lo + HEAD_DIM] = pv[:, lo:lo + HEAD_DIM].astype(BF16)
        lse_scr[rows, :] = jnp.where(lane == h, m + jnp.log(den), lse_scr[rows, :])
        if last:
            lse = lse_scr[0:tq, :]
            mx = jnp.max(lse, axis=-1, keepdims=True)
            tot = jnp.sum(jnp.exp(lse - mx), axis=-1, keepdims=True)
            lse_ref[...] = jnp.broadcast_to(mx + jnp.log(tot) - math.log(n_heads), (tq, LANES))

    items = [(unit, j) for unit in units for j in range(nsub)]
    return [(functools.partial(scores, item), functools.partial(finish, item, n == 0, n == len(items) - 1))
            for n, item in enumerate(items)]


def _run_lookahead(work):
    pending = [scores() for scores, _ in work[:ATTN_LOOKAHEAD]]
    for n, (_, finish) in enumerate(work):
        if n + ATTN_LOOKAHEAD < len(work):
            pending.append(work[n + ATTN_LOOKAHEAD][0]())
        finish(pending.pop(0))


def _attn_kernel(q_ref, k_ref, v_ref, kp_ref, vp_ref, o_ref, lse_ref, lse_scr, *, w, n_heads):
    masks = _band_masks(w, pl.program_id(1) > 0)
    work = []
    for sq in range(q_ref.shape[0]):
        work += _attn_segment(masks, q_ref.at[sq], k_ref.at[sq], v_ref.at[sq], kp_ref.at[sq], vp_ref.at[sq],
                              o_ref.at[sq], lse_ref.at[sq], lse_scr.at[sq], w=w, n_heads=n_heads)
    _run_lookahead(work)


def _attn_group(q, k, v, *, w, n_heads, rows=ATTN_STEP_ROWS):
    nseq, l, cg = q.shape
    tq = min(rows, l)
    nb = min(rows // tq, ATTN_STEP_SEQS)
    ratio = tq // w
    kern = functools.partial(_attn_kernel, w=w, n_heads=n_heads)
    cur = pl.BlockSpec((nb, tq, cg), lambda n, i: (n, i, 0))
    prev = pl.BlockSpec((nb, w, cg), lambda n, i: (n, jnp.maximum(i * ratio - 1, 0), 0))
    return pl.pallas_call(
        kern,
        grid=(nseq // nb, l // tq),
        in_specs=[cur, cur, cur, prev, prev],
        out_specs=[cur, pl.BlockSpec((nb, tq, LANES), lambda n, i: (n, i, 0))],
        out_shape=[jax.ShapeDtypeStruct((nseq, l, cg), BF16),
                   jax.ShapeDtypeStruct((nseq, l, LANES), F32)],
        scratch_shapes=[pltpu.VMEM((nb, tq, LANES), F32)],
        compiler_params=_params("arbitrary", "arbitrary"),
        name=f"attn_h{n_heads}_l{l}",
    )(q, k, v, k, v)


def _lane_chunks(blk, cg):
    chunks = []
    for c0 in range(0, cg, LANES):
        piece = blk[:, c0:min(c0 + LANES, cg)]
        if piece.shape[1] < LANES:
            piece = jnp.concatenate([piece, jnp.zeros((blk.shape[0], LANES - piece.shape[1]), F32)], axis=1)
        chunks.append(piece)
    return chunks


def _merge_steps(x_ref, o_refs, l_refs, wout_ref, scratch, x1_ref, *, ts, groups):
    ng = len(groups)
    outs, lses = [], []
    si = 0
    for gi, (d, cg) in enumerate(groups):
        if d == 1:
            outs.append(_lane_chunks(o_refs[gi][0].astype(F32), cg))
            lses.append(l_refs[gi][0])
            continue
        n = ts // d
        o_scr, l_scr = scratch[si], scratch[si + 1]
        si += 2
        for r in range(d):
            for c, piece in enumerate(_lane_chunks(o_refs[gi][r].astype(F32), cg)):
                o_scr[c, pl.ds(r, n, stride=d), :] = piece
            l_scr[pl.ds(r, n, stride=d), :] = l_refs[gi][r]
        outs.append([o_scr[c] for c in range(o_scr.shape[0])])
        lses.append(l_scr[...])
    m = functools.reduce(jnp.maximum, lses)
    es = [jnp.exp(l - m) for l in lses]
    inv = float(ng) / functools.reduce(lambda a, b2: a + b2, es)
    lane = lax.broadcasted_iota(jnp.int32, (1, LANES), 1)
    full, half = [], None
    for (d, cg), chunks, e in zip(groups, outs, es):
        alpha = (e * inv)[:, :1]
        for c, chunk in enumerate(chunks):
            chunk = chunk * alpha
            whole = cg - c * LANES >= LANES
            if half is None:
                if whole:
                    full.append(chunk)
                else:
                    half = chunk
            else:
                rot = pltpu.roll(chunk, HEAD_DIM, axis=1)
                full.append(jnp.where(lane < HEAD_DIM, half, rot))
                half = rot if whole else None
    assert half is None, "attention width must be a multiple of 128"
    merged = jnp.concatenate(full, axis=1).astype(BF16)
    yield
    x1_ref[...] = x_ref[...] + jnp.dot(merged, wout_ref[...], preferred_element_type=F32)
    yield


def _merge_ffn_kernel(x_ref, *refs, layer, ts, groups, n_tiles):
    ng = len(groups)
    o_refs, l_refs = refs[:ng], refs[ng:2 * ng]
    wout_ref, gf_ref = refs[2 * ng:2 * ng + 2]
    hbm_refs = refs[2 * ng + 2:2 * ng + 5]
    gfin_ref, y_ref, x1_ref = refs[2 * ng + 5:2 * ng + 8]
    (wg_ref, wu_ref, wd_ref), stage_refs, sem = refs[2 * ng + 8:2 * ng + 11], refs[2 * ng + 11:2 * ng + 14], refs[2 * ng + 14]
    scratch = refs[2 * ng + 15:]
    t = pl.program_id(0)
    make_mixer = functools.partial(_merge_steps, x_ref, o_refs, l_refs, wout_ref, scratch, x1_ref,
                                   ts=ts, groups=groups)
    make_ffn = functools.partial(_swiglu_steps, x1_ref, y_ref, gf_ref, wg_ref, wu_ref, wd_ref,
                                 lambda acc: _rmsnorm(acc, gfin_ref[...]))
    load_weights = functools.partial(_load_ffn_weights, layer, hbm_refs, (wg_ref, wu_ref, wd_ref), stage_refs, sem)
    _skewed_step(t, n_tiles, make_mixer, make_ffn, load_weights)


def _merge_ffn_layer(x, outs, lses, w_out, g_ffn, w_gate, w_up, w_down, g_final, groups, *,
                     attn_layer, layer, ts=TOKEN_TILE):
    b, s, d_model = x.shape
    per_batch = s // ts
    n_tiles = b * per_batch
    kern = functools.partial(_merge_ffn_kernel, layer=layer, ts=ts, groups=groups, n_tiles=n_tiles)

    def mixer_spec(block):
        lead = len(block) - 2
        return pl.BlockSpec(block, lambda t: (_tile_index(t, n_tiles, per_batch)[0],) + (0,) * (lead - 1)
                            + (_tile_index(t, n_tiles, per_batch)[1], 0))

    ffn_tile = pl.BlockSpec((None, ts, d_model),
                            lambda t: (*_tile_index(jnp.maximum(t - 1, 0), n_tiles, per_batch), 0))
    o_specs = [mixer_spec((None, d, ts // d, cg)) for d, cg in groups]
    l_specs = [mixer_spec((None, d, ts // d, LANES)) for d, cg in groups]
    scratch = [pltpu.VMEM((ts, d_model), F32)] + _ffn_scratch(d_model, w_gate.shape[-1])
    for d, cg in groups:
        if d > 1:
            scratch += [pltpu.VMEM((pl.cdiv(cg, LANES), ts, LANES), F32), pltpu.VMEM((ts, LANES), F32)]
    return pl.pallas_call(
        kern,
        grid=(n_tiles + 1,),
        in_specs=[mixer_spec((None, ts, d_model))] + o_specs + l_specs + [_layer_spec(w_out, attn_layer)]
        + _ffn_specs(d_model) + [_const_spec((1, d_model))],
        out_specs=ffn_tile,
        out_shape=jax.ShapeDtypeStruct(x.shape, F32),
        scratch_shapes=scratch,
        compiler_params=_params("arbitrary"),
        name="attn_merge_ffn_final",
    )(x, *outs, *lses, w_out, g_ffn.reshape(1, d_model), w_gate, w_up, w_down, g_final.reshape(1, d_model))


def _rope_tables(s):
    half = HEAD_DIM // 2
    inv_freq = 1.0 / (ROPE_THETA ** (np.arange(0, HEAD_DIM, 2, dtype=np.float64) / HEAD_DIM))
    ang = np.arange(s, dtype=np.float64)[:, None] * inv_freq[None, :]
    cos, sin = np.cos(ang), np.sin(ang)
    reps = LANES // half
    cos_t = np.concatenate([cos] * reps, axis=-1)
    sin_t = np.concatenate([-sin, sin] * (reps // 2), axis=-1)
    return jnp.asarray(cos_t, F32), jnp.asarray(sin_t, F32)


def kernel(x, norm_mix, norm_ffn, norm_final, pool_w_in, pool_w_group, pool_scale, pool_w_out,
           attn_w_qkv, attn_w_out, ffn_w_gate, ffn_w_up, ffn_w_down):
    b, s, d_model = x.shape
    assert norm_mix.shape[0] == 2, "two layers: pooling mixer then dilated attention"
    n_heads = attn_w_out.shape[1] // HEAD_DIM
    d_attn = n_heads * HEAD_DIM
    head_groups = _head_groups(n_heads)
    groups = tuple((dil, nh * HEAD_DIM) for (_, dil), nh in zip(ATTN_PATTERNS, head_groups))
    bf = lambda t: t.astype(BF16)

    x = _pool_ffn_layer(x, norm_mix[0], bf(pool_w_in), bf(pool_w_group), pool_scale[0], bf(pool_w_out),
                        norm_ffn[0], ffn_w_gate, ffn_w_up, ffn_w_down, layer=0)

    w_qkv = attn_w_qkv[0]
    part_scale = (HEAD_DIM ** -0.5, 1.0, 1.0)
    w_groups, col = [], 0
    for _, cg in groups:
        w_groups.append(bf(jnp.concatenate(
            [w_qkv[:, p * d_attn + col:p * d_attn + col + cg] * part_scale[p] for p in range(3)], axis=1)))
        col += cg
    cos_t, sin_t = _rope_tables(s)
    qkv = _qkv_layer(x, norm_mix[1], cos_t, sin_t, w_groups, groups)
    outs, lses = [], []
    for gi, ((window, dil), nh) in enumerate(zip(ATTN_PATTERNS, head_groups)):
        cg = nh * HEAD_DIM
        q, k, v = (t.reshape(b * dil, s // dil, cg) for t in qkv[3 * gi:3 * gi + 3])
        o, lse = _attn_group(q, k, v, w=window // dil, n_heads=nh)
        outs.append(o.reshape(b, dil, s // dil, cg))
        lses.append(lse.reshape(b, dil, s // dil, LANES))
    return _merge_ffn_layer(x, outs, lses, bf(attn_w_out), norm_ffn[1], ffn_w_gate, ffn_w_up, ffn_w_down,
                            norm_final, groups, attn_layer=0, layer=1)
```

```python
import functools
import math

import jax
import jax.numpy as jnp
import numpy as np
from jax import lax
from jax.experimental import pallas as pl
from jax.experimental.pallas import tpu as pltpu

EPS = 1e-6
POOL_WINDOWS = (2, 4, 8, 16)
HEAD_DIM = 64
ATTN_PATTERNS = ((128, 1), (512, 4), (2048, 16))
ROPE_THETA = 10000.0
NEG_INF = -1e30

LANES = 128
VMEM_LIMIT_BYTES = 58 * 1024 * 1024
TOKEN_TILE = 512
QKV_TILE = 1024
QKV_SUB_TILE = 256
ATTN_STEP_ROWS = 2048
ATTN_STEP_SEQS = 2
ATTN_LOOKAHEAD = 2
FFN_CHUNK = 256
MIXER_GAP = 3
DEINT_STRIDE = 4

F32 = jnp.float32
BF16 = jnp.bfloat16


def _head_groups(n_heads):
    n = len(ATTN_PATTERNS)
    return tuple(n_heads // n + (1 if g < n_heads % n else 0) for g in range(n))


def _rmsnorm(x, g):
    ms = jnp.mean(x * x, axis=-1, keepdims=True)
    return x * lax.rsqrt(ms + EPS) * g


def _params(*sem):
    return pltpu.CompilerParams(dimension_semantics=sem, vmem_limit_bytes=VMEM_LIMIT_BYTES)


def _const_spec(shape):
    zeros = (0,) * len(shape)
    return pl.BlockSpec(shape, lambda *_: zeros, pipeline_mode=pl.Buffered(1))


def _layer_spec(stacked, layer):
    idx = (layer,) + (0,) * (stacked.ndim - 1)
    return pl.BlockSpec((None,) + stacked.shape[1:], lambda *_: idx, pipeline_mode=pl.Buffered(1))


def _swiglu_steps(x1_ref, o_ref, g_ref, wg_ref, wu_ref, wd_ref, finish):
    x1 = x1_ref[...]
    h = _rmsnorm(x1, g_ref[...]).astype(BF16)
    acc = x1
    n_chunks = wg_ref.shape[-1] // FFN_CHUNK
    for c in range(n_chunks):
        sl = slice(c * FFN_CHUNK, (c + 1) * FFN_CHUNK)
        gate = jnp.dot(h, wg_ref[:, sl], preferred_element_type=F32)
        up = jnp.dot(h, wu_ref[:, sl], preferred_element_type=F32)
        a = (gate * jax.nn.sigmoid(gate) * up).astype(BF16)
        acc = acc + jnp.dot(a, wd_ref[sl, :], preferred_element_type=F32)
        if c == n_chunks - 1:
            o_ref[...] = finish(acc)
        yield


def _load_ffn_weights(layer, hbm_refs, bf_refs, stage_refs, sem, meanwhile):
    n_chunks = hbm_refs[0].shape[-1] // FFN_CHUNK

    def chunk(c):
        return slice(c * FFN_CHUNK, (c + 1) * FFN_CHUNK)

    def copies(c):
        srcs = (hbm_refs[0].at[layer, :, chunk(c)], hbm_refs[1].at[layer, :, chunk(c)],
                hbm_refs[2].at[layer, chunk(c), :])
        return [pltpu.make_async_copy(src, stage.at[c % 2], sem.at[k, c % 2])
                for k, (src, stage) in enumerate(zip(srcs, stage_refs))]

    for cp in copies(0):
        cp.start()
    meanwhile()
    for c in range(n_chunks):
        if c + 1 < n_chunks:
            for cp in copies(c + 1):
                cp.start()
        for cp in copies(c):
            cp.wait()
        bf_refs[0][:, chunk(c)] = stage_refs[0][c % 2].astype(BF16)
        bf_refs[1][:, chunk(c)] = stage_refs[1][c % 2].astype(BF16)
        bf_refs[2][chunk(c), :] = stage_refs[2][c % 2].astype(BF16)


def _skewed_step(t, n_tiles, make_mixer, make_ffn, load_weights):
    @pl.when(t == 0)
    def _():
        def run_mixer():
            for _ in make_mixer():
                pass
        load_weights(run_mixer)

    @pl.when(t == n_tiles)
    def _():
        for _ in make_ffn():
            pass

    @pl.when(jnp.logical_and(t > 0, t < n_tiles))
    def _():
        mixer = make_mixer()
        for c, _ in enumerate(make_ffn()):
            if c % MIXER_GAP == MIXER_GAP - 1:
                next(mixer, None)
        for _ in mixer:
            pass


def _ffn_specs(d):
    hbm = pl.BlockSpec(memory_space=pl.ANY)
    return [_const_spec((1, d)), hbm, hbm, hbm]


def _ffn_scratch(d, f):
    return [pltpu.VMEM((d, f), BF16), pltpu.VMEM((d, f), BF16), pltpu.VMEM((f, d), BF16),
            pltpu.VMEM((2, d, FFN_CHUNK), F32), pltpu.VMEM((2, d, FFN_CHUNK), F32),
            pltpu.VMEM((2, FFN_CHUNK, d), F32), pltpu.SemaphoreType.DMA((3, 2))]


def _tile_index(t, n_tiles, per_batch):
    tt = jnp.minimum(t, n_tiles - 1)
    return tt // per_batch, tt % per_batch


def _pool_steps(i, x_ref, g_ref, win_ref, wgrp_ref, scale_ref, wout_ref, ext_ref, x1_ref, *, ts, halo):
    d_model = x_ref.shape[-1]
    gdim = d_model // len(POOL_WINDOWS)
    h = _rmsnorm(x_ref[...], g_ref[...]).astype(BF16)
    ext_ref[halo:halo + ts, :] = jnp.dot(h, win_ref[...].astype(BF16), preferred_element_type=F32)
    yield
    pos = i * ts + lax.broadcasted_iota(jnp.int32, (ts, 1), 0)
    zs = []
    for g, w in enumerate(POOL_WINDOWS):
        cols = slice(g * gdim, (g + 1) * gdim)
        s = ext_ref[:, cols]
        k = 1
        while k < w:
            s = s + pltpu.roll(s, k, axis=0)
            k *= 2
        inv_cnt = 1.0 / jnp.minimum(pos + 1, w).astype(F32)
        p = s[halo:, :] * inv_cnt - ext_ref[halo:halo + ts, cols]
        zs.append(jnp.dot(p.astype(BF16), wgrp_ref[g].astype(BF16), preferred_element_type=F32))
    yield
    z = jnp.concatenate(zs, axis=1) * scale_ref[...]
    y = jnp.dot(z.astype(BF16), wout_ref[...].astype(BF16), preferred_element_type=F32)
    ext_ref[0:halo, :] = ext_ref[ts:ts + halo, :]
    x1_ref[...] = x_ref[...] + y
    yield


def _pool_ffn_kernel(x_ref, g_ref, win_ref, wgrp_ref, scale_ref, wout_ref, gf_ref, *refs,
                     layer, ts, halo, n_tiles, per_batch):
    hbm_refs, (o_ref, ext_ref, x1_ref) = refs[:3], refs[3:6]
    (wg_ref, wu_ref, wd_ref), stage_refs, sem = refs[6:9], refs[9:12], refs[12]
    t = pl.program_id(0)
    i = jnp.minimum(t, n_tiles - 1) % per_batch

    @pl.when(i == 0)
    def _():
        ext_ref[0:halo, :] = jnp.zeros((halo, x_ref.shape[-1]), F32)

    make_mixer = functools.partial(_pool_steps, i, x_ref, g_ref, win_ref, wgrp_ref, scale_ref, wout_ref,
                                   ext_ref, x1_ref, ts=ts, halo=halo)
    make_ffn = functools.partial(_swiglu_steps, x1_ref, o_ref, gf_ref, wg_ref, wu_ref, wd_ref, lambda acc: acc)
    load_weights = functools.partial(_load_ffn_weights, layer, hbm_refs, (wg_ref, wu_ref, wd_ref), stage_refs, sem)
    _skewed_step(t, n_tiles, make_mixer, make_ffn, load_weights)


def _pool_ffn_layer(x, g_mix, w_in, w_group, scale, w_out, g_ffn, w_gate, w_up, w_down, *, layer, ts=TOKEN_TILE):
    b, s, d = x.shape
    halo = max(POOL_WINDOWS)
    per_batch = s // ts
    n_tiles = b * per_batch
    kern = functools.partial(_pool_ffn_kernel, layer=layer, ts=ts, halo=halo, n_tiles=n_tiles, per_batch=per_batch)
    mixer_tile = pl.BlockSpec((None, ts, d), lambda t: (*_tile_index(t, n_tiles, per_batch), 0))
    ffn_tile = pl.BlockSpec((None, ts, d), lambda t: (*_tile_index(jnp.maximum(t - 1, 0), n_tiles, per_batch), 0))
    return pl.pallas_call(
        kern,
        grid=(n_tiles + 1,),
        in_specs=[mixer_tile, _const_spec((1, d)), _layer_spec(w_in, layer), _layer_spec(w_group, layer),
                  _const_spec((1, d)), _layer_spec(w_out, layer)] + _ffn_specs(d),
        out_specs=ffn_tile,
        out_shape=jax.ShapeDtypeStruct(x.shape, F32),
        scratch_shapes=[pltpu.VMEM((ts + halo, d), F32), pltpu.VMEM((ts, d), F32)]
        + _ffn_scratch(d, w_gate.shape[-1]),
        compiler_params=_params("arbitrary"),
        name="pool_ffn",
    )(x, g_mix.reshape(1, d), w_in, w_group, scale.reshape(1, d), w_out, g_ffn.reshape(1, d), w_gate, w_up, w_down)


def _class_order(d):
    order, cur = (0,), 1
    while cur < d:
        order = tuple(c + cur * r for c in order for r in range(DEINT_STRIDE))
        cur *= DEINT_STRIDE
    assert cur == d, "dilations must be powers of DEINT_STRIDE"
    return order


def _split_rows(src, n_blocks, blk):
    n = blk // DEINT_STRIDE
    return jnp.concatenate([src[pl.ds(b0 * blk + r, n, stride=DEINT_STRIDE), :]
                            for b0 in range(n_blocks) for r in range(DEINT_STRIDE)], axis=0)


def _qkv_kernel(x_ref, g_ref, cos_ref, sin_ref, *refs, ts, sub, groups):
    ng = len(groups)
    w_refs, out_refs = refs[:ng], refs[ng:-2]
    hs_all, tab_all = refs[-2:]
    lane = lax.broadcasted_iota(jnp.int32, (1, LANES), 1)
    first_half = (lane % HEAD_DIM) < (HEAD_DIM // 2)
    n_chunks = x_ref.shape[-1] // LANES
    for si, s0 in enumerate(range(0, ts, sub)):
        hs_ref, tab_ref = hs_all.at[si], tab_all.at[si]
        h = _rmsnorm(x_ref[s0:s0 + sub, :], g_ref[...])
        cs, sn = cos_ref[s0:s0 + sub, :], sin_ref[s0:s0 + sub, :]
        cur_d = 1
        for gi, (d, cg) in enumerate(groups):
            while cur_d < d:
                for c in range(n_chunks):
                    hs_ref[c] = h[:, c * LANES:(c + 1) * LANES]
                tab_ref[0], tab_ref[1] = cs, sn
                blk = sub // cur_d
                h = jnp.concatenate([_split_rows(hs_ref.at[c], cur_d, blk) for c in range(n_chunks)], axis=1)
                cs, sn = _split_rows(tab_ref.at[0], cur_d, blk), _split_rows(tab_ref.at[1], cur_d, blk)
                cur_d *= DEINT_STRIDE
            assert cur_d == d, "head groups must come in increasing powers of DEINT_STRIDE"
            qkv = jnp.dot(h.astype(BF16), w_refs[gi][...], preferred_element_type=F32)
            chunks = []
            for c in range(2 * cg // LANES):
                t = qkv[:, c * LANES:(c + 1) * LANES]
                swapped = jnp.where(first_half, pltpu.roll(t, LANES - HEAD_DIM // 2, axis=1),
                                    pltpu.roll(t, HEAD_DIM // 2, axis=1))
                chunks.append(t * cs + swapped * sn)
            qk = jnp.concatenate(chunks, axis=1)
            parts = (qk[:, :cg], qk[:, cg:2 * cg], qkv[:, 2 * cg:])
            n = sub // d
            for part, o_ref in zip(parts, out_refs[3 * gi:3 * gi + 3]):
                pb = part.astype(BF16)
                for slot, r in enumerate(_class_order(d)):
                    o_ref[r, si * n:(si + 1) * n, :] = pb[slot * n:(slot + 1) * n, :]


def _qkv_layer(x, g, cos, sin, w_groups, groups, *, ts=QKV_TILE, sub=QKV_SUB_TILE):
    b, s, d_model = x.shape
    kern = functools.partial(_qkv_kernel, ts=ts, sub=sub, groups=groups)
    out_shapes, out_specs = [], []
    for d, cg in groups:
        for _ in range(3):
            out_shapes.append(jax.ShapeDtypeStruct((b, d, s // d, cg), BF16))
            out_specs.append(pl.BlockSpec((None, d, ts // d, cg), lambda bi, i: (bi, 0, i, 0)))
    table_spec = pl.BlockSpec((ts, LANES), lambda bi, i: (i, 0))
    return pl.pallas_call(
        kern,
        grid=(b, s // ts),
        in_specs=[pl.BlockSpec((None, ts, d_model), lambda bi, i: (bi, i, 0)), _const_spec((1, d_model)),
                  table_spec, table_spec] + [_const_spec(w.shape) for w in w_groups],
        out_specs=out_specs,
        out_shape=out_shapes,
        scratch_shapes=[pltpu.VMEM((ts // sub, d_model // LANES, sub, LANES), F32),
                        pltpu.VMEM((ts // sub, 2, sub, LANES), F32)],
        compiler_params=_params("arbitrary", "arbitrary"),
        name="qkv_rope",
    )(x, g.reshape(1, d_model), cos, sin, *w_groups)


def _band_masks(w, has_prev):
    qi = lax.broadcasted_iota(jnp.int32, (w, 2 * w), 0)
    kj = lax.broadcasted_iota(jnp.int32, (w, 2 * w), 1)
    band = jnp.logical_and(kj >= qi, kj <= qi + w)
    return band, jnp.logical_and(band, jnp.logical_or(kj >= w, has_prev))


def _attn_segment(masks, q_ref, k_ref, v_ref, kp_ref, vp_ref, o_ref, lse_ref, lse_scr, *, w, n_heads):
    band, band_first = masks
    tq = q_ref.shape[0]
    cg = n_heads * HEAD_DIM
    nsub = tq // w
    lane = lax.broadcasted_iota(jnp.int32, (1, LANES), 1)
    nt = (((1,), (1,)), ((), ()))

    units = []
    for c0 in range(0, cg, LANES):
        width = min(LANES, cg - c0)
        for hh in range(width // HEAD_DIM):
            if width == LANES:
                sel = (lane < HEAD_DIM) if hh == 0 else (lane >= HEAD_DIM)
            else:
                sel = None
            units.append((len(units), slice(c0, c0 + width), hh, sel))

    def key_rows(ref, prev_ref, cols, j):
        if j == 0:
            return jnp.concatenate([prev_ref[:, cols], ref[0:w, cols]], axis=0)
        return ref[(j - 1) * w:(j + 1) * w, cols]

    def scores(item):
        (_, cols, _, sel), j = item
        qb = q_ref[j * w:(j + 1) * w, cols]
        if sel is not None:
            qb = jnp.where(sel, qb, jnp.zeros_like(qb))
        return lax.dot_general(qb, key_rows(k_ref, kp_ref, cols, j), nt, preferred_element_type=F32)

    def finish(item, first, last, sc):
        (h, cols, hh, _), j = item
        if first:
            lse_scr[0:tq, :] = jnp.full((tq, LANES), NEG_INF, F32)
        lo = hh * HEAD_DIM
        rows = slice(j * w, (j + 1) * w)
        s = jnp.where(band_first if j == 0 else band, sc, NEG_INF)
        m = jnp.max(s, axis=-1, keepdims=True)
        e = jnp.exp(s - m)
        den = jnp.sum(e, axis=-1, keepdims=True)
        pv = jnp.dot(e.astype(BF16), key_rows(v_ref, vp_ref, cols, j), preferred_element_type=F32) / den
        o_ref[rows, cols.start + lo:cols.start + lo + HEAD_DIM] = pv[:, lo:lo + HEAD_DIM].astype(BF16)
        lse_scr[rows, :] = jnp.where(lane == h, m + jnp.log(den), lse_scr[rows, :])
        if last:
            lse = lse_scr[0:tq, :]
            mx = jnp.max(lse, axis=-1, keepdims=True)
            tot = jnp.sum(jnp.exp(lse - mx), axis=-1, keepdims=True)
            lse_ref[...] = jnp.broadcast_to(mx + jnp.log(tot) - math.log(n_heads), (tq, LANES))

    items = [(unit, j) for unit in units for j in range(nsub)]
    return [(functools.partial(scores, item), functools.partial(finish, item, n == 0, n == len(items) - 1))
            for n, item in enumerate(items)]


def _run_lookahead(work):
    pending = [scores() for scores, _ in work[:ATTN_LOOKAHEAD]]
    for n, (_, finish) in enumerate(work):
        if n + ATTN_LOOKAHEAD < len(work):
            pending.append(work[n + ATTN_LOOKAHEAD][0]())
        finish(pending.pop(0))


def _attn_kernel(q_ref, k_ref, v_ref, kp_ref, vp_ref, o_ref, lse_ref, lse_scr, *, w, n_heads):
    masks = _band_masks(w, pl.program_id(1) > 0)
    work = []
    for sq in range(q_ref.shape[0]):
        work += _attn_segment(masks, q_ref.at[sq], k_ref.at[sq], v_ref.at[sq], kp_ref.at[sq], vp_ref.at[sq],
                              o_ref.at[sq], lse_ref.at[sq], lse_scr.at[sq], w=w, n_heads=n_heads)
    _run_lookahead(work)


def _attn_group(q, k, v, *, w, n_heads, rows=ATTN_STEP_ROWS):
    nseq, l, cg = q.shape
    tq = min(rows, l)
    nb = min(rows // tq, ATTN_STEP_SEQS)
    ratio = tq // w
    kern = functools.partial(_attn_kernel, w=w, n_heads=n_heads)
    cur = pl.BlockSpec((nb, tq, cg), lambda n, i: (n, i, 0))
    prev = pl.BlockSpec((nb, w, cg), lambda n, i: (n, jnp.maximum(i * ratio - 1, 0), 0))
    return pl.pallas_call(
        kern,
        grid=(nseq // nb, l // tq),
        in_specs=[cur, cur, cur, prev, prev],
        out_specs=[cur, pl.BlockSpec((nb, tq, LANES), lambda n, i: (n, i, 0))],
        out_shape=[jax.ShapeDtypeStruct((nseq, l, cg), BF16),
                   jax.ShapeDtypeStruct((nseq, l, LANES), F32)],
        scratch_shapes=[pltpu.VMEM((nb, tq, LANES), F32)],
        compiler_params=_params("arbitrary", "arbitrary"),
        name=f"attn_h{n_heads}_l{l}",
    )(q, k, v, k, v)


def _lane_chunks(blk, cg):
    chunks = []
    for c0 in range(0, cg, LANES):
        piece = blk[:, c0:min(c0 + LANES, cg)]
        if piece.shape[1] < LANES:
            piece = jnp.concatenate([piece, jnp.zeros((blk.shape[0], LANES - piece.shape[1]), F32)], axis=1)
        chunks.append(piece)
    return chunks


def _merge_steps(x_ref, o_refs, l_refs, wout_ref, scratch, x1_ref, *, ts, groups):
    ng = len(groups)
    outs, lses = [], []
    si = 0
    for gi, (d, cg) in enumerate(groups):
        if d == 1:
            outs.append(_lane_chunks(o_refs[gi][0].astype(F32), cg))
            lses.append(l_refs[gi][0])
            continue
        n = ts // d
        o_scr, l_scr = scratch[si], scratch[si + 1]
        si += 2
        for r in range(d):
            for c, piece in enumerate(_lane_chunks(o_refs[gi][r].astype(F32), cg)):
                o_scr[c, pl.ds(r, n, stride=d), :] = piece
            l_scr[pl.ds(r, n, stride=d), :] = l_refs[gi][r]
        outs.append([o_scr[c] for c in range(o_scr.shape[0])])
        lses.append(l_scr[...])
    m = functools.reduce(jnp.maximum, lses)
    es = [jnp.exp(l - m) for l in lses]
    inv = float(ng) / functools.reduce(lambda a, b2: a + b2, es)
    lane = lax.broadcasted_iota(jnp.int32, (1, LANES), 1)
    full, half = [], None
    for (d, cg), chunks, e in zip(groups, outs, es):
        alpha = (e * inv)[:, :1]
        for c, chunk in enumerate(chunks):
            chunk = chunk * alpha
            whole = cg - c * LANES >= LANES
            if half is None:
                if whole:
                    full.append(chunk)
                else:
                    half = chunk
            else:
                rot = pltpu.roll(chunk, HEAD_DIM, axis=1)
                full.append(jnp.where(lane < HEAD_DIM, half, rot))
                half = rot if whole else None
    assert half is None, "attention width must be a multiple of 128"
    merged = jnp.concatenate(full, axis=1).astype(BF16)
    yield
    x1_ref[...] = x_ref[...] + jnp.dot(merged, wout_ref[...].astype(BF16), preferred_element_type=F32)
    yield


def _merge_ffn_kernel(x_ref, *refs, layer, ts, groups, n_tiles):
    ng = len(groups)
    o_refs, l_refs = refs[:ng], refs[ng:2 * ng]
    wout_ref, gf_ref = refs[2 * ng:2 * ng + 2]
    hbm_refs = refs[2 * ng + 2:2 * ng + 5]
    gfin_ref, y_ref, x1_ref = refs[2 * ng + 5:2 * ng + 8]
    (wg_ref, wu_ref, wd_ref), stage_refs, sem = refs[2 * ng + 8:2 * ng + 11], refs[2 * ng + 11:2 * ng + 14], refs[2 * ng + 14]
    scratch = refs[2 * ng + 15:]
    t = pl.program_id(0)
    make_mixer = functools.partial(_merge_steps, x_ref, o_refs, l_refs, wout_ref, scratch, x1_ref,
                                   ts=ts, groups=groups)
    make_ffn = functools.partial(_swiglu_steps, x1_ref, y_ref, gf_ref, wg_ref, wu_ref, wd_ref,
                                 lambda acc: _rmsnorm(acc, gfin_ref[...]))
    load_weights = functools.partial(_load_ffn_weights, layer, hbm_refs, (wg_ref, wu_ref, wd_ref), stage_refs, sem)
    _skewed_step(t, n_tiles, make_mixer, make_ffn, load_weights)


def _merge_ffn_layer(x, outs, lses, w_out, g_ffn, w_gate, w_up, w_down, g_final, groups, *,
                     attn_layer, layer, ts=TOKEN_TILE):
    b, s, d_model = x.shape
    per_batch = s // ts
    n_tiles = b * per_batch
    kern = functools.partial(_merge_ffn_kernel, layer=layer, ts=ts, groups=groups, n_tiles=n_tiles)

    def mixer_spec(block):
        lead = len(block) - 2
        return pl.BlockSpec(block, lambda t: (_tile_index(t, n_tiles, per_batch)[0],) + (0,) * (lead - 1)
                            + (_tile_index(t, n_tiles, per_batch)[1], 0))

    ffn_tile = pl.BlockSpec((None, ts, d_model),
                            lambda t: (*_tile_index(jnp.maximum(t - 1, 0), n_tiles, per_batch), 0))
    o_specs = [mixer_spec((None, d, ts // d, cg)) for d, cg in groups]
    l_specs = [mixer_spec((None, d, ts // d, LANES)) for d, cg in groups]
    scratch = [pltpu.VMEM((ts, d_model), F32)] + _ffn_scratch(d_model, w_gate.shape[-1])
    for d, cg in groups:
        if d > 1:
            scratch += [pltpu.VMEM((pl.cdiv(cg, LANES), ts, LANES), F32), pltpu.VMEM((ts, LANES), F32)]
    return pl.pallas_call(
        kern,
        grid=(n_tiles + 1,),
        in_specs=[mixer_spec((None, ts, d_model))] + o_specs + l_specs + [_layer_spec(w_out, attn_layer)]
        + _ffn_specs(d_model) + [_const_spec((1, d_model))],
        out_specs=ffn_tile,
        out_shape=jax.ShapeDtypeStruct(x.shape, F32),
        scratch_shapes=scratch,
        compiler_params=_params("arbitrary"),
        name="attn_merge_ffn_final",
    )(x, *outs, *lses, w_out, g_ffn.reshape(1, d_model), w_gate, w_up, w_down, g_final.reshape(1, d_model))


def _rope_tables(s):
    half = HEAD_DIM // 2
    inv_freq = 1.0 / (ROPE_THETA ** (np.arange(0, HEAD_DIM, 2, dtype=np.float64) / HEAD_DIM))
    ang = np.arange(s, dtype=np.float64)[:, None] * inv_freq[None, :]
    cos, sin = np.cos(ang), np.sin(ang)
    reps = LANES // half
    cos_t = np.concatenate([cos] * reps, axis=-1)
    sin_t = np.concatenate([-sin, sin] * (reps // 2), axis=-1)
    return jnp.asarray(cos_t, F32), jnp.asarray(sin_t, F32)


def kernel(x, norm_mix, norm_ffn, norm_final, pool_w_in, pool_w_group, pool_scale, pool_w_out,
           attn_w_qkv, attn_w_out, ffn_w_gate, ffn_w_up, ffn_w_down):
    b, s, d_model = x.shape
    assert norm_mix.shape[0] == 2, "two layers: pooling mixer then dilated attention"
    n_heads = attn_w_out.shape[1] // HEAD_DIM
    d_attn = n_heads * HEAD_DIM
    head_groups = _head_groups(n_heads)
    groups = tuple((dil, nh * HEAD_DIM) for (_, dil), nh in zip(ATTN_PATTERNS, head_groups))
    bf = lambda t: t.astype(BF16)

    x = _pool_ffn_layer(x, norm_mix[0], pool_w_in, pool_w_group, pool_scale[0], pool_w_out,
                        norm_ffn[0], ffn_w_gate, ffn_w_up, ffn_w_down, layer=0)

    w_qkv = attn_w_qkv[0]
    part_scale = (HEAD_DIM ** -0.5, 1.0, 1.0)
    w_groups, col = [], 0
    for _, cg in groups:
        w_groups.append(bf(jnp.concatenate(
            [w_qkv[:, p * d_attn + col:p * d_attn + col + cg] * part_scale[p] for p in range(3)], axis=1)))
        col += cg
    cos_t, sin_t = _rope_tables(s)
    qkv = _qkv_layer(x, norm_mix[1], cos_t, sin_t, w_groups, groups)
    outs, lses = [], []
    for gi, ((window, dil), nh) in enumerate(zip(ATTN_PATTERNS, head_groups)):
        cg = nh * HEAD_DIM
        q, k, v = (t.reshape(b * dil, s // dil, cg) for t in qkv[3 * gi:3 * gi + 3])
        o, lse = _attn_group(q, k, v, w=window // dil, n_heads=nh)
        outs.append(o.reshape(b, dil, s // dil, cg))
        lses.append(lse.reshape(b, dil, s // dil, LANES))
    return _merge_ffn_layer(x, outs, lses, attn_w_out, norm_ffn[1], ffn_w_gate, ffn_w_up, ffn_w_down,
                            norm_final, groups, attn_layer=0, layer=1)
```

```python
import functools
import math

import jax
import jax.numpy as jnp
import numpy as np
from jax import lax
from jax.experimental import pallas as pl
from jax.experimental.pallas import tpu as pltpu

EPS = 1e-6
POOL_WINDOWS = (2, 4, 8, 16)
HEAD_DIM = 64
ATTN_PATTERNS = ((128, 1), (512, 4), (2048, 16))
ROPE_THETA = 10000.0
NEG_INF = -1e30

LANES = 128
VMEM_LIMIT_BYTES = 58 * 1024 * 1024
TOKEN_TILE = 512
QKV_TILE = 1024
QKV_SUB_TILE = 256
ATTN_STEP_ROWS = 2048
ATTN_STEP_SEQS = 2
ATTN_LOOKAHEAD = 2
FFN_CHUNK = 256
MIXER_GAP = 3
DEINT_STRIDE = 4

F32 = jnp.float32
BF16 = jnp.bfloat16


def _head_groups(n_heads):
    n = len(ATTN_PATTERNS)
    return tuple(n_heads // n + (1 if g < n_heads % n else 0) for g in range(n))


def _rmsnorm(x, g):
    ms = jnp.mean(x * x, axis=-1, keepdims=True)
    return x * lax.rsqrt(ms + EPS) * g


def _params(*sem):
    return pltpu.CompilerParams(dimension_semantics=sem, vmem_limit_bytes=VMEM_LIMIT_BYTES)


def _const_spec(shape):
    zeros = (0,) * len(shape)
    return pl.BlockSpec(shape, lambda *_: zeros, pipeline_mode=pl.Buffered(1))


def _layer_spec(stacked, layer):
    idx = (layer,) + (0,) * (stacked.ndim - 1)
    return pl.BlockSpec((None,) + stacked.shape[1:], lambda *_: idx, pipeline_mode=pl.Buffered(1))


def _swiglu_steps(x1_ref, o_ref, g_ref, wg_ref, wu_ref, wd_ref, finish):
    x1 = x1_ref[...]
    h = _rmsnorm(x1, g_ref[...]).astype(BF16)
    acc = x1
    n_chunks = wg_ref.shape[-1] // FFN_CHUNK
    for c in range(n_chunks):
        sl = slice(c * FFN_CHUNK, (c + 1) * FFN_CHUNK)
        gate = jnp.dot(h, wg_ref[:, sl], preferred_element_type=F32)
        up = jnp.dot(h, wu_ref[:, sl], preferred_element_type=F32)
        a = (gate * jax.nn.sigmoid(gate) * up).astype(BF16)
        acc = acc + jnp.dot(a, wd_ref[sl, :], preferred_element_type=F32)
        if c == n_chunks - 1:
            o_ref[...] = finish(acc)
        yield


def _load_ffn_weights(layer, hbm_refs, bf_refs, stage_refs, sem, meanwhile):
    n_chunks = hbm_refs[0].shape[-1] // FFN_CHUNK

    def chunk(c):
        return slice(c * FFN_CHUNK, (c + 1) * FFN_CHUNK)

    def copies(c):
        srcs = (hbm_refs[0].at[layer, :, chunk(c)], hbm_refs[1].at[layer, :, chunk(c)],
                hbm_refs[2].at[layer, chunk(c), :])
        return [pltpu.make_async_copy(src, stage.at[c % 2], sem.at[k, c % 2])
                for k, (src, stage) in enumerate(zip(srcs, stage_refs))]

    for cp in copies(0):
        cp.start()
    meanwhile()
    for c in range(n_chunks):
        if c + 1 < n_chunks:
            for cp in copies(c + 1):
                cp.start()
        for cp in copies(c):
            cp.wait()
        bf_refs[0][:, chunk(c)] = stage_refs[0][c % 2].astype(BF16)
        bf_refs[1][:, chunk(c)] = stage_refs[1][c % 2].astype(BF16)
        bf_refs[2][chunk(c), :] = stage_refs[2][c % 2].astype(BF16)


def _skewed_step(t, n_tiles, make_mixer, make_ffn, load_weights):
    @pl.when(t == 0)
    def _():
        def run_mixer():
            for _ in make_mixer():
                pass
        load_weights(run_mixer)

    @pl.when(t == n_tiles)
    def _():
        for _ in make_ffn():
            pass

    @pl.when(jnp.logical_and(t > 0, t < n_tiles))
    def _():
        mixer = make_mixer()
        for c, _ in enumerate(make_ffn()):
            if c % MIXER_GAP == MIXER_GAP - 1:
                next(mixer, None)
        for _ in mixer:
            pass


def _ffn_specs(d):
    hbm = pl.BlockSpec(memory_space=pl.ANY)
    return [_const_spec((1, d)), hbm, hbm, hbm]


def _ffn_scratch(d, f):
    return [pltpu.VMEM((d, f), BF16), pltpu.VMEM((d, f), BF16), pltpu.VMEM((f, d), BF16),
            pltpu.VMEM((2, d, FFN_CHUNK), F32), pltpu.VMEM((2, d, FFN_CHUNK), F32),
            pltpu.VMEM((2, FFN_CHUNK, d), F32), pltpu.SemaphoreType.DMA((3, 2))]


def _tile_index(t, n_tiles, per_batch):
    tt = jnp.minimum(t, n_tiles - 1)
    return tt // per_batch, tt % per_batch


def _pool_steps(i, x_ref, g_ref, win_ref, wgrp_ref, scale_ref, wout_ref, ext_ref, x1_ref, *, ts, halo):
    d_model = x_ref.shape[-1]
    gdim = d_model // len(POOL_WINDOWS)
    h = _rmsnorm(x_ref[...], g_ref[...]).astype(BF16)
    ext_ref[halo:halo + ts, :] = jnp.dot(h, win_ref[...].astype(BF16), preferred_element_type=F32)
    yield
    pos = i * ts + lax.broadcasted_iota(jnp.int32, (ts, 1), 0)
    zs = []
    for g, w in enumerate(POOL_WINDOWS):
        cols = slice(g * gdim, (g + 1) * gdim)
        s = ext_ref[:, cols]
        k = 1
        while k < w:
            s = s + pltpu.roll(s, k, axis=0)
            k *= 2
        inv_cnt = 1.0 / jnp.minimum(pos + 1, w).astype(F32)
        p = s[halo:, :] * inv_cnt - ext_ref[halo:halo + ts, cols]
        zs.append(jnp.dot(p.astype(BF16), wgrp_ref[g].astype(BF16), preferred_element_type=F32))
    yield
    z = jnp.concatenate(zs, axis=1) * scale_ref[...]
    y = jnp.dot(z.astype(BF16), wout_ref[...].astype(BF16), preferred_element_type=F32)
    ext_ref[0:halo, :] = ext_ref[ts:ts + halo, :]
    x1_ref[...] = x_ref[...] + y
    yield


def _pool_ffn_kernel(x_ref, g_ref, win_ref, wgrp_ref, scale_ref, wout_ref, gf_ref, *refs,
                     layer, ts, halo, n_tiles, per_batch):
    hbm_refs, (o_ref, ext_ref, x1_ref) = refs[:3], refs[3:6]
    (wg_ref, wu_ref, wd_ref), stage_refs, sem = refs[6:9], refs[9:12], refs[12]
    t = pl.program_id(0)
    i = jnp.minimum(t, n_tiles - 1) % per_batch

    @pl.when(i == 0)
    def _():
        ext_ref[0:halo, :] = jnp.zeros((halo, x_ref.shape[-1]), F32)

    make_mixer = functools.partial(_pool_steps, i, x_ref, g_ref, win_ref, wgrp_ref, scale_ref, wout_ref,
                                   ext_ref, x1_ref, ts=ts, halo=halo)
    make_ffn = functools.partial(_swiglu_steps, x1_ref, o_ref, gf_ref, wg_ref, wu_ref, wd_ref, lambda acc: acc)
    load_weights = functools.partial(_load_ffn_weights, layer, hbm_refs, (wg_ref, wu_ref, wd_ref), stage_refs, sem)
    _skewed_step(t, n_tiles, make_mixer, make_ffn, load_weights)


def _pool_ffn_layer(x, g_mix, w_in, w_group, scale, w_out, g_ffn, w_gate, w_up, w_down, *, layer, ts=TOKEN_TILE):
    b, s, d = x.shape
    halo = max(POOL_WINDOWS)
    per_batch = s // ts
    n_tiles = b * per_batch
    kern = functools.partial(_pool_ffn_kernel, layer=layer, ts=ts, halo=halo, n_tiles=n_tiles, per_batch=per_batch)
    mixer_tile = pl.BlockSpec((None, ts, d), lambda t: (*_tile_index(t, n_tiles, per_batch), 0))
    ffn_tile = pl.BlockSpec((None, ts, d), lambda t: (*_tile_index(jnp.maximum(t - 1, 0), n_tiles, per_batch), 0))
    return pl.pallas_call(
        kern,
        grid=(n_tiles + 1,),
        in_specs=[mixer_tile, _const_spec((1, d)), _layer_spec(w_in, layer), _layer_spec(w_group, layer),
                  _const_spec((1, d)), _layer_spec(w_out, layer)] + _ffn_specs(d),
        out_specs=ffn_tile,
        out_shape=jax.ShapeDtypeStruct(x.shape, F32),
        scratch_shapes=[pltpu.VMEM((ts + halo, d), F32), pltpu.VMEM((ts, d), F32)]
        + _ffn_scratch(d, w_gate.shape[-1]),
        compiler_params=_params("arbitrary"),
        name="pool_ffn",
    )(x, g_mix.reshape(1, d), w_in, w_group, scale.reshape(1, d), w_out, g_ffn.reshape(1, d), w_gate, w_up, w_down)


def _class_order(d):
    order, cur = (0,), 1
    while cur < d:
        order = tuple(c + cur * r for c in order for r in range(DEINT_STRIDE))
        cur *= DEINT_STRIDE
    assert cur == d, "dilations must be powers of DEINT_STRIDE"
    return order


def _split_rows(src, n_blocks, blk):
    n = blk // DEINT_STRIDE
    return jnp.concatenate([src[pl.ds(b0 * blk + r, n, stride=DEINT_STRIDE), :]
                            for b0 in range(n_blocks) for r in range(DEINT_STRIDE)], axis=0)


def _regroup_qkv_weights(w_ref, wb_refs, groups):
    d_model, d_attn = w_ref.shape[0], w_ref.shape[1] // 3
    half = LANES // 2
    lane = lax.broadcasted_iota(jnp.int32, (1, LANES), 1)

    def body(rb, carry):
        rows = pl.ds(pl.multiple_of(rb * LANES, LANES), LANES)

        def half_of(src, upper):
            ci = src // LANES
            v = w_ref[rows, ci * LANES:(ci + 1) * LANES]
            if src < d_attn:
                v = v * HEAD_DIM ** -0.5
            return v if (src % LANES == half) == upper else pltpu.roll(v, half, axis=1)

        col = 0
        for wb_ref, (_, cg) in zip(wb_refs, groups):
            runs = [p * d_attn + col + i * half for p in range(3) for i in range(cg // half)]
            for k in range(0, len(runs), 2):
                lo = half_of(runs[k], False)
                if k + 1 < len(runs):
                    chunk = jnp.where(lane < half, lo, half_of(runs[k + 1], True))
                    wb_ref[rows, k * half:(k + 2) * half] = chunk.astype(BF16)
                else:
                    wb_ref[rows, k * half:(k + 1) * half] = lo[:, :half].astype(BF16)
            col += cg
        return carry

    lax.fori_loop(0, d_model // LANES, body, 0)


def _qkv_kernel(x_ref, g_ref, cos_ref, sin_ref, w_ref, *refs, ts, sub, groups):
    ng = len(groups)
    out_refs, w_refs = refs[:3 * ng], refs[3 * ng:4 * ng]
    hs_all, tab_all = refs[4 * ng:]
    lane = lax.broadcasted_iota(jnp.int32, (1, LANES), 1)
    first_half = (lane % HEAD_DIM) < (HEAD_DIM // 2)
    n_chunks = x_ref.shape[-1] // LANES

    @pl.when(jnp.logical_and(pl.program_id(0) == 0, pl.program_id(1) == 0))
    def _():
        _regroup_qkv_weights(w_ref, w_refs, groups)

    for si, s0 in enumerate(range(0, ts, sub)):
        hs_ref, tab_ref = hs_all.at[si], tab_all.at[si]
        h = _rmsnorm(x_ref[s0:s0 + sub, :], g_ref[...])
        cs, sn = cos_ref[s0:s0 + sub, :], sin_ref[s0:s0 + sub, :]
        cur_d = 1
        for gi, (d, cg) in enumerate(groups):
            while cur_d < d:
                for c in range(n_chunks):
                    hs_ref[c] = h[:, c * LANES:(c + 1) * LANES]
                tab_ref[0], tab_ref[1] = cs, sn
                blk = sub // cur_d
                h = jnp.concatenate([_split_rows(hs_ref.at[c], cur_d, blk) for c in range(n_chunks)], axis=1)
                cs, sn = _split_rows(tab_ref.at[0], cur_d, blk), _split_rows(tab_ref.at[1], cur_d, blk)
                cur_d *= DEINT_STRIDE
            assert cur_d == d, "head groups must come in increasing powers of DEINT_STRIDE"
            qkv = jnp.dot(h.astype(BF16), w_refs[gi][...], preferred_element_type=F32)
            chunks = []
            for c in range(2 * cg // LANES):
                t = qkv[:, c * LANES:(c + 1) * LANES]
                swapped = jnp.where(first_half, pltpu.roll(t, LANES - HEAD_DIM // 2, axis=1),
                                    pltpu.roll(t, HEAD_DIM // 2, axis=1))
                chunks.append(t * cs + swapped * sn)
            qk = jnp.concatenate(chunks, axis=1)
            parts = (qk[:, :cg], qk[:, cg:2 * cg], qkv[:, 2 * cg:])
            n = sub // d
            for part, o_ref in zip(parts, out_refs[3 * gi:3 * gi + 3]):
                pb = part.astype(BF16)
                for slot, r in enumerate(_class_order(d)):
                    o_ref[r, si * n:(si + 1) * n, :] = pb[slot * n:(slot + 1) * n, :]


def _qkv_layer(x, g, cos, sin, w_qkv, groups, *, layer, ts=QKV_TILE, sub=QKV_SUB_TILE):
    b, s, d_model = x.shape
    kern = functools.partial(_qkv_kernel, ts=ts, sub=sub, groups=groups)
    out_shapes, out_specs = [], []
    for d, cg in groups:
        for _ in range(3):
            out_shapes.append(jax.ShapeDtypeStruct((b, d, s // d, cg), BF16))
            out_specs.append(pl.BlockSpec((None, d, ts // d, cg), lambda bi, i: (bi, 0, i, 0)))
    table_spec = pl.BlockSpec((ts, LANES), lambda bi, i: (i, 0))
    return pl.pallas_call(
        kern,
        grid=(b, s // ts),
        in_specs=[pl.BlockSpec((None, ts, d_model), lambda bi, i: (bi, i, 0)), _const_spec((1, d_model)),
                  table_spec, table_spec, _layer_spec(w_qkv, layer)],
        out_specs=out_specs,
        out_shape=out_shapes,
        scratch_shapes=[pltpu.VMEM((d_model, 3 * cg), BF16) for _, cg in groups]
        + [pltpu.VMEM((ts // sub, d_model // LANES, sub, LANES), F32),
           pltpu.VMEM((ts // sub, 2, sub, LANES), F32)],
        compiler_params=_params("arbitrary", "arbitrary"),
        name="qkv_rope",
    )(x, g.reshape(1, d_model), cos, sin, w_qkv)


def _band_masks(w, has_prev):
    qi = lax.broadcasted_iota(jnp.int32, (w, 2 * w), 0)
    kj = lax.broadcasted_iota(jnp.int32, (w, 2 * w), 1)
    band = jnp.logical_and(kj >= qi, kj <= qi + w)
    return band, jnp.logical_and(band, jnp.logical_or(kj >= w, has_prev))


def _attn_segment(masks, q_ref, k_ref, v_ref, kp_ref, vp_ref, o_ref, lse_ref, lse_scr, *, w, n_heads):
    band, band_first = masks
    tq = q_ref.shape[0]
    cg = n_heads * HEAD_DIM
    nsub = tq // w
    lane = lax.broadcasted_iota(jnp.int32, (1, LANES), 1)
    nt = (((1,), (1,)), ((), ()))

    units = []
    for c0 in range(0, cg, LANES):
        width = min(LANES, cg - c0)
        for hh in range(width // HEAD_DIM):
            if width == LANES:
                sel = (lane < HEAD_DIM) if hh == 0 else (lane >= HEAD_DIM)
            else:
                sel = None
            units.append((len(units), slice(c0, c0 + width), hh, sel))

    def key_rows(ref, prev_ref, cols, j):
        if j == 0:
            return jnp.concatenate([prev_ref[:, cols], ref[0:w, cols]], axis=0)
        return ref[(j - 1) * w:(j + 1) * w, cols]

    def scores(item):
        (_, cols, _, sel), j = item
        qb = q_ref[j * w:(j + 1) * w, cols]
        if sel is not None:
            qb = jnp.where(sel, qb, jnp.zeros_like(qb))
        return lax.dot_general(qb, key_rows(k_ref, kp_ref, cols, j), nt, preferred_element_type=F32)

    def finish(item, first, last, sc):
        (h, cols, hh, _), j = item
        if first:
            lse_scr[0:tq, :] = jnp.full((tq, LANES), NEG_INF, F32)
        lo = hh * HEAD_DIM
        rows = slice(j * w, (j + 1) * w)
        s = jnp.where(band_first if j == 0 else band, sc, NEG_INF)
        m = jnp.max(s, axis=-1, keepdims=True)
        e = jnp.exp(s - m)
        den = jnp.sum(e, axis=-1, keepdims=True)
        pv = jnp.dot(e.astype(BF16), key_rows(v_ref, vp_ref, cols, j), preferred_element_type=F32) / den
        o_ref[rows, cols.start + lo:cols.start + lo + HEAD_DIM] = pv[:, lo:lo + HEAD_DIM].astype(BF16)
        lse_scr[rows, :] = jnp.where(lane == h, m + jnp.log(den), lse_scr[rows, :])
        if last:
            lse = lse_scr[0:tq, :]
            mx = jnp.max(lse, axis=-1, keepdims=True)
            tot = jnp.sum(jnp.exp(lse - mx), axis=-1, keepdims=True)
            lse_ref[...] = jnp.broadcast_to(mx + jnp.log(tot) - math.log(n_heads), (tq, LANES))

    items = [(unit, j) for unit in units for j in range(nsub)]
    return [(functools.partial(scores, item), functools.partial(finish, item, n == 0, n == len(items) - 1))
            for n, item in enumerate(items)]


def _run_lookahead(work):
    pending = [scores() for scores, _ in work[:ATTN_LOOKAHEAD]]
    for n, (_, finish) in enumerate(work):
        if n + ATTN_LOOKAHEAD < len(work):
            pending.append(work[n + ATTN_LOOKAHEAD][0]())
        finish(pending.pop(0))


def _attn_kernel(q_ref, k_ref, v_ref, kp_ref, vp_ref, o_ref, lse_ref, lse_scr, *, w, n_heads):
    masks = _band_masks(w, pl.program_id(1) > 0)
    work = []
    for sq in range(q_ref.shape[0]):
        work += _attn_segment(masks, q_ref.at[sq], k_ref.at[sq], v_ref.at[sq], kp_ref.at[sq], vp_ref.at[sq],
                              o_ref.at[sq], lse_ref.at[sq], lse_scr.at[sq], w=w, n_heads=n_heads)
    _run_lookahead(work)


def _attn_group(q, k, v, *, w, n_heads, rows=ATTN_STEP_ROWS):
    nseq, l, cg = q.shape
    tq = min(rows, l)
    nb = min(rows // tq, ATTN_STEP_SEQS)
    ratio = tq // w
    kern = functools.partial(_attn_kernel, w=w, n_heads=n_heads)
    cur = pl.BlockSpec((nb, tq, cg), lambda n, i: (n, i, 0))
    prev = pl.BlockSpec((nb, w, cg), lambda n, i: (n, jnp.maximum(i * ratio - 1, 0), 0))
    return pl.pallas_call(
        kern,
        grid=(nseq // nb, l // tq),
        in_specs=[cur, cur, cur, prev, prev],
        out_specs=[cur, pl.BlockSpec((nb, tq, LANES), lambda n, i: (n, i, 0))],
        out_shape=[jax.ShapeDtypeStruct((nseq, l, cg), BF16),
                   jax.ShapeDtypeStruct((nseq, l, LANES), F32)],
        scratch_shapes=[pltpu.VMEM((nb, tq, LANES), F32)],
        compiler_params=_params("arbitrary", "arbitrary"),
        name=f"attn_h{n_heads}_l{l}",
    )(q, k, v, k, v)


def _lane_chunks(blk, cg):
    chunks = []
    for c0 in range(0, cg, LANES):
        piece = blk[:, c0:min(c0 + LANES, cg)]
        if piece.shape[1] < LANES:
            piece = jnp.concatenate([piece, jnp.zeros((blk.shape[0], LANES - piece.shape[1]), F32)], axis=1)
        chunks.append(piece)
    return chunks


def _merge_steps(x_ref, o_refs, l_refs, wout_ref, scratch, x1_ref, *, ts, groups):
    ng = len(groups)
    outs, lses = [], []
    si = 0
    for gi, (d, cg) in enumerate(groups):
        if d == 1:
            outs.append(_lane_chunks(o_refs[gi][0].astype(F32), cg))
            lses.append(l_refs[gi][0])
            continue
        n = ts // d
        o_scr, l_scr = scratch[si], scratch[si + 1]
        si += 2
        for r in range(d):
            for c, piece in enumerate(_lane_chunks(o_refs[gi][r].astype(F32), cg)):
                o_scr[c, pl.ds(r, n, stride=d), :] = piece
            l_scr[pl.ds(r, n, stride=d), :] = l_refs[gi][r]
        outs.append([o_scr[c] for c in range(o_scr.shape[0])])
        lses.append(l_scr[...])
    m = functools.reduce(jnp.maximum, lses)
    es = [jnp.exp(l - m) for l in lses]
    inv = float(ng) / functools.reduce(lambda a, b2: a + b2, es)
    lane = lax.broadcasted_iota(jnp.int32, (1, LANES), 1)
    full, half = [], None
    for (d, cg), chunks, e in zip(groups, outs, es):
        alpha = (e * inv)[:, :1]
        for c, chunk in enumerate(chunks):
            chunk = chunk * alpha
            whole = cg - c * LANES >= LANES
            if half is None:
                if whole:
                    full.append(chunk)
                else:
                    half = chunk
            else:
                rot = pltpu.roll(chunk, HEAD_DIM, axis=1)
                full.append(jnp.where(lane < HEAD_DIM, half, rot))
                half = rot if whole else None
    assert half is None, "attention width must be a multiple of 128"
    merged = jnp.concatenate(full, axis=1).astype(BF16)
    yield
    x1_ref[...] = x_ref[...] + jnp.dot(merged, wout_ref[...].astype(BF16), preferred_element_type=F32)
    yield


def _merge_ffn_kernel(x_ref, *refs, layer, ts, groups, n_tiles):
    ng = len(groups)
    o_refs, l_refs = refs[:ng], refs[ng:2 * ng]
    wout_ref, gf_ref = refs[2 * ng:2 * ng + 2]
    hbm_refs = refs[2 * ng + 2:2 * ng + 5]
    gfin_ref, y_ref, x1_ref = refs[2 * ng + 5:2 * ng + 8]
    (wg_ref, wu_ref, wd_ref), stage_refs, sem = refs[2 * ng + 8:2 * ng + 11], refs[2 * ng + 11:2 * ng + 14], refs[2 * ng + 14]
    scratch = refs[2 * ng + 15:]
    t = pl.program_id(0)
    make_mixer = functools.partial(_merge_steps, x_ref, o_refs, l_refs, wout_ref, scratch, x1_ref,
                                   ts=ts, groups=groups)
    make_ffn = functools.partial(_swiglu_steps, x1_ref, y_ref, gf_ref, wg_ref, wu_ref, wd_ref,
                                 lambda acc: _rmsnorm(acc, gfin_ref[...]))
    load_weights = functools.partial(_load_ffn_weights, layer, hbm_refs, (wg_ref, wu_ref, wd_ref), stage_refs, sem)
    _skewed_step(t, n_tiles, make_mixer, make_ffn, load_weights)


def _merge_ffn_layer(x, outs, lses, w_out, g_ffn, w_gate, w_up, w_down, g_final, groups, *,
                     attn_layer, layer, ts=TOKEN_TILE):
    b, s, d_model = x.shape
    per_batch = s // ts
    n_tiles = b * per_batch
    kern = functools.partial(_merge_ffn_kernel, layer=layer, ts=ts, groups=groups, n_tiles=n_tiles)

    def mixer_spec(block):
        lead = len(block) - 2
        return pl.BlockSpec(block, lambda t: (_tile_index(t, n_tiles, per_batch)[0],) + (0,) * (lead - 1)
                            + (_tile_index(t, n_tiles, per_batch)[1], 0))

    ffn_tile = pl.BlockSpec((None, ts, d_model),
                            lambda t: (*_tile_index(jnp.maximum(t - 1, 0), n_tiles, per_batch), 0))
    o_specs = [mixer_spec((None, d, ts // d, cg)) for d, cg in groups]
    l_specs = [mixer_spec((None, d, ts // d, LANES)) for d, cg in groups]
    scratch = [pltpu.VMEM((ts, d_model), F32)] + _ffn_scratch(d_model, w_gate.shape[-1])
    for d, cg in groups:
        if d > 1:
            scratch += [pltpu.VMEM((pl.cdiv(cg, LANES), ts, LANES), F32), pltpu.VMEM((ts, LANES), F32)]
    return pl.pallas_call(
        kern,
        grid=(n_tiles + 1,),
        in_specs=[mixer_spec((None, ts, d_model))] + o_specs + l_specs + [_layer_spec(w_out, attn_layer)]
        + _ffn_specs(d_model) + [_const_spec((1, d_model))],
        out_specs=ffn_tile,
        out_shape=jax.ShapeDtypeStruct(x.shape, F32),
        scratch_shapes=scratch,
        compiler_params=_params("arbitrary"),
        name="attn_merge_ffn_final",
    )(x, *outs, *lses, w_out, g_ffn.reshape(1, d_model), w_gate, w_up, w_down, g_final.reshape(1, d_model))


def _rope_tables(s):
    half = HEAD_DIM // 2
    inv_freq = 1.0 / (ROPE_THETA ** (np.arange(0, HEAD_DIM, 2, dtype=np.float64) / HEAD_DIM))
    ang = np.arange(s, dtype=np.float64)[:, None] * inv_freq[None, :]
    cos, sin = np.cos(ang), np.sin(ang)
    reps = LANES // half
    cos_t = np.concatenate([cos] * reps, axis=-1)
    sin_t = np.concatenate([-sin, sin] * (reps // 2), axis=-1)
    return jnp.asarray(cos_t, F32), jnp.asarray(sin_t, F32)


def kernel(x, norm_mix, norm_ffn, norm_final, pool_w_in, pool_w_group, pool_scale, pool_w_out,
           attn_w_qkv, attn_w_out, ffn_w_gate, ffn_w_up, ffn_w_down):
    b, s, d_model = x.shape
    assert norm_mix.shape[0] == 2, "two layers: pooling mixer then dilated attention"
    n_heads = attn_w_out.shape[1] // HEAD_DIM
    head_groups = _head_groups(n_heads)
    groups = tuple((dil, nh * HEAD_DIM) for (_, dil), nh in zip(ATTN_PATTERNS, head_groups))

    x = _pool_ffn_layer(x, norm_mix[0], pool_w_in, pool_w_group, pool_scale[0], pool_w_out,
                        norm_ffn[0], ffn_w_gate, ffn_w_up, ffn_w_down, layer=0)

    cos_t, sin_t = _rope_tables(s)
    qkv = _qkv_layer(x, norm_mix[1], cos_t, sin_t, attn_w_qkv, groups, layer=0)
    outs, lses = [], []
    for gi, ((window, dil), nh) in enumerate(zip(ATTN_PATTERNS, head_groups)):
        cg = nh * HEAD_DIM
        q, k, v = (t.reshape(b * dil, s // dil, cg) for t in qkv[3 * gi:3 * gi + 3])
        o, lse = _attn_group(q, k, v, w=window // dil, n_heads=nh)
        outs.append(o.reshape(b, dil, s // dil, cg))
        lses.append(lse.reshape(b, dil, s // dil, LANES))
    return _merge_ffn_layer(x, outs, lses, attn_w_out, norm_ffn[1], ffn_w_gate, ffn_w_up, ffn_w_down,
                            norm_final, groups, attn_layer=0, layer=1)
```

```python
import functools
import math

import jax
import jax.numpy as jnp
import numpy as np
from jax import lax
from jax.experimental import pallas as pl
from jax.experimental.pallas import tpu as pltpu

EPS = 1e-6
POOL_WINDOWS = (2, 4, 8, 16)
HEAD_DIM = 64
ATTN_PATTERNS = ((128, 1), (512, 4), (2048, 16))
ROPE_THETA = 10000.0
NEG_INF = -1e30

LANES = 128
BF16_ROWS = 16
VMEM_LIMIT_BYTES = 58 * 1024 * 1024
TOKEN_TILE = 512
QKV_TILE = 1024
QKV_SUB_TILE = 256
ATTN_STEP_ROWS = 2048
ATTN_STEP_SEQS = 2
ATTN_LOOKAHEAD = 2
FFN_CHUNK = 256
MIXER_GAP = 3
DEINT_STRIDE = 4

F32 = jnp.float32
BF16 = jnp.bfloat16


def _head_groups(n_heads):
    n = len(ATTN_PATTERNS)
    return tuple(n_heads // n + (1 if g < n_heads % n else 0) for g in range(n))


def _rmsnorm(x, g):
    ms = jnp.mean(x * x, axis=-1, keepdims=True)
    return x * lax.rsqrt(ms + EPS) * g


def _params(*sem):
    return pltpu.CompilerParams(dimension_semantics=sem, vmem_limit_bytes=VMEM_LIMIT_BYTES)


def _const_spec(shape):
    zeros = (0,) * len(shape)
    return pl.BlockSpec(shape, lambda *_: zeros, pipeline_mode=pl.Buffered(1))


def _layer_spec(stacked, layer):
    idx = (layer,) + (0,) * (stacked.ndim - 1)
    return pl.BlockSpec((None,) + stacked.shape[1:], lambda *_: idx, pipeline_mode=pl.Buffered(1))


def _swiglu_steps(x1_ref, o_ref, g_ref, wg_ref, wu_ref, wd_ref, finish):
    x1 = x1_ref[...]
    h = _rmsnorm(x1, g_ref[...]).astype(BF16)
    acc = x1
    n_chunks = wg_ref.shape[-1] // FFN_CHUNK
    for c in range(n_chunks):
        sl = slice(c * FFN_CHUNK, (c + 1) * FFN_CHUNK)
        gate = jnp.dot(h, wg_ref[:, sl], preferred_element_type=F32)
        up = jnp.dot(h, wu_ref[:, sl], preferred_element_type=F32)
        a = (gate * jax.nn.sigmoid(gate) * up).astype(BF16)
        acc = acc + jnp.dot(a, wd_ref[sl, :], preferred_element_type=F32)
        if c == n_chunks - 1:
            o_ref[...] = finish(acc)
        yield


def _load_ffn_weights(layer, hbm_refs, bf_refs, stage_refs, sem, meanwhile):
    n_chunks = hbm_refs[0].shape[-1] // FFN_CHUNK

    def chunk(c):
        return slice(c * FFN_CHUNK, (c + 1) * FFN_CHUNK)

    def copies(c):
        srcs = (hbm_refs[0].at[layer, :, chunk(c)], hbm_refs[1].at[layer, :, chunk(c)],
                hbm_refs[2].at[layer, chunk(c), :])
        return [pltpu.make_async_copy(src, stage.at[c % 2], sem.at[k, c % 2])
                for k, (src, stage) in enumerate(zip(srcs, stage_refs))]

    for cp in copies(0):
        cp.start()
    meanwhile()
    for c in range(n_chunks):
        if c + 1 < n_chunks:
            for cp in copies(c + 1):
                cp.start()
        for cp in copies(c):
            cp.wait()
        bf_refs[0][:, chunk(c)] = stage_refs[0][c % 2].astype(BF16)
        bf_refs[1][:, chunk(c)] = stage_refs[1][c % 2].astype(BF16)
        bf_refs[2][chunk(c), :] = stage_refs[2][c % 2].astype(BF16)


def _skewed_step(t, n_tiles, make_mixer, make_ffn, load_weights):
    @pl.when(t == 0)
    def _():
        def run_mixer():
            for _ in make_mixer():
                pass
        load_weights(run_mixer)

    @pl.when(t == n_tiles)
    def _():
        for _ in make_ffn():
            pass

    @pl.when(jnp.logical_and(t > 0, t < n_tiles))
    def _():
        mixer = make_mixer()
        for c, _ in enumerate(make_ffn()):
            if c % MIXER_GAP == MIXER_GAP - 1:
                next(mixer, None)
        for _ in mixer:
            pass


def _ffn_specs(d):
    hbm = pl.BlockSpec(memory_space=pl.ANY)
    return [_const_spec((1, d)), hbm, hbm, hbm]


def _ffn_scratch(d, f):
    return [pltpu.VMEM((d, f), BF16), pltpu.VMEM((d, f), BF16), pltpu.VMEM((f, d), BF16),
            pltpu.VMEM((2, d, FFN_CHUNK), F32), pltpu.VMEM((2, d, FFN_CHUNK), F32),
            pltpu.VMEM((2, FFN_CHUNK, d), F32), pltpu.SemaphoreType.DMA((3, 2))]


def _tile_index(t, n_tiles, per_batch):
    tt = jnp.minimum(t, n_tiles - 1)
    return tt // per_batch, tt % per_batch


def _pool_steps(i, x_ref, g_ref, win_ref, wgrp_ref, scale_ref, wout_ref, ext_ref, x1_ref, *, ts, halo):
    d_model = x_ref.shape[-1]
    gdim = d_model // len(POOL_WINDOWS)
    h = _rmsnorm(x_ref[...], g_ref[...]).astype(BF16)
    ext_ref[halo:halo + ts, :] = jnp.dot(h, win_ref[...].astype(BF16), preferred_element_type=F32)
    yield
    pos = i * ts + lax.broadcasted_iota(jnp.int32, (ts, 1), 0)
    zs = []
    for g, w in enumerate(POOL_WINDOWS):
        cols = slice(g * gdim, (g + 1) * gdim)
        s = ext_ref[:, cols]
        k = 1
        while k < w:
            s = s + pltpu.roll(s, k, axis=0)
            k *= 2
        inv_cnt = 1.0 / jnp.minimum(pos + 1, w).astype(F32)
        p = s[halo:, :] * inv_cnt - ext_ref[halo:halo + ts, cols]
        zs.append(jnp.dot(p.astype(BF16), wgrp_ref[g].astype(BF16), preferred_element_type=F32))
    yield
    z = jnp.concatenate(zs, axis=1) * scale_ref[...]
    y = jnp.dot(z.astype(BF16), wout_ref[...].astype(BF16), preferred_element_type=F32)
    ext_ref[0:halo, :] = ext_ref[ts:ts + halo, :]
    x1_ref[...] = x_ref[...] + y
    yield


def _pool_ffn_kernel(x_ref, g_ref, win_ref, wgrp_ref, scale_ref, wout_ref, gf_ref, *refs,
                     layer, ts, halo, n_tiles, per_batch):
    hbm_refs, (o_ref, ext_ref, x1_ref) = refs[:3], refs[3:6]
    (wg_ref, wu_ref, wd_ref), stage_refs, sem = refs[6:9], refs[9:12], refs[12]
    t = pl.program_id(0)
    i = jnp.minimum(t, n_tiles - 1) % per_batch

    @pl.when(i == 0)
    def _():
        ext_ref[0:halo, :] = jnp.zeros((halo, x_ref.shape[-1]), F32)

    make_mixer = functools.partial(_pool_steps, i, x_ref, g_ref, win_ref, wgrp_ref, scale_ref, wout_ref,
                                   ext_ref, x1_ref, ts=ts, halo=halo)
    make_ffn = functools.partial(_swiglu_steps, x1_ref, o_ref, gf_ref, wg_ref, wu_ref, wd_ref, lambda acc: acc)
    load_weights = functools.partial(_load_ffn_weights, layer, hbm_refs, (wg_ref, wu_ref, wd_ref), stage_refs, sem)
    _skewed_step(t, n_tiles, make_mixer, make_ffn, load_weights)


def _pool_ffn_layer(x, g_mix, w_in, w_group, scale, w_out, g_ffn, w_gate, w_up, w_down, *, layer, ts=TOKEN_TILE):
    b, s, d = x.shape
    halo = max(POOL_WINDOWS)
    per_batch = s // ts
    n_tiles = b * per_batch
    kern = functools.partial(_pool_ffn_kernel, layer=layer, ts=ts, halo=halo, n_tiles=n_tiles, per_batch=per_batch)
    mixer_tile = pl.BlockSpec((None, ts, d), lambda t: (*_tile_index(t, n_tiles, per_batch), 0))
    ffn_tile = pl.BlockSpec((None, ts, d), lambda t: (*_tile_index(jnp.maximum(t - 1, 0), n_tiles, per_batch), 0))
    return pl.pallas_call(
        kern,
        grid=(n_tiles + 1,),
        in_specs=[mixer_tile, _const_spec((1, d)), _layer_spec(w_in, layer), _layer_spec(w_group, layer),
                  _const_spec((1, d)), _layer_spec(w_out, layer)] + _ffn_specs(d),
        out_specs=ffn_tile,
        out_shape=jax.ShapeDtypeStruct(x.shape, F32),
        scratch_shapes=[pltpu.VMEM((ts + halo, d), F32), pltpu.VMEM((ts, d), F32)]
        + _ffn_scratch(d, w_gate.shape[-1]),
        compiler_params=_params("arbitrary"),
        name="pool_ffn",
    )(x, g_mix.reshape(1, d), w_in, w_group, scale.reshape(1, d), w_out, g_ffn.reshape(1, d), w_gate, w_up, w_down)


def _class_order(d):
    order, cur = (0,), 1
    while cur < d:
        order = tuple(c + cur * r for c in order for r in range(DEINT_STRIDE))
        cur *= DEINT_STRIDE
    assert cur == d, "dilations must be powers of DEINT_STRIDE"
    return order


def _split_rows(src, n_blocks, blk):
    n = blk // DEINT_STRIDE
    return jnp.concatenate([src[pl.ds(b0 * blk + r, n, stride=DEINT_STRIDE), :]
                            for b0 in range(n_blocks) for r in range(DEINT_STRIDE)], axis=0)


def _regroup_qkv_weights(w_ref, wb_refs, groups):
    d_model, d_attn = w_ref.shape[0], w_ref.shape[1] // 3
    half = LANES // 2
    lane = lax.broadcasted_iota(jnp.int32, (1, LANES), 1)

    def body(rb, carry):
        rows = pl.ds(pl.multiple_of(rb * LANES, LANES), LANES)

        def half_of(src, upper):
            ci = src // LANES
            v = w_ref[rows, ci * LANES:(ci + 1) * LANES]
            if src < d_attn:
                v = v * HEAD_DIM ** -0.5
            return v if (src % LANES == half) == upper else pltpu.roll(v, half, axis=1)

        col = 0
        for wb_ref, (_, cg) in zip(wb_refs, groups):
            runs = [p * d_attn + col + i * half for p in range(3) for i in range(cg // half)]
            for k in range(0, len(runs), 2):
                lo = half_of(runs[k], False)
                if k + 1 < len(runs):
                    chunk = jnp.where(lane < half, lo, half_of(runs[k + 1], True))
                    wb_ref[rows, k * half:(k + 2) * half] = chunk.astype(BF16)
                else:
                    wb_ref[rows, k * half:(k + 1) * half] = lo[:, :half].astype(BF16)
            col += cg
        return carry

    lax.fori_loop(0, d_model // LANES, body, 0)


def _qkv_kernel(x_ref, g_ref, cos_ref, sin_ref, w_ref, *refs, ts, sub, groups):
    ng = len(groups)
    out_refs, w_refs = refs[:3 * ng], refs[3 * ng:4 * ng]
    hs_all, tab_all = refs[4 * ng:]
    lane = lax.broadcasted_iota(jnp.int32, (1, LANES), 1)
    first_half = (lane % HEAD_DIM) < (HEAD_DIM // 2)
    n_chunks = x_ref.shape[-1] // LANES

    @pl.when(jnp.logical_and(pl.program_id(0) == 0, pl.program_id(1) == 0))
    def _():
        _regroup_qkv_weights(w_ref, w_refs, groups)

    for si, s0 in enumerate(range(0, ts, sub)):
        hs_ref, tab_ref = hs_all.at[si], tab_all.at[si]
        h = _rmsnorm(x_ref[s0:s0 + sub, :], g_ref[...])
        cs, sn = cos_ref[s0:s0 + sub, :], sin_ref[s0:s0 + sub, :]
        cur_d = 1
        for gi, (d, cg) in enumerate(groups):
            while cur_d < d:
                for c in range(n_chunks):
                    hs_ref[c] = h[:, c * LANES:(c + 1) * LANES]
                tab_ref[0], tab_ref[1] = cs, sn
                blk = sub // cur_d
                h = jnp.concatenate([_split_rows(hs_ref.at[c], cur_d, blk) for c in range(n_chunks)], axis=1)
                cs, sn = _split_rows(tab_ref.at[0], cur_d, blk), _split_rows(tab_ref.at[1], cur_d, blk)
                cur_d *= DEINT_STRIDE
            assert cur_d == d, "head groups must come in increasing powers of DEINT_STRIDE"
            qkv = jnp.dot(h.astype(BF16), w_refs[gi][...], preferred_element_type=F32)
            chunks = []
            for c in range(2 * cg // LANES):
                t = qkv[:, c * LANES:(c + 1) * LANES]
                swapped = jnp.where(first_half, pltpu.roll(t, LANES - HEAD_DIM // 2, axis=1),
                                    pltpu.roll(t, HEAD_DIM // 2, axis=1))
                chunks.append(t * cs + swapped * sn)
            qk = jnp.concatenate(chunks, axis=1)
            parts = (qk[:, :cg], qk[:, cg:2 * cg], qkv[:, 2 * cg:])
            n = sub // d
            for part, o_ref in zip(parts, out_refs[3 * gi:3 * gi + 3]):
                pb = part.astype(BF16)
                for slot, r in enumerate(_class_order(d)):
                    o_ref[r, si * n:(si + 1) * n, :] = pb[slot * n:(slot + 1) * n, :]


def _qkv_layer(x, g, cos, sin, w_qkv, groups, *, layer, ts=QKV_TILE, sub=QKV_SUB_TILE):
    b, s, d_model = x.shape
    kern = functools.partial(_qkv_kernel, ts=ts, sub=sub, groups=groups)
    out_shapes, out_specs = [], []
    for d, cg in groups:
        for _ in range(3):
            out_shapes.append(jax.ShapeDtypeStruct((b, d, s // d, cg), BF16))
            out_specs.append(pl.BlockSpec((None, d, ts // d, cg), lambda bi, i: (bi, 0, i, 0)))
    table_spec = pl.BlockSpec((ts, LANES), lambda bi, i: (i, 0))
    return pl.pallas_call(
        kern,
        grid=(b, s // ts),
        in_specs=[pl.BlockSpec((None, ts, d_model), lambda bi, i: (bi, i, 0)), _const_spec((1, d_model)),
                  table_spec, table_spec, _layer_spec(w_qkv, layer)],
        out_specs=out_specs,
        out_shape=out_shapes,
        scratch_shapes=[pltpu.VMEM((d_model, 3 * cg), BF16) for _, cg in groups]
        + [pltpu.VMEM((ts // sub, d_model // LANES, sub, LANES), F32),
           pltpu.VMEM((ts // sub, 2, sub, LANES), F32)],
        compiler_params=_params("arbitrary", "arbitrary"),
        name="qkv_rope",
    )(x, g.reshape(1, d_model), cos, sin, w_qkv)


def _band_masks(w, has_prev):
    qi = lax.broadcasted_iota(jnp.int32, (w, 2 * w), 0)
    kj = lax.broadcasted_iota(jnp.int32, (w, 2 * w), 1)
    band = jnp.logical_and(kj >= qi, kj <= qi + w)
    return band, jnp.logical_and(band, jnp.logical_or(kj >= w, has_prev))


def _attn_segment(masks, q_ref, k_ref, v_ref, kp_ref, vp_ref, o_ref, lse_ref, lse_scr, *, w, n_heads):
    band, band_first = masks
    tq = q_ref.shape[0]
    cg = n_heads * HEAD_DIM
    nsub = tq // w
    lane = lax.broadcasted_iota(jnp.int32, (1, LANES), 1)
    nt = (((1,), (1,)), ((), ()))

    units = []
    for c0 in range(0, cg, LANES):
        width = min(LANES, cg - c0)
        for hh in range(width // HEAD_DIM):
            if width == LANES:
                sel = (lane < HEAD_DIM) if hh == 0 else (lane >= HEAD_DIM)
            else:
                sel = None
            units.append((len(units), slice(c0, c0 + width), hh, sel))

    def key_rows(ref, prev_ref, cols, j):
        if j == 0:
            return jnp.concatenate([prev_ref[:, cols], ref[0:w, cols]], axis=0)
        return ref[(j - 1) * w:(j + 1) * w, cols]

    def scores(item):
        (_, cols, _, sel), j = item
        qb = q_ref[j * w:(j + 1) * w, cols]
        if sel is not None:
            qb = jnp.where(sel, qb, jnp.zeros_like(qb))
        return lax.dot_general(qb, key_rows(k_ref, kp_ref, cols, j), nt, preferred_element_type=F32)

    def finish(item, first, last, sc):
        (h, cols, hh, _), j = item
        if first:
            lse_scr[0:tq, :] = jnp.full((tq, LANES), NEG_INF, F32)
        lo = hh * HEAD_DIM
        rows = slice(j * w, (j + 1) * w)
        s = jnp.where(band_first if j == 0 else band, sc, NEG_INF)
        m = jnp.max(s, axis=-1, keepdims=True)
        e = jnp.exp(s - m)
        den = jnp.sum(e, axis=-1, keepdims=True)
        pv = jnp.dot(e.astype(BF16), key_rows(v_ref, vp_ref, cols, j), preferred_element_type=F32) / den
        o_ref[rows, cols.start + lo:cols.start + lo + HEAD_DIM] = pv[:, lo:lo + HEAD_DIM].astype(BF16)
        lse_scr[rows, :] = jnp.where(lane == h, m + jnp.log(den), lse_scr[rows, :])
        if last:
            lse = lse_scr[0:tq, :]
            mx = jnp.max(lse, axis=-1, keepdims=True)
            tot = jnp.sum(jnp.exp(lse - mx), axis=-1, keepdims=True)
            lse_ref[...] = jnp.broadcast_to(mx + jnp.log(tot) - math.log(n_heads), (tq, LANES))

    items = [(unit, j) for unit in units for j in range(nsub)]
    return [(functools.partial(scores, item), functools.partial(finish, item, n == 0, n == len(items) - 1))
            for n, item in enumerate(items)]


def _run_lookahead(work):
    pending = [scores() for scores, _ in work[:ATTN_LOOKAHEAD]]
    for n, (_, finish) in enumerate(work):
        if n + ATTN_LOOKAHEAD < len(work):
            pending.append(work[n + ATTN_LOOKAHEAD][0]())
        finish(pending.pop(0))


def _attn_kernel(q_ref, k_ref, v_ref, kp_ref, vp_ref, o_ref, lse_ref, lse_scr, *, w, n_heads):
    masks = _band_masks(w, pl.program_id(1) > 0)
    work = []
    for sq in range(q_ref.shape[0]):
        work += _attn_segment(masks, q_ref.at[sq], k_ref.at[sq], v_ref.at[sq], kp_ref.at[sq], vp_ref.at[sq],
                              o_ref.at[sq], lse_ref.at[sq], lse_scr.at[sq], w=w, n_heads=n_heads)
    _run_lookahead(work)


def _attn_group(q, k, v, *, w, n_heads, rows=ATTN_STEP_ROWS):
    nseq, l, cg = q.shape
    tq = min(rows, l)
    nb = min(rows // tq, ATTN_STEP_SEQS)
    ratio = tq // w
    kern = functools.partial(_attn_kernel, w=w, n_heads=n_heads)
    cur = pl.BlockSpec((nb, tq, cg), lambda n, i: (n, i, 0))
    prev = pl.BlockSpec((nb, w, cg), lambda n, i: (n, jnp.maximum(i * ratio - 1, 0), 0))
    return pl.pallas_call(
        kern,
        grid=(nseq // nb, l // tq),
        in_specs=[cur, cur, cur, prev, prev],
        out_specs=[cur, pl.BlockSpec((nb, tq, LANES), lambda n, i: (n, i, 0))],
        out_shape=[jax.ShapeDtypeStruct((nseq, l, cg), BF16),
                   jax.ShapeDtypeStruct((nseq, l, LANES), F32)],
        scratch_shapes=[pltpu.VMEM((nb, tq, LANES), F32)],
        compiler_params=_params("arbitrary", "arbitrary"),
        name=f"attn_h{n_heads}_l{l}",
    )(q, k, v, k, v)


def _lane_chunks(blk, cg):
    chunks = []
    for c0 in range(0, cg, LANES):
        piece = blk[:, c0:min(c0 + LANES, cg)]
        if piece.shape[1] < LANES:
            piece = jnp.concatenate([piece, jnp.zeros((blk.shape[0], LANES - piece.shape[1]), F32)], axis=1)
        chunks.append(piece)
    return chunks


def _merge_steps(x_ref, o_refs, l_refs, wout_ref, scratch, x1_ref, *, ts, groups):
    ng = len(groups)
    outs, lses = [], []
    si = 0
    for gi, (d, cg) in enumerate(groups):
        if d == 1:
            outs.append(_lane_chunks(o_refs[gi][0].astype(F32), cg))
            lses.append(l_refs[gi][0])
            continue
        n = ts // d
        o_scr, l_scr = scratch[si], scratch[si + 1]
        si += 2
        for r in range(d):
            for c, piece in enumerate(_lane_chunks(o_refs[gi][r].astype(F32), cg)):
                o_scr[c, pl.ds(r, n, stride=d), :] = piece
            l_scr[pl.ds(r, n, stride=d), :] = l_refs[gi][r]
        outs.append([o_scr[c] for c in range(o_scr.shape[0])])
        lses.append(l_scr[...])
    m = functools.reduce(jnp.maximum, lses)
    es = [jnp.exp(l - m) for l in lses]
    inv = float(ng) / functools.reduce(lambda a, b2: a + b2, es)
    lane = lax.broadcasted_iota(jnp.int32, (1, LANES), 1)
    full, half = [], None
    for (d, cg), chunks, e in zip(groups, outs, es):
        alpha = (e * inv)[:, :1]
        for c, chunk in enumerate(chunks):
            chunk = chunk * alpha
            whole = cg - c * LANES >= LANES
            if half is None:
                if whole:
                    full.append(chunk)
                else:
                    half = chunk
            else:
                rot = pltpu.roll(chunk, HEAD_DIM, axis=1)
                full.append(jnp.where(lane < HEAD_DIM, half, rot))
                half = rot if whole else None
    assert half is None, "attention width must be a multiple of 128"
    merged = jnp.concatenate(full, axis=1).astype(BF16)
    yield
    x1_ref[...] = x_ref[...] + jnp.dot(merged, wout_ref[...].astype(BF16), preferred_element_type=F32)
    yield


def _merge_ffn_kernel(x_ref, *refs, layer, ts, groups, n_tiles):
    ng = len(groups)
    o_refs, l_refs = refs[:ng], refs[ng:2 * ng]
    wout_ref, gf_ref = refs[2 * ng:2 * ng + 2]
    hbm_refs = refs[2 * ng + 2:2 * ng + 5]
    gfin_ref, y_ref, x1_ref = refs[2 * ng + 5:2 * ng + 8]
    (wg_ref, wu_ref, wd_ref), stage_refs, sem = refs[2 * ng + 8:2 * ng + 11], refs[2 * ng + 11:2 * ng + 14], refs[2 * ng + 14]
    scratch = refs[2 * ng + 15:]
    t = pl.program_id(0)
    make_mixer = functools.partial(_merge_steps, x_ref, o_refs, l_refs, wout_ref, scratch, x1_ref,
                                   ts=ts, groups=groups)
    make_ffn = functools.partial(_swiglu_steps, x1_ref, y_ref, gf_ref, wg_ref, wu_ref, wd_ref,
                                 lambda acc: _rmsnorm(acc, gfin_ref[...]))
    load_weights = functools.partial(_load_ffn_weights, layer, hbm_refs, (wg_ref, wu_ref, wd_ref), stage_refs, sem)
    _skewed_step(t, n_tiles, make_mixer, make_ffn, load_weights)


def _merge_ffn_layer(x, outs, lses, w_out, g_ffn, w_gate, w_up, w_down, g_final, groups, *,
                     attn_layer, layer, ts=TOKEN_TILE):
    b, s, d_model = x.shape
    per_batch = s // ts
    n_tiles = b * per_batch
    kern = functools.partial(_merge_ffn_kernel, layer=layer, ts=ts, groups=groups, n_tiles=n_tiles)

    def mixer_spec(block):
        lead = len(block) - 2
        return pl.BlockSpec(block, lambda t: (_tile_index(t, n_tiles, per_batch)[0],) + (0,) * (lead - 1)
                            + (_tile_index(t, n_tiles, per_batch)[1], 0))

    ffn_tile = pl.BlockSpec((None, ts, d_model),
                            lambda t: (*_tile_index(jnp.maximum(t - 1, 0), n_tiles, per_batch), 0))
    o_specs = [mixer_spec((None, d, ts // d, cg)) for d, cg in groups]
    l_specs = [mixer_spec((None, d, ts // d, LANES)) for d, cg in groups]
    scratch = [pltpu.VMEM((ts, d_model), F32)] + _ffn_scratch(d_model, w_gate.shape[-1])
    for d, cg in groups:
        if d > 1:
            scratch += [pltpu.VMEM((pl.cdiv(cg, LANES), ts, LANES), F32), pltpu.VMEM((ts, LANES), F32)]
    return pl.pallas_call(
        kern,
        grid=(n_tiles + 1,),
        in_specs=[mixer_spec((None, ts, d_model))] + o_specs + l_specs + [_layer_spec(w_out, attn_layer)]
        + _ffn_specs(d_model) + [_const_spec((1, d_model))],
        out_specs=ffn_tile,
        out_shape=jax.ShapeDtypeStruct(x.shape, F32),
        scratch_shapes=scratch,
        compiler_params=_params("arbitrary"),
        name="attn_merge_ffn_final",
    )(x, *outs, *lses, w_out, g_ffn.reshape(1, d_model), w_gate, w_up, w_down, g_final.reshape(1, d_model))


def _rope_tables(s):
    half = HEAD_DIM // 2
    inv_freq = 1.0 / (ROPE_THETA ** (np.arange(0, HEAD_DIM, 2, dtype=np.float64) / HEAD_DIM))
    ang = np.arange(s, dtype=np.float64)[:, None] * inv_freq[None, :]
    cos, sin = np.cos(ang), np.sin(ang)
    reps = LANES // half
    cos_t = np.concatenate([cos] * reps, axis=-1)
    sin_t = np.concatenate([-sin, sin] * (reps // 2), axis=-1)
    return jnp.asarray(cos_t, F32), jnp.asarray(sin_t, F32)


def kernel(x, norm_mix, norm_ffn, norm_final, pool_w_in, pool_w_group, pool_scale, pool_w_out,
           attn_w_qkv, attn_w_out, ffn_w_gate, ffn_w_up, ffn_w_down):
    b, s, d_model = x.shape
    assert norm_mix.shape[0] == 2, "two layers: pooling mixer then dilated attention"
    n_heads = attn_w_out.shape[1] // HEAD_DIM
    head_groups = _head_groups(n_heads)
    groups = tuple((dil, nh * HEAD_DIM) for (_, dil), nh in zip(ATTN_PATTERNS, head_groups))
    assert x.dtype == F32 and d_model % LANES == 0 and (n_heads * HEAD_DIM) % LANES == 0
    assert s % TOKEN_TILE == 0 and s % QKV_TILE == 0 and ffn_w_gate.shape[-1] % FFN_CHUNK == 0
    assert all(win // dil == LANES and (s // dil) % LANES == 0 and QKV_SUB_TILE % (BF16_ROWS * dil) == 0
               for win, dil in ATTN_PATTERNS), "banded attention works on 128-row class blocks"

    x = _pool_ffn_layer(x, norm_mix[0], pool_w_in, pool_w_group, pool_scale[0], pool_w_out,
                        norm_ffn[0], ffn_w_gate, ffn_w_up, ffn_w_down, layer=0)

    cos_t, sin_t = _rope_tables(s)
    qkv = _qkv_layer(x, norm_mix[1], cos_t, sin_t, attn_w_qkv, groups, layer=0)
    outs, lses = [], []
    for gi, ((window, dil), nh) in enumerate(zip(ATTN_PATTERNS, head_groups)):
        cg = nh * HEAD_DIM
        q, k, v = (t.reshape(b * dil, s // dil, cg) for t in qkv[3 * gi:3 * gi + 3])
        o, lse = _attn_group(q, k, v, w=window // dil, n_heads=nh)
        outs.append(o.reshape(b, dil, s // dil, cg))
        lses.append(lse.reshape(b, dil, s // dil, LANES))
    return _merge_ffn_layer(x, outs, lses, attn_w_out, norm_ffn[1], ffn_w_gate, ffn_w_up, ffn_w_down,
                            norm_final, groups, attn_layer=0, layer=1)
```

```python
import functools
import math

import jax
import jax.numpy as jnp
import numpy as np
from jax import lax
from jax.experimental import pallas as pl
from jax.experimental.pallas import tpu as pltpu

EPS = 1e-6
POOL_WINDOWS = (2, 4, 8, 16)
HEAD_DIM = 64
ATTN_PATTERNS = ((128, 1), (512, 4), (2048, 16))
ROPE_THETA = 10000.0
NEG_INF = -1e30

LANES = 128
BF16_ROWS = 16
VMEM_LIMIT_BYTES = 58 * 1024 * 1024
TOKEN_TILE = 512
QKV_TILE = 1024
QKV_SUB_TILE = 256
ATTN_STEP_ROWS = 2048
ATTN_STEP_SEQS = 2
ATTN_LOOKAHEAD = 2
FFN_CHUNK = 256
MIXER_GAP = 3
DEINT_STRIDE = 4

F32 = jnp.float32
BF16 = jnp.bfloat16


def _head_groups(n_heads):
    n = len(ATTN_PATTERNS)
    return tuple(n_heads // n + (1 if g < n_heads % n else 0) for g in range(n))


def _rmsnorm(x, g):
    ms = jnp.mean(x * x, axis=-1, keepdims=True)
    return x * lax.rsqrt(ms + EPS) * g


def _params(*sem):
    return pltpu.CompilerParams(dimension_semantics=sem, vmem_limit_bytes=VMEM_LIMIT_BYTES)


def _const_spec(shape):
    zeros = (0,) * len(shape)
    return pl.BlockSpec(shape, lambda *_: zeros, pipeline_mode=pl.Buffered(1))


def _layer_spec(stacked, layer):
    idx = (layer,) + (0,) * (stacked.ndim - 1)
    return pl.BlockSpec((None,) + stacked.shape[1:], lambda *_: idx, pipeline_mode=pl.Buffered(1))


def _swiglu_steps(x1_ref, o_ref, g_ref, wg_ref, wu_ref, wd_ref, finish):
    x1 = x1_ref[...]
    h = _rmsnorm(x1, g_ref[...]).astype(BF16)
    n_chunks = wg_ref.shape[-1] // FFN_CHUNK
    acts = []
    for c in range(n_chunks):
        sl = slice(c * FFN_CHUNK, (c + 1) * FFN_CHUNK)
        gate = jnp.dot(h, wg_ref[:, sl], preferred_element_type=F32)
        up = jnp.dot(h, wu_ref[:, sl], preferred_element_type=F32)
        acts.append((gate * jax.nn.sigmoid(gate) * up).astype(BF16))
        if c == n_chunks - 1:
            down = jnp.dot(jnp.concatenate(acts, axis=1), wd_ref[...], preferred_element_type=F32)
            o_ref[...] = finish(x1 + down)
        yield


def _load_ffn_weights(layer, hbm_refs, bf_refs, stage_refs, sem, meanwhile):
    n_chunks = hbm_refs[0].shape[-1] // FFN_CHUNK

    def chunk(c):
        return slice(c * FFN_CHUNK, (c + 1) * FFN_CHUNK)

    def copies(c):
        srcs = (hbm_refs[0].at[layer, :, chunk(c)], hbm_refs[1].at[layer, :, chunk(c)],
                hbm_refs[2].at[layer, chunk(c), :])
        return [pltpu.make_async_copy(src, stage.at[c % 2], sem.at[k, c % 2])
                for k, (src, stage) in enumerate(zip(srcs, stage_refs))]

    for cp in copies(0):
        cp.start()
    meanwhile()
    for c in range(n_chunks):
        if c + 1 < n_chunks:
            for cp in copies(c + 1):
                cp.start()
        for cp in copies(c):
            cp.wait()
        bf_refs[0][:, chunk(c)] = stage_refs[0][c % 2].astype(BF16)
        bf_refs[1][:, chunk(c)] = stage_refs[1][c % 2].astype(BF16)
        bf_refs[2][chunk(c), :] = stage_refs[2][c % 2].astype(BF16)


def _skewed_step(t, n_tiles, make_mixer, make_ffn, load_weights):
    @pl.when(t == 0)
    def _():
        def run_mixer():
            for _ in make_mixer():
                pass
        load_weights(run_mixer)

    @pl.when(t == n_tiles)
    def _():
        for _ in make_ffn():
            pass

    @pl.when(jnp.logical_and(t > 0, t < n_tiles))
    def _():
        mixer = make_mixer()
        for c, _ in enumerate(make_ffn()):
            if c % MIXER_GAP == MIXER_GAP - 1:
                next(mixer, None)
        for _ in mixer:
            pass


def _ffn_specs(d):
    hbm = pl.BlockSpec(memory_space=pl.ANY)
    return [_const_spec((1, d)), hbm, hbm, hbm]


def _ffn_scratch(d, f):
    return [pltpu.VMEM((d, f), BF16), pltpu.VMEM((d, f), BF16), pltpu.VMEM((f, d), BF16),
            pltpu.VMEM((2, d, FFN_CHUNK), F32), pltpu.VMEM((2, d, FFN_CHUNK), F32),
            pltpu.VMEM((2, FFN_CHUNK, d), F32), pltpu.SemaphoreType.DMA((3, 2))]


def _tile_index(t, n_tiles, per_batch):
    tt = jnp.minimum(t, n_tiles - 1)
    return tt // per_batch, tt % per_batch


def _pool_steps(i, x_ref, g_ref, win_ref, wgrp_ref, scale_ref, wout_ref, ext_ref, x1_ref, *, ts, halo):
    d_model = x_ref.shape[-1]
    gdim = d_model // len(POOL_WINDOWS)
    h = _rmsnorm(x_ref[...], g_ref[...]).astype(BF16)
    ext_ref[halo:halo + ts, :] = jnp.dot(h, win_ref[...].astype(BF16), preferred_element_type=F32)
    yield
    pos = i * ts + lax.broadcasted_iota(jnp.int32, (ts, 1), 0)
    zs = []
    for g, w in enumerate(POOL_WINDOWS):
        cols = slice(g * gdim, (g + 1) * gdim)
        s = ext_ref[:, cols]
        k = 1
        while k < w:
            s = s + pltpu.roll(s, k, axis=0)
            k *= 2
        inv_cnt = 1.0 / jnp.minimum(pos + 1, w).astype(F32)
        p = s[halo:, :] * inv_cnt - ext_ref[halo:halo + ts, cols]
        zs.append(jnp.dot(p.astype(BF16), wgrp_ref[g].astype(BF16), preferred_element_type=F32))
    yield
    z = jnp.concatenate(zs, axis=1) * scale_ref[...]
    y = jnp.dot(z.astype(BF16), wout_ref[...].astype(BF16), preferred_element_type=F32)
    ext_ref[0:halo, :] = ext_ref[ts:ts + halo, :]
    x1_ref[...] = x_ref[...] + y
    yield


def _pool_ffn_kernel(x_ref, g_ref, win_ref, wgrp_ref, scale_ref, wout_ref, gf_ref, *refs,
                     layer, ts, halo, n_tiles, per_batch):
    hbm_refs, (o_ref, ext_ref, x1_ref) = refs[:3], refs[3:6]
    (wg_ref, wu_ref, wd_ref), stage_refs, sem = refs[6:9], refs[9:12], refs[12]
    t = pl.program_id(0)
    i = jnp.minimum(t, n_tiles - 1) % per_batch

    @pl.when(i == 0)
    def _():
        ext_ref[0:halo, :] = jnp.zeros((halo, x_ref.shape[-1]), F32)

    make_mixer = functools.partial(_pool_steps, i, x_ref, g_ref, win_ref, wgrp_ref, scale_ref, wout_ref,
                                   ext_ref, x1_ref, ts=ts, halo=halo)
    make_ffn = functools.partial(_swiglu_steps, x1_ref, o_ref, gf_ref, wg_ref, wu_ref, wd_ref, lambda acc: acc)
    load_weights = functools.partial(_load_ffn_weights, layer, hbm_refs, (wg_ref, wu_ref, wd_ref), stage_refs, sem)
    _skewed_step(t, n_tiles, make_mixer, make_ffn, load_weights)


def _pool_ffn_layer(x, g_mix, w_in, w_group, scale, w_out, g_ffn, w_gate, w_up, w_down, *, layer, ts=TOKEN_TILE):
    b, s, d = x.shape
    halo = max(POOL_WINDOWS)
    per_batch = s // ts
    n_tiles = b * per_batch
    kern = functools.partial(_pool_ffn_kernel, layer=layer, ts=ts, halo=halo, n_tiles=n_tiles, per_batch=per_batch)
    mixer_tile = pl.BlockSpec((None, ts, d), lambda t: (*_tile_index(t, n_tiles, per_batch), 0))
    ffn_tile = pl.BlockSpec((None, ts, d), lambda t: (*_tile_index(jnp.maximum(t - 1, 0), n_tiles, per_batch), 0))
    return pl.pallas_call(
        kern,
        grid=(n_tiles + 1,),
        in_specs=[mixer_tile, _const_spec((1, d)), _layer_spec(w_in, layer), _layer_spec(w_group, layer),
                  _const_spec((1, d)), _layer_spec(w_out, layer)] + _ffn_specs(d),
        out_specs=ffn_tile,
        out_shape=jax.ShapeDtypeStruct(x.shape, F32),
        scratch_shapes=[pltpu.VMEM((ts + halo, d), F32), pltpu.VMEM((ts, d), F32)]
        + _ffn_scratch(d, w_gate.shape[-1]),
        compiler_params=_params("arbitrary"),
        name="pool_ffn",
    )(x, g_mix.reshape(1, d), w_in, w_group, scale.reshape(1, d), w_out, g_ffn.reshape(1, d), w_gate, w_up, w_down)


def _class_order(d):
    order, cur = (0,), 1
    while cur < d:
        order = tuple(c + cur * r for c in order for r in range(DEINT_STRIDE))
        cur *= DEINT_STRIDE
    assert cur == d, "dilations must be powers of DEINT_STRIDE"
    return order


def _split_rows(src, n_blocks, blk):
    n = blk // DEINT_STRIDE
    return jnp.concatenate([src[pl.ds(b0 * blk + r, n, stride=DEINT_STRIDE), :]
                            for b0 in range(n_blocks) for r in range(DEINT_STRIDE)], axis=0)


def _regroup_qkv_weights(w_ref, wb_refs, groups):
    d_model, d_attn = w_ref.shape[0], w_ref.shape[1] // 3
    half = LANES // 2
    lane = lax.broadcasted_iota(jnp.int32, (1, LANES), 1)

    def body(rb, carry):
        rows = pl.ds(pl.multiple_of(rb * LANES, LANES), LANES)

        def half_of(src, upper):
            ci = src // LANES
            v = w_ref[rows, ci * LANES:(ci + 1) * LANES]
            if src < d_attn:
                v = v * HEAD_DIM ** -0.5
            return v if (src % LANES == half) == upper else pltpu.roll(v, half, axis=1)

        col = 0
        for wb_ref, (_, cg) in zip(wb_refs, groups):
            runs = [p * d_attn + col + i * half for p in range(3) for i in range(cg // half)]
            for k in range(0, len(runs), 2):
                lo = half_of(runs[k], False)
                if k + 1 < len(runs):
                    chunk = jnp.where(lane < half, lo, half_of(runs[k + 1], True))
                    wb_ref[rows, k * half:(k + 2) * half] = chunk.astype(BF16)
                else:
                    wb_ref[rows, k * half:(k + 1) * half] = lo[:, :half].astype(BF16)
            col += cg
        return carry

    lax.fori_loop(0, d_model // LANES, body, 0)


def _qkv_kernel(x_ref, g_ref, cos_ref, sin_ref, w_ref, *refs, ts, sub, groups):
    ng = len(groups)
    out_refs, w_refs = refs[:3 * ng], refs[3 * ng:4 * ng]
    hs_all, tab_all = refs[4 * ng:]
    lane = lax.broadcasted_iota(jnp.int32, (1, LANES), 1)
    first_half = (lane % HEAD_DIM) < (HEAD_DIM // 2)
    n_chunks = x_ref.shape[-1] // LANES

    @pl.when(jnp.logical_and(pl.program_id(0) == 0, pl.program_id(1) == 0))
    def _():
        _regroup_qkv_weights(w_ref, w_refs, groups)

    for si, s0 in enumerate(range(0, ts, sub)):
        hs_ref, tab_ref = hs_all.at[si], tab_all.at[si]
        h = _rmsnorm(x_ref[s0:s0 + sub, :], g_ref[...])
        cs, sn = cos_ref[s0:s0 + sub, :], sin_ref[s0:s0 + sub, :]
        cur_d = 1
        for gi, (d, cg) in enumerate(groups):
            while cur_d < d:
                for c in range(n_chunks):
                    hs_ref[c] = h[:, c * LANES:(c + 1) * LANES]
                tab_ref[0], tab_ref[1] = cs, sn
                blk = sub // cur_d
                h = jnp.concatenate([_split_rows(hs_ref.at[c], cur_d, blk) for c in range(n_chunks)], axis=1)
                cs, sn = _split_rows(tab_ref.at[0], cur_d, blk), _split_rows(tab_ref.at[1], cur_d, blk)
                cur_d *= DEINT_STRIDE
            assert cur_d == d, "head groups must come in increasing powers of DEINT_STRIDE"
            qkv = jnp.dot(h.astype(BF16), w_refs[gi][...], preferred_element_type=F32)
            chunks = []
            for c in range(2 * cg // LANES):
                t = qkv[:, c * LANES:(c + 1) * LANES]
                swapped = jnp.where(first_half, pltpu.roll(t, LANES - HEAD_DIM // 2, axis=1),
                                    pltpu.roll(t, HEAD_DIM // 2, axis=1))
                chunks.append(t * cs + swapped * sn)
            qk = jnp.concatenate(chunks, axis=1)
            parts = (qk[:, :cg], qk[:, cg:2 * cg], qkv[:, 2 * cg:])
            n = sub // d
            for part, o_ref in zip(parts, out_refs[3 * gi:3 * gi + 3]):
                pb = part.astype(BF16)
                for slot, r in enumerate(_class_order(d)):
                    o_ref[r, si * n:(si + 1) * n, :] = pb[slot * n:(slot + 1) * n, :]


def _qkv_layer(x, g, cos, sin, w_qkv, groups, *, layer, ts=QKV_TILE, sub=QKV_SUB_TILE):
    b, s, d_model = x.shape
    kern = functools.partial(_qkv_kernel, ts=ts, sub=sub, groups=groups)
    out_shapes, out_specs = [], []
    for d, cg in groups:
        for _ in range(3):
            out_shapes.append(jax.ShapeDtypeStruct((b, d, s // d, cg), BF16))
            out_specs.append(pl.BlockSpec((None, d, ts // d, cg), lambda bi, i: (bi, 0, i, 0)))
    table_spec = pl.BlockSpec((ts, LANES), lambda bi, i: (i, 0))
    return pl.pallas_call(
        kern,
        grid=(b, s // ts),
        in_specs=[pl.BlockSpec((None, ts, d_model), lambda bi, i: (bi, i, 0)), _const_spec((1, d_model)),
                  table_spec, table_spec, _layer_spec(w_qkv, layer)],
        out_specs=out_specs,
        out_shape=out_shapes,
        scratch_shapes=[pltpu.VMEM((d_model, 3 * cg), BF16) for _, cg in groups]
        + [pltpu.VMEM((ts // sub, d_model // LANES, sub, LANES), F32),
           pltpu.VMEM((ts // sub, 2, sub, LANES), F32)],
        compiler_params=_params("arbitrary", "arbitrary"),
        name="qkv_rope",
    )(x, g.reshape(1, d_model), cos, sin, w_qkv)


def _band_masks(w, has_prev):
    qi = lax.broadcasted_iota(jnp.int32, (w, 2 * w), 0)
    kj = lax.broadcasted_iota(jnp.int32, (w, 2 * w), 1)
    band = jnp.logical_and(kj >= qi, kj <= qi + w)
    return band, jnp.logical_and(band, jnp.logical_or(kj >= w, has_prev))


def _attn_segment(masks, q_ref, k_ref, v_ref, kp_ref, vp_ref, o_ref, lse_ref, lse_scr, *, w, n_heads):
    band, band_first = masks
    tq = q_ref.shape[0]
    cg = n_heads * HEAD_DIM
    nsub = tq // w
    lane = lax.broadcasted_iota(jnp.int32, (1, LANES), 1)
    nt = (((1,), (1,)), ((), ()))

    units = []
    for c0 in range(0, cg, LANES):
        width = min(LANES, cg - c0)
        for hh in range(width // HEAD_DIM):
            if width == LANES:
                sel = (lane < HEAD_DIM) if hh == 0 else (lane >= HEAD_DIM)
            else:
                sel = None
            units.append((len(units), slice(c0, c0 + width), hh, sel))

    def key_rows(ref, prev_ref, cols, j):
        if j == 0:
            return jnp.concatenate([prev_ref[:, cols], ref[0:w, cols]], axis=0)
        return ref[(j - 1) * w:(j + 1) * w, cols]

    def scores(item):
        (_, cols, _, sel), j = item
        qb = q_ref[j * w:(j + 1) * w, cols]
        if sel is not None:
            qb = jnp.where(sel, qb, jnp.zeros_like(qb))
        return lax.dot_general(qb, key_rows(k_ref, kp_ref, cols, j), nt, preferred_element_type=F32)

    def finish(item, first, last, sc):
        (h, cols, hh, _), j = item
        if first:
            lse_scr[0:tq, :] = jnp.full((tq, LANES), NEG_INF, F32)
        lo = hh * HEAD_DIM
        rows = slice(j * w, (j + 1) * w)
        s = jnp.where(band_first if j == 0 else band, sc, NEG_INF)
        m = jnp.max(s, axis=-1, keepdims=True)
        e = jnp.exp(s - m)
        den = jnp.sum(e, axis=-1, keepdims=True)
        pv = jnp.dot(e.astype(BF16), key_rows(v_ref, vp_ref, cols, j), preferred_element_type=F32) / den
        o_ref[rows, cols.start + lo:cols.start + lo + HEAD_DIM] = pv[:, lo:lo + HEAD_DIM].astype(BF16)
        lse_scr[rows, :] = jnp.where(lane == h, m + jnp.log(den), lse_scr[rows, :])
        if last:
            lse = lse_scr[0:tq, :]
            mx = jnp.max(lse, axis=-1, keepdims=True)
            tot = jnp.sum(jnp.exp(lse - mx), axis=-1, keepdims=True)
            lse_ref[...] = jnp.broadcast_to(mx + jnp.log(tot) - math.log(n_heads), (tq, LANES))

    items = [(unit, j) for unit in units for j in range(nsub)]
    return [(functools.partial(scores, item), functools.partial(finish, item, n == 0, n == len(items) - 1))
            for n, item in enumerate(items)]


def _run_lookahead(work):
    pending = [scores() for scores, _ in work[:ATTN_LOOKAHEAD]]
    for n, (_, finish) in enumerate(work):
        if n + ATTN_LOOKAHEAD < len(work):
            pending.append(work[n + ATTN_LOOKAHEAD][0]())
        finish(pending.pop(0))


def _attn_kernel(q_ref, k_ref, v_ref, kp_ref, vp_ref, o_ref, lse_ref, lse_scr, *, w, n_heads):
    masks = _band_masks(w, pl.program_id(1) > 0)
    work = []
    for sq in range(q_ref.shape[0]):
        work += _attn_segment(masks, q_ref.at[sq], k_ref.at[sq], v_ref.at[sq], kp_ref.at[sq], vp_ref.at[sq],
                              o_ref.at[sq], lse_ref.at[sq], lse_scr.at[sq], w=w, n_heads=n_heads)
    _run_lookahead(work)


def _attn_group(q, k, v, *, w, n_heads, rows=ATTN_STEP_ROWS):
    nseq, l, cg = q.shape
    tq = min(rows, l)
    nb = min(rows // tq, ATTN_STEP_SEQS)
    ratio = tq // w
    kern = functools.partial(_attn_kernel, w=w, n_heads=n_heads)
    cur = pl.BlockSpec((nb, tq, cg), lambda n, i: (n, i, 0))
    prev = pl.BlockSpec((nb, w, cg), lambda n, i: (n, jnp.maximum(i * ratio - 1, 0), 0))
    return pl.pallas_call(
        kern,
        grid=(nseq // nb, l // tq),
        in_specs=[cur, cur, cur, prev, prev],
        out_specs=[cur, pl.BlockSpec((nb, tq, LANES), lambda n, i: (n, i, 0))],
        out_shape=[jax.ShapeDtypeStruct((nseq, l, cg), BF16),
                   jax.ShapeDtypeStruct((nseq, l, LANES), F32)],
        scratch_shapes=[pltpu.VMEM((nb, tq, LANES), F32)],
        compiler_params=_params("arbitrary", "arbitrary"),
        name=f"attn_h{n_heads}_l{l}",
    )(q, k, v, k, v)


def _lane_chunks(blk, cg):
    chunks = []
    for c0 in range(0, cg, LANES):
        piece = blk[:, c0:min(c0 + LANES, cg)]
        if piece.shape[1] < LANES:
            piece = jnp.concatenate([piece, jnp.zeros((blk.shape[0], LANES - piece.shape[1]), F32)], axis=1)
        chunks.append(piece)
    return chunks


def _merge_steps(x_ref, o_refs, l_refs, wout_ref, scratch, x1_ref, *, ts, groups):
    ng = len(groups)
    outs, lses = [], []
    si = 0
    for gi, (d, cg) in enumerate(groups):
        if d == 1:
            outs.append(_lane_chunks(o_refs[gi][0].astype(F32), cg))
            lses.append(l_refs[gi][0])
            continue
        n = ts // d
        o_scr, l_scr = scratch[si], scratch[si + 1]
        si += 2
        for r in range(d):
            for c, piece in enumerate(_lane_chunks(o_refs[gi][r].astype(F32), cg)):
                o_scr[c, pl.ds(r, n, stride=d), :] = piece
            l_scr[pl.ds(r, n, stride=d), :] = l_refs[gi][r]
        outs.append([o_scr[c] for c in range(o_scr.shape[0])])
        lses.append(l_scr[...])
    m = functools.reduce(jnp.maximum, lses)
    es = [jnp.exp(l - m) for l in lses]
    inv = float(ng) / functools.reduce(lambda a, b2: a + b2, es)
    lane = lax.broadcasted_iota(jnp.int32, (1, LANES), 1)
    full, half = [], None
    for (d, cg), chunks, e in zip(groups, outs, es):
        alpha = (e * inv)[:, :1]
        for c, chunk in enumerate(chunks):
            chunk = chunk * alpha
            whole = cg - c * LANES >= LANES
            if half is None:
                if whole:
                    full.append(chunk)
                else:
                    half = chunk
            else:
                rot = pltpu.roll(chunk, HEAD_DIM, axis=1)
                full.append(jnp.where(lane < HEAD_DIM, half, rot))
                half = rot if whole else None
    assert half is None, "attention width must be a multiple of 128"
    merged = jnp.concatenate(full, axis=1).astype(BF16)
    yield
    x1_ref[...] = x_ref[...] + jnp.dot(merged, wout_ref[...].astype(BF16), preferred_element_type=F32)
    yield


def _merge_ffn_kernel(x_ref, *refs, layer, ts, groups, n_tiles):
    ng = len(groups)
    o_refs, l_refs = refs[:ng], refs[ng:2 * ng]
    wout_ref, gf_ref = refs[2 * ng:2 * ng + 2]
    hbm_refs = refs[2 * ng + 2:2 * ng + 5]
    gfin_ref, y_ref, x1_ref = refs[2 * ng + 5:2 * ng + 8]
    (wg_ref, wu_ref, wd_ref), stage_refs, sem = refs[2 * ng + 8:2 * ng + 11], refs[2 * ng + 11:2 * ng + 14], refs[2 * ng + 14]
    scratch = refs[2 * ng + 15:]
    t = pl.program_id(0)
    make_mixer = functools.partial(_merge_steps, x_ref, o_refs, l_refs, wout_ref, scratch, x1_ref,
                                   ts=ts, groups=groups)
    make_ffn = functools.partial(_swiglu_steps, x1_ref, y_ref, gf_ref, wg_ref, wu_ref, wd_ref,
                                 lambda acc: _rmsnorm(acc, gfin_ref[...]))
    load_weights = functools.partial(_load_ffn_weights, layer, hbm_refs, (wg_ref, wu_ref, wd_ref), stage_refs, sem)
    _skewed_step(t, n_tiles, make_mixer, make_ffn, load_weights)


def _merge_ffn_layer(x, outs, lses, w_out, g_ffn, w_gate, w_up, w_down, g_final, groups, *,
                     attn_layer, layer, ts=TOKEN_TILE):
    b, s, d_model = x.shape
    per_batch = s // ts
    n_tiles = b * per_batch
    kern = functools.partial(_merge_ffn_kernel, layer=layer, ts=ts, groups=groups, n_tiles=n_tiles)

    def mixer_spec(block):
        lead = len(block) - 2
        return pl.BlockSpec(block, lambda t: (_tile_index(t, n_tiles, per_batch)[0],) + (0,) * (lead - 1)
                            + (_tile_index(t, n_tiles, per_batch)[1], 0))

    ffn_tile = pl.BlockSpec((None, ts, d_model),
                            lambda t: (*_tile_index(jnp.maximum(t - 1, 0), n_tiles, per_batch), 0))
    o_specs = [mixer_spec((None, d, ts // d, cg)) for d, cg in groups]
    l_specs = [mixer_spec((None, d, ts // d, LANES)) for d, cg in groups]
    scratch = [pltpu.VMEM((ts, d_model), F32)] + _ffn_scratch(d_model, w_gate.shape[-1])
    for d, cg in groups:
        if d > 1:
            scratch += [pltpu.VMEM((pl.cdiv(cg, LANES), ts, LANES), F32), pltpu.VMEM((ts, LANES), F32)]
    return pl.pallas_call(
        kern,
        grid=(n_tiles + 1,),
        in_specs=[mixer_spec((None, ts, d_model))] + o_specs + l_specs + [_layer_spec(w_out, attn_layer)]
        + _ffn_specs(d_model) + [_const_spec((1, d_model))],
        out_specs=ffn_tile,
        out_shape=jax.ShapeDtypeStruct(x.shape, F32),
        scratch_shapes=scratch,
        compiler_params=_params("arbitrary"),
        name="attn_merge_ffn_final",
    )(x, *outs, *lses, w_out, g_ffn.reshape(1, d_model), w_gate, w_up, w_down, g_final.reshape(1, d_model))


def _rope_tables(s):
    half = HEAD_DIM // 2
    inv_freq = 1.0 / (ROPE_THETA ** (np.arange(0, HEAD_DIM, 2, dtype=np.float64) / HEAD_DIM))
    ang = np.arange(s, dtype=np.float64)[:, None] * inv_freq[None, :]
    cos, sin = np.cos(ang), np.sin(ang)
    reps = LANES // half
    cos_t = np.concatenate([cos] * reps, axis=-1)
    sin_t = np.concatenate([-sin, sin] * (reps // 2), axis=-1)
    return jnp.asarray(cos_t, F32), jnp.asarray(sin_t, F32)


def kernel(x, norm_mix, norm_ffn, norm_final, pool_w_in, pool_w_group, pool_scale, pool_w_out,
           attn_w_qkv, attn_w_out, ffn_w_gate, ffn_w_up, ffn_w_down):
    b, s, d_model = x.shape
    assert norm_mix.shape[0] == 2, "two layers: pooling mixer then dilated attention"
    n_heads = attn_w_out.shape[1] // HEAD_DIM
    head_groups = _head_groups(n_heads)
    groups = tuple((dil, nh * HEAD_DIM) for (_, dil), nh in zip(ATTN_PATTERNS, head_groups))
    assert x.dtype == F32 and d_model % LANES == 0 and (n_heads * HEAD_DIM) % LANES == 0
    assert s % TOKEN_TILE == 0 and s % QKV_TILE == 0 and ffn_w_gate.shape[-1] % FFN_CHUNK == 0
    assert all(win // dil == LANES and (s // dil) % LANES == 0 and QKV_SUB_TILE % (BF16_ROWS * dil) == 0
               for win, dil in ATTN_PATTERNS), "banded attention works on 128-row class blocks"

    x = _pool_ffn_layer(x, norm_mix[0], pool_w_in, pool_w_group, pool_scale[0], pool_w_out,
                        norm_ffn[0], ffn_w_gate, ffn_w_up, ffn_w_down, layer=0)

    cos_t, sin_t = _rope_tables(s)
    qkv = _qkv_layer(x, norm_mix[1], cos_t, sin_t, attn_w_qkv, groups, layer=0)
    outs, lses = [], []
    for gi, ((window, dil), nh) in enumerate(zip(ATTN_PATTERNS, head_groups)):
        cg = nh * HEAD_DIM
        q, k, v = (t.reshape(b * dil, s // dil, cg) for t in qkv[3 * gi:3 * gi + 3])
        o, lse = _attn_group(q, k, v, w=window // dil, n_heads=nh)
        outs.append(o.reshape(b, dil, s // dil, cg))
        lses.append(lse.reshape(b, dil, s // dil, LANES))
    return _merge_ffn_layer(x, outs, lses, attn_w_out, norm_ffn[1], ffn_w_gate, ffn_w_up, ffn_w_down,
                            norm_final, groups, attn_layer=0, layer=1)
```

```python
import functools
import math

import jax
import jax.numpy as jnp
import numpy as np
from jax import lax
from jax.experimental import pallas as pl
from jax.experimental.pallas import tpu as pltpu

EPS = 1e-6
POOL_WINDOWS = (2, 4, 8, 16)
HEAD_DIM = 64
ATTN_PATTERNS = ((128, 1), (512, 4), (2048, 16))
ROPE_THETA = 10000.0
NEG_INF = -1e30

LANES = 128
BF16_ROWS = 16
VMEM_LIMIT_BYTES = 58 * 1024 * 1024
TOKEN_TILE = 512
QKV_TILE = 1024
QKV_SUB_TILE = 256
ATTN_STEP_ROWS = 2048
ATTN_STEP_SEQS = 2
ATTN_LOOKAHEAD = 2
FFN_CHUNK = 256
MIXER_GAP = 3
DEINT_STRIDE = 4

F32 = jnp.float32
BF16 = jnp.bfloat16


def _head_groups(n_heads):
    n = len(ATTN_PATTERNS)
    return tuple(n_heads // n + (1 if g < n_heads % n else 0) for g in range(n))


def _rmsnorm(x, g):
    ms = jnp.mean(x * x, axis=-1, keepdims=True)
    return x * lax.rsqrt(ms + EPS) * g


def _params(*sem):
    return pltpu.CompilerParams(dimension_semantics=sem, vmem_limit_bytes=VMEM_LIMIT_BYTES)


def _const_spec(shape):
    zeros = (0,) * len(shape)
    return pl.BlockSpec(shape, lambda *_: zeros, pipeline_mode=pl.Buffered(1))


def _layer_spec(stacked, layer):
    idx = (layer,) + (0,) * (stacked.ndim - 1)
    return pl.BlockSpec((None,) + stacked.shape[1:], lambda *_: idx, pipeline_mode=pl.Buffered(1))


def _swiglu_steps(x1_ref, o_ref, g_ref, wg_ref, wu_ref, wd_ref, act_ref, finish):
    x1 = x1_ref[...]
    h = _rmsnorm(x1, g_ref[...]).astype(BF16)
    n_chunks = wg_ref.shape[-1] // FFN_CHUNK
    for c in range(n_chunks):
        sl = slice(c * FFN_CHUNK, (c + 1) * FFN_CHUNK)
        gate = jnp.dot(h, wg_ref[:, sl], preferred_element_type=F32)
        up = jnp.dot(h, wu_ref[:, sl], preferred_element_type=F32)
        act_ref[:, sl] = (gate * jax.nn.sigmoid(gate) * up).astype(BF16)
        if c == n_chunks - 1:
            down = jnp.dot(act_ref[...], wd_ref[...], preferred_element_type=F32)
            o_ref[...] = finish(x1 + down)
        yield


def _load_ffn_weights(layer, hbm_refs, bf_refs, stage_refs, sem, meanwhile):
    n_chunks = hbm_refs[0].shape[-1] // FFN_CHUNK

    def chunk(c):
        return slice(c * FFN_CHUNK, (c + 1) * FFN_CHUNK)

    def copies(c):
        srcs = (hbm_refs[0].at[layer, :, chunk(c)], hbm_refs[1].at[layer, :, chunk(c)],
                hbm_refs[2].at[layer, chunk(c), :])
        return [pltpu.make_async_copy(src, stage.at[c % 2], sem.at[k, c % 2])
                for k, (src, stage) in enumerate(zip(srcs, stage_refs))]

    for cp in copies(0):
        cp.start()
    meanwhile()
    for c in range(n_chunks):
        if c + 1 < n_chunks:
            for cp in copies(c + 1):
                cp.start()
        for cp in copies(c):
            cp.wait()
        bf_refs[0][:, chunk(c)] = stage_refs[0][c % 2].astype(BF16)
        bf_refs[1][:, chunk(c)] = stage_refs[1][c % 2].astype(BF16)
        bf_refs[2][chunk(c), :] = stage_refs[2][c % 2].astype(BF16)


def _skewed_step(t, n_tiles, make_mixer, make_ffn, load_weights):
    @pl.when(t == 0)
    def _():
        def run_mixer():
            for _ in make_mixer():
                pass
        load_weights(run_mixer)

    @pl.when(t == n_tiles)
    def _():
        for _ in make_ffn():
            pass

    @pl.when(jnp.logical_and(t > 0, t < n_tiles))
    def _():
        mixer = make_mixer()
        for c, _ in enumerate(make_ffn()):
            if c % MIXER_GAP == MIXER_GAP - 1:
                next(mixer, None)
        for _ in mixer:
            pass


def _ffn_specs(d):
    hbm = pl.BlockSpec(memory_space=pl.ANY)
    return [_const_spec((1, d)), hbm, hbm, hbm]


def _ffn_scratch(d, f):
    return [pltpu.VMEM((d, f), BF16), pltpu.VMEM((d, f), BF16), pltpu.VMEM((f, d), BF16),
            pltpu.VMEM((2, d, FFN_CHUNK), F32), pltpu.VMEM((2, d, FFN_CHUNK), F32),
            pltpu.VMEM((2, FFN_CHUNK, d), F32), pltpu.SemaphoreType.DMA((3, 2)),
            pltpu.VMEM((TOKEN_TILE, f), BF16)]


def _tile_index(t, n_tiles, per_batch):
    tt = jnp.minimum(t, n_tiles - 1)
    return tt // per_batch, tt % per_batch


def _pool_steps(i, x_ref, g_ref, win_ref, wgrp_ref, scale_ref, wout_ref, ext_ref, x1_ref, *, ts, halo):
    d_model = x_ref.shape[-1]
    gdim = d_model // len(POOL_WINDOWS)
    h = _rmsnorm(x_ref[...], g_ref[...]).astype(BF16)
    ext_ref[halo:halo + ts, :] = jnp.dot(h, win_ref[...].astype(BF16), preferred_element_type=F32)
    yield
    pos = i * ts + lax.broadcasted_iota(jnp.int32, (ts, 1), 0)
    zs = []
    for g, w in enumerate(POOL_WINDOWS):
        cols = slice(g * gdim, (g + 1) * gdim)
        s = ext_ref[:, cols]
        k = 1
        while k < w:
            s = s + pltpu.roll(s, k, axis=0)
            k *= 2
        inv_cnt = 1.0 / jnp.minimum(pos + 1, w).astype(F32)
        p = s[halo:, :] * inv_cnt - ext_ref[halo:halo + ts, cols]
        zs.append(jnp.dot(p.astype(BF16), wgrp_ref[g].astype(BF16), preferred_element_type=F32))
    yield
    z = jnp.concatenate(zs, axis=1) * scale_ref[...]
    y = jnp.dot(z.astype(BF16), wout_ref[...].astype(BF16), preferred_element_type=F32)
    ext_ref[0:halo, :] = ext_ref[ts:ts + halo, :]
    x1_ref[...] = x_ref[...] + y
    yield


def _pool_ffn_kernel(x_ref, g_ref, win_ref, wgrp_ref, scale_ref, wout_ref, gf_ref, *refs,
                     layer, ts, halo, n_tiles, per_batch):
    hbm_refs, (o_ref, ext_ref, x1_ref) = refs[:3], refs[3:6]
    (wg_ref, wu_ref, wd_ref), stage_refs, sem, act_ref = refs[6:9], refs[9:12], refs[12], refs[13]
    t = pl.program_id(0)
    i = jnp.minimum(t, n_tiles - 1) % per_batch

    @pl.when(i == 0)
    def _():
        ext_ref[0:halo, :] = jnp.zeros((halo, x_ref.shape[-1]), F32)

    make_mixer = functools.partial(_pool_steps, i, x_ref, g_ref, win_ref, wgrp_ref, scale_ref, wout_ref,
                                   ext_ref, x1_ref, ts=ts, halo=halo)
    make_ffn = functools.partial(_swiglu_steps, x1_ref, o_ref, gf_ref, wg_ref, wu_ref, wd_ref, act_ref,
                                 lambda acc: acc)
    load_weights = functools.partial(_load_ffn_weights, layer, hbm_refs, (wg_ref, wu_ref, wd_ref), stage_refs, sem)
    _skewed_step(t, n_tiles, make_mixer, make_ffn, load_weights)


def _pool_ffn_layer(x, g_mix, w_in, w_group, scale, w_out, g_ffn, w_gate, w_up, w_down, *, layer, ts=TOKEN_TILE):
    b, s, d = x.shape
    halo = max(POOL_WINDOWS)
    per_batch = s // ts
    n_tiles = b * per_batch
    kern = functools.partial(_pool_ffn_kernel, layer=layer, ts=ts, halo=halo, n_tiles=n_tiles, per_batch=per_batch)
    mixer_tile = pl.BlockSpec((None, ts, d), lambda t: (*_tile_index(t, n_tiles, per_batch), 0))
    ffn_tile = pl.BlockSpec((None, ts, d), lambda t: (*_tile_index(jnp.maximum(t - 1, 0), n_tiles, per_batch), 0))
    return pl.pallas_call(
        kern,
        grid=(n_tiles + 1,),
        in_specs=[mixer_tile, _const_spec((1, d)), _layer_spec(w_in, layer), _layer_spec(w_group, layer),
                  _const_spec((1, d)), _layer_spec(w_out, layer)] + _ffn_specs(d),
        out_specs=ffn_tile,
        out_shape=jax.ShapeDtypeStruct(x.shape, F32),
        scratch_shapes=[pltpu.VMEM((ts + halo, d), F32), pltpu.VMEM((ts, d), F32)]
        + _ffn_scratch(d, w_gate.shape[-1]),
        compiler_params=_params("arbitrary"),
        name="pool_ffn",
    )(x, g_mix.reshape(1, d), w_in, w_group, scale.reshape(1, d), w_out, g_ffn.reshape(1, d), w_gate, w_up, w_down)


def _class_order(d):
    order, cur = (0,), 1
    while cur < d:
        order = tuple(c + cur * r for c in order for r in range(DEINT_STRIDE))
        cur *= DEINT_STRIDE
    assert cur == d, "dilations must be powers of DEINT_STRIDE"
    return order


def _split_rows(src, n_blocks, blk):
    n = blk // DEINT_STRIDE
    return jnp.concatenate([src[pl.ds(b0 * blk + r, n, stride=DEINT_STRIDE), :]
                            for b0 in range(n_blocks) for r in range(DEINT_STRIDE)], axis=0)


def _regroup_qkv_weights(w_ref, wb_refs, groups):
    d_model, d_attn = w_ref.shape[0], w_ref.shape[1] // 3
    half = LANES // 2
    lane = lax.broadcasted_iota(jnp.int32, (1, LANES), 1)

    def body(rb, carry):
        rows = pl.ds(pl.multiple_of(rb * LANES, LANES), LANES)

        def half_of(src, upper):
            ci = src // LANES
            v = w_ref[rows, ci * LANES:(ci + 1) * LANES]
            if src < d_attn:
                v = v * HEAD_DIM ** -0.5
            return v if (src % LANES == half) == upper else pltpu.roll(v, half, axis=1)

        col = 0
        for wb_ref, (_, cg) in zip(wb_refs, groups):
            runs = [p * d_attn + col + i * half for p in range(3) for i in range(cg // half)]
            for k in range(0, len(runs), 2):
                lo = half_of(runs[k], False)
                if k + 1 < len(runs):
                    chunk = jnp.where(lane < half, lo, half_of(runs[k + 1], True))
                    wb_ref[rows, k * half:(k + 2) * half] = chunk.astype(BF16)
                else:
                    wb_ref[rows, k * half:(k + 1) * half] = lo[:, :half].astype(BF16)
            col += cg
        return carry

    lax.fori_loop(0, d_model // LANES, body, 0)


def _qkv_kernel(x_ref, g_ref, cos_ref, sin_ref, w_ref, *refs, ts, sub, groups):
    ng = len(groups)
    out_refs, w_refs = refs[:3 * ng], refs[3 * ng:4 * ng]
    hs_all, tab_all = refs[4 * ng:]
    lane = lax.broadcasted_iota(jnp.int32, (1, LANES), 1)
    first_half = (lane % HEAD_DIM) < (HEAD_DIM // 2)
    n_chunks = x_ref.shape[-1] // LANES

    @pl.when(jnp.logical_and(pl.program_id(0) == 0, pl.program_id(1) == 0))
    def _():
        _regroup_qkv_weights(w_ref, w_refs, groups)

    for si, s0 in enumerate(range(0, ts, sub)):
        hs_ref, tab_ref = hs_all.at[si], tab_all.at[si]
        h = _rmsnorm(x_ref[s0:s0 + sub, :], g_ref[...])
        cs, sn = cos_ref[s0:s0 + sub, :], sin_ref[s0:s0 + sub, :]
        cur_d = 1
        for gi, (d, cg) in enumerate(groups):
            while cur_d < d:
                for c in range(n_chunks):
                    hs_ref[c] = h[:, c * LANES:(c + 1) * LANES]
                tab_ref[0], tab_ref[1] = cs, sn
                blk = sub // cur_d
                h = jnp.concatenate([_split_rows(hs_ref.at[c], cur_d, blk) for c in range(n_chunks)], axis=1)
                cs, sn = _split_rows(tab_ref.at[0], cur_d, blk), _split_rows(tab_ref.at[1], cur_d, blk)
                cur_d *= DEINT_STRIDE
            assert cur_d == d, "head groups must come in increasing powers of DEINT_STRIDE"
            qkv = jnp.dot(h.astype(BF16), w_refs[gi][...], preferred_element_type=F32)
            chunks = []
            for c in range(2 * cg // LANES):
                t = qkv[:, c * LANES:(c + 1) * LANES]
                swapped = jnp.where(first_half, pltpu.roll(t, LANES - HEAD_DIM // 2, axis=1),
                                    pltpu.roll(t, HEAD_DIM // 2, axis=1))
                chunks.append(t * cs + swapped * sn)
            qk = jnp.concatenate(chunks, axis=1)
            parts = (qk[:, :cg], qk[:, cg:2 * cg], qkv[:, 2 * cg:])
            n = sub // d
            for part, o_ref in zip(parts, out_refs[3 * gi:3 * gi + 3]):
                pb = part.astype(BF16)
                for slot, r in enumerate(_class_order(d)):
                    o_ref[r, si * n:(si + 1) * n, :] = pb[slot * n:(slot + 1) * n, :]


def _qkv_layer(x, g, cos, sin, w_qkv, groups, *, layer, ts=QKV_TILE, sub=QKV_SUB_TILE):
    b, s, d_model = x.shape
    kern = functools.partial(_qkv_kernel, ts=ts, sub=sub, groups=groups)
    out_shapes, out_specs = [], []
    for d, cg in groups:
        for _ in range(3):
            out_shapes.append(jax.ShapeDtypeStruct((b, d, s // d, cg), BF16))
            out_specs.append(pl.BlockSpec((None, d, ts // d, cg), lambda bi, i: (bi, 0, i, 0)))
    table_spec = pl.BlockSpec((ts, LANES), lambda bi, i: (i, 0))
    return pl.pallas_call(
        kern,
        grid=(b, s // ts),
        in_specs=[pl.BlockSpec((None, ts, d_model), lambda bi, i: (bi, i, 0)), _const_spec((1, d_model)),
                  table_spec, table_spec, _layer_spec(w_qkv, layer)],
        out_specs=out_specs,
        out_shape=out_shapes,
        scratch_shapes=[pltpu.VMEM((d_model, 3 * cg), BF16) for _, cg in groups]
        + [pltpu.VMEM((ts // sub, d_model // LANES, sub, LANES), F32),
           pltpu.VMEM((ts // sub, 2, sub, LANES), F32)],
        compiler_params=_params("arbitrary", "arbitrary"),
        name="qkv_rope",
    )(x, g.reshape(1, d_model), cos, sin, w_qkv)


def _band_masks(w, has_prev):
    qi = lax.broadcasted_iota(jnp.int32, (w, 2 * w), 0)
    kj = lax.broadcasted_iota(jnp.int32, (w, 2 * w), 1)
    band = jnp.logical_and(kj >= qi, kj <= qi + w)
    return band, jnp.logical_and(band, jnp.logical_or(kj >= w, has_prev))


def _attn_segment(masks, q_ref, k_ref, v_ref, kp_ref, vp_ref, o_ref, lse_ref, lse_scr, *, w, n_heads):
    band, band_first = masks
    tq = q_ref.shape[0]
    cg = n_heads * HEAD_DIM
    nsub = tq // w
    lane = lax.broadcasted_iota(jnp.int32, (1, LANES), 1)
    nt = (((1,), (1,)), ((), ()))

    units = []
    for c0 in range(0, cg, LANES):
        width = min(LANES, cg - c0)
        for hh in range(width // HEAD_DIM):
            if width == LANES:
                sel = (lane < HEAD_DIM) if hh == 0 else (lane >= HEAD_DIM)
            else:
                sel = None
            units.append((len(units), slice(c0, c0 + width), hh, sel))

    def key_rows(ref, prev_ref, cols, j):
        if j == 0:
            return jnp.concatenate([prev_ref[:, cols], ref[0:w, cols]], axis=0)
        return ref[(j - 1) * w:(j + 1) * w, cols]

    def scores(item):
        (_, cols, _, sel), j = item
        qb = q_ref[j * w:(j + 1) * w, cols]
        if sel is not None:
            qb = jnp.where(sel, qb, jnp.zeros_like(qb))
        return lax.dot_general(qb, key_rows(k_ref, kp_ref, cols, j), nt, preferred_element_type=F32)

    def finish(item, first, last, sc):
        (h, cols, hh, _), j = item
        if first:
            lse_scr[0:tq, :] = jnp.full((tq, LANES), NEG_INF, F32)
        lo = hh * HEAD_DIM
        rows = slice(j * w, (j + 1) * w)
        s = jnp.where(band_first if j == 0 else band, sc, NEG_INF)
        m = jnp.max(s, axis=-1, keepdims=True)
        e = jnp.exp(s - m)
        den = jnp.sum(e, axis=-1, keepdims=True)
        pv = jnp.dot(e.astype(BF16), key_rows(v_ref, vp_ref, cols, j), preferred_element_type=F32) / den
        o_ref[rows, cols.start + lo:cols.start + lo + HEAD_DIM] = pv[:, lo:lo + HEAD_DIM].astype(BF16)
        lse_scr[rows, :] = jnp.where(lane == h, m + jnp.log(den), lse_scr[rows, :])
        if last:
            lse = lse_scr[0:tq, :]
            mx = jnp.max(lse, axis=-1, keepdims=True)
            tot = jnp.sum(jnp.exp(lse - mx), axis=-1, keepdims=True)
            lse_ref[...] = jnp.broadcast_to(mx + jnp.log(tot) - math.log(n_heads), (tq, LANES))

    items = [(unit, j) for unit in units for j in range(nsub)]
    return [(functools.partial(scores, item), functools.partial(finish, item, n == 0, n == len(items) - 1))
            for n, item in enumerate(items)]


def _run_lookahead(work):
    pending = [scores() for scores, _ in work[:ATTN_LOOKAHEAD]]
    for n, (_, finish) in enumerate(work):
        if n + ATTN_LOOKAHEAD < len(work):
            pending.append(work[n + ATTN_LOOKAHEAD][0]())
        finish(pending.pop(0))


def _attn_kernel(q_ref, k_ref, v_ref, kp_ref, vp_ref, o_ref, lse_ref, lse_scr, *, w, n_heads):
    masks = _band_masks(w, pl.program_id(1) > 0)
    work = []
    for sq in range(q_ref.shape[0]):
        work += _attn_segment(masks, q_ref.at[sq], k_ref.at[sq], v_ref.at[sq], kp_ref.at[sq], vp_ref.at[sq],
                              o_ref.at[sq], lse_ref.at[sq], lse_scr.at[sq], w=w, n_heads=n_heads)
    _run_lookahead(work)


def _attn_group(q, k, v, *, w, n_heads, rows=ATTN_STEP_ROWS):
    nseq, l, cg = q.shape
    tq = min(rows, l)
    nb = min(rows // tq, ATTN_STEP_SEQS)
    ratio = tq // w
    kern = functools.partial(_attn_kernel, w=w, n_heads=n_heads)
    cur = pl.BlockSpec((nb, tq, cg), lambda n, i: (n, i, 0))
    prev = pl.BlockSpec((nb, w, cg), lambda n, i: (n, jnp.maximum(i * ratio - 1, 0), 0))
    return pl.pallas_call(
        kern,
        grid=(nseq // nb, l // tq),
        in_specs=[cur, cur, cur, prev, prev],
        out_specs=[cur, pl.BlockSpec((nb, tq, LANES), lambda n, i: (n, i, 0))],
        out_shape=[jax.ShapeDtypeStruct((nseq, l, cg), BF16),
                   jax.ShapeDtypeStruct((nseq, l, LANES), F32)],
        scratch_shapes=[pltpu.VMEM((nb, tq, LANES), F32)],
        compiler_params=_params("arbitrary", "arbitrary"),
        name=f"attn_h{n_heads}_l{l}",
    )(q, k, v, k, v)


def _lane_chunks(blk, cg):
    chunks = []
    for c0 in range(0, cg, LANES):
        piece = blk[:, c0:min(c0 + LANES, cg)]
        if piece.shape[1] < LANES:
            piece = jnp.concatenate([piece, jnp.zeros((blk.shape[0], LANES - piece.shape[1]), F32)], axis=1)
        chunks.append(piece)
    return chunks


def _merge_steps(x_ref, o_refs, l_refs, wout_ref, scratch, x1_ref, *, ts, groups):
    ng = len(groups)
    outs, lses = [], []
    si = 0
    for gi, (d, cg) in enumerate(groups):
        if d == 1:
            outs.append(_lane_chunks(o_refs[gi][0].astype(F32), cg))
            lses.append(l_refs[gi][0])
            continue
        n = ts // d
        o_scr, l_scr = scratch[si], scratch[si + 1]
        si += 2
        for r in range(d):
            for c, piece in enumerate(_lane_chunks(o_refs[gi][r].astype(F32), cg)):
                o_scr[c, pl.ds(r, n, stride=d), :] = piece
            l_scr[pl.ds(r, n, stride=d), :] = l_refs[gi][r]
        outs.append([o_scr[c] for c in range(o_scr.shape[0])])
        lses.append(l_scr[...])
    m = functools.reduce(jnp.maximum, lses)
    es = [jnp.exp(l - m) for l in lses]
    inv = float(ng) / functools.reduce(lambda a, b2: a + b2, es)
    lane = lax.broadcasted_iota(jnp.int32, (1, LANES), 1)
    full, half = [], None
    for (d, cg), chunks, e in zip(groups, outs, es):
        alpha = (e * inv)[:, :1]
        for c, chunk in enumerate(chunks):
            chunk = chunk * alpha
            whole = cg - c * LANES >= LANES
            if half is None:
                if whole:
                    full.append(chunk)
                else:
                    half = chunk
            else:
                rot = pltpu.roll(chunk, HEAD_DIM, axis=1)
                full.append(jnp.where(lane < HEAD_DIM, half, rot))
                half = rot if whole else None
    assert half is None, "attention width must be a multiple of 128"
    merged = jnp.concatenate(full, axis=1).astype(BF16)
    yield
    x1_ref[...] = x_ref[...] + jnp.dot(merged, wout_ref[...].astype(BF16), preferred_element_type=F32)
    yield


def _merge_ffn_kernel(x_ref, *refs, layer, ts, groups, n_tiles):
    ng = len(groups)
    o_refs, l_refs = refs[:ng], refs[ng:2 * ng]
    wout_ref, gf_ref = refs[2 * ng:2 * ng + 2]
    hbm_refs = refs[2 * ng + 2:2 * ng + 5]
    gfin_ref, y_ref, x1_ref = refs[2 * ng + 5:2 * ng + 8]
    (wg_ref, wu_ref, wd_ref), stage_refs, sem = refs[2 * ng + 8:2 * ng + 11], refs[2 * ng + 11:2 * ng + 14], refs[2 * ng + 14]
    act_ref = refs[2 * ng + 15]
    scratch = refs[2 * ng + 16:]
    t = pl.program_id(0)
    make_mixer = functools.partial(_merge_steps, x_ref, o_refs, l_refs, wout_ref, scratch, x1_ref,
                                   ts=ts, groups=groups)
    make_ffn = functools.partial(_swiglu_steps, x1_ref, y_ref, gf_ref, wg_ref, wu_ref, wd_ref, act_ref,
                                 lambda acc: _rmsnorm(acc, gfin_ref[...]))
    load_weights = functools.partial(_load_ffn_weights, layer, hbm_refs, (wg_ref, wu_ref, wd_ref), stage_refs, sem)
    _skewed_step(t, n_tiles, make_mixer, make_ffn, load_weights)


def _merge_ffn_layer(x, outs, lses, w_out, g_ffn, w_gate, w_up, w_down, g_final, groups, *,
                     attn_layer, layer, ts=TOKEN_TILE):
    b, s, d_model = x.shape
    per_batch = s // ts
    n_tiles = b * per_batch
    kern = functools.partial(_merge_ffn_kernel, layer=layer, ts=ts, groups=groups, n_tiles=n_tiles)

    def mixer_spec(block):
        lead = len(block) - 2
        return pl.BlockSpec(block, lambda t: (_tile_index(t, n_tiles, per_batch)[0],) + (0,) * (lead - 1)
                            + (_tile_index(t, n_tiles, per_batch)[1], 0))

    ffn_tile = pl.BlockSpec((None, ts, d_model),
                            lambda t: (*_tile_index(jnp.maximum(t - 1, 0), n_tiles, per_batch), 0))
    o_specs = [mixer_spec((None, d, ts // d, cg)) for d, cg in groups]
    l_specs = [mixer_spec((None, d, ts // d, LANES)) for d, cg in groups]
    scratch = [pltpu.VMEM((ts, d_model), F32)] + _ffn_scratch(d_model, w_gate.shape[-1])
    for d, cg in groups:
        if d > 1:
            scratch += [pltpu.VMEM((pl.cdiv(cg, LANES), ts, LANES), F32), pltpu.VMEM((ts, LANES), F32)]
    return pl.pallas_call(
        kern,
        grid=(n_tiles + 1,),
        in_specs=[mixer_spec((None, ts, d_model))] + o_specs + l_specs + [_layer_spec(w_out, attn_layer)]
        + _ffn_specs(d_model) + [_const_spec((1, d_model))],
        out_specs=ffn_tile,
        out_shape=jax.ShapeDtypeStruct(x.shape, F32),
        scratch_shapes=scratch,
        compiler_params=_params("arbitrary"),
        name="attn_merge_ffn_final",
    )(x, *outs, *lses, w_out, g_ffn.reshape(1, d_model), w_gate, w_up, w_down, g_final.reshape(1, d_model))


def _rope_tables(s):
    half = HEAD_DIM // 2
    inv_freq = 1.0 / (ROPE_THETA ** (np.arange(0, HEAD_DIM, 2, dtype=np.float64) / HEAD_DIM))
    ang = np.arange(s, dtype=np.float64)[:, None] * inv_freq[None, :]
    cos, sin = np.cos(ang), np.sin(ang)
    reps = LANES // half
    cos_t = np.concatenate([cos] * reps, axis=-1)
    sin_t = np.concatenate([-sin, sin] * (reps // 2), axis=-1)
    return jnp.asarray(cos_t, F32), jnp.asarray(sin_t, F32)


def kernel(x, norm_mix, norm_ffn, norm_final, pool_w_in, pool_w_group, pool_scale, pool_w_out,
           attn_w_qkv, attn_w_out, ffn_w_gate, ffn_w_up, ffn_w_down):
    b, s, d_model = x.shape
    assert norm_mix.shape[0] == 2, "two layers: pooling mixer then dilated attention"
    n_heads = attn_w_out.shape[1] // HEAD_DIM
    head_groups = _head_groups(n_heads)
    groups = tuple((dil, nh * HEAD_DIM) for (_, dil), nh in zip(ATTN_PATTERNS, head_groups))
    assert x.dtype == F32 and d_model % LANES == 0 and (n_heads * HEAD_DIM) % LANES == 0
    assert s % TOKEN_TILE == 0 and s % QKV_TILE == 0 and ffn_w_gate.shape[-1] % FFN_CHUNK == 0
    assert all(win // dil == LANES and (s // dil) % LANES == 0 and QKV_SUB_TILE % (BF16_ROWS * dil) == 0
               for win, dil in ATTN_PATTERNS), "banded attention works on 128-row class blocks"

    x = _pool_ffn_layer(x, norm_mix[0], pool_w_in, pool_w_group, pool_scale[0], pool_w_out,
                        norm_ffn[0], ffn_w_gate, ffn_w_up, ffn_w_down, layer=0)

    cos_t, sin_t = _rope_tables(s)
    qkv = _qkv_layer(x, norm_mix[1], cos_t, sin_t, attn_w_qkv, groups, layer=0)
    outs, lses = [], []
    for gi, ((window, dil), nh) in enumerate(zip(ATTN_PATTERNS, head_groups)):
        cg = nh * HEAD_DIM
        q, k, v = (t.reshape(b * dil, s // dil, cg) for t in qkv[3 * gi:3 * gi + 3])
        o, lse = _attn_group(q, k, v, w=window // dil, n_heads=nh)
        outs.append(o.reshape(b, dil, s // dil, cg))
        lses.append(lse.reshape(b, dil, s // dil, LANES))
    return _merge_ffn_layer(x, outs, lses, attn_w_out, norm_ffn[1], ffn_w_gate, ffn_w_up, ffn_w_down,
                            norm_final, groups, attn_layer=0, layer=1)
```
